```python
import math
import jax
import jax.numpy as jnp
from jax import lax
import numpy as np

D_MODEL = 1024
BATCH = 2
SEQ = 8192
DEPTH = 4
DEC_BATCH = 32
DEC_SEQ = 1
PAST_LEN = 8192
PAGE_SIZE = 128

NORM_EPS = 1e-6
ROPE_THETA = 10000.0
N_MIXERS = 2
N_DN_LAYERS = (DEPTH + 1) // N_MIXERS
N_MB_LAYERS = DEPTH // N_MIXERS
DN_QK_HEADS = 8
DN_V_HEADS = 16
DN_HEAD_DIM = 128
DN_CONV = 4
DN_CHUNK = 64
DN_KEY_DIM = DN_QK_HEADS * DN_HEAD_DIM
DN_VAL_DIM = DN_V_HEADS * DN_HEAD_DIM
DN_CONV_DIM = 2 * DN_KEY_DIM + DN_VAL_DIM
DN_PROJ_DIM = DN_CONV_DIM + DN_VAL_DIM + 2 * DN_V_HEADS
MB_HEADS = 8
MB_HEAD_DIM = D_MODEL // MB_HEADS
MB_BLOCK = 256
MB_TOP_K = 3
MB_Q_CHUNK = 32
D_FF = 2816
FFN_CONV = 3

kernel_name = 'hybrid_gdn_moba_convffn_step'


def rms_norm(x, g):
    xf = x.astype(jnp.float32)
    y = xf * lax.rsqrt(jnp.mean(xf * xf, axis=-1, keepdims=True) + NORM_EPS)
    return (y * g.astype(jnp.float32)).astype(x.dtype)


def l2_norm(x):
    return x * lax.rsqrt(jnp.sum(x * x, axis=-1, keepdims=True) + NORM_EPS)


def rope(x, pos):
    half = x.shape[-1] // 2
    inv = ROPE_THETA ** (-jnp.arange(half, dtype=jnp.float32) / half)
    ang = pos.astype(jnp.float32)[:, None] * inv[None, :]
    cos, sin = jnp.cos(ang)[:, None, :], jnp.sin(ang)[:, None, :]
    xf = x.astype(jnp.float32)
    x1, x2 = xf[..., :half], xf[..., half:]
    return jnp.concatenate([x1 * cos - x2 * sin, x2 * cos + x1 * sin], axis=-1).astype(x.dtype)


def causal_dwconv(x, buf, w):
    width, t = w.shape[0], x.shape[1]
    xp = jnp.concatenate([buf.astype(x.dtype), x], axis=1)
    y = xp[:, 0:t] * w[0]
    for i in range(1, width):
        y = y + xp[:, i:i + t] * w[i]
    return y, xp[:, t:]


def gated_delta_chunked(q, k, v, beta, g, s0, chunk):
    b, t, nh, dk = q.shape
    dv = v.shape[-1]
    nc = -(-t // chunk)
    pad = nc * chunk - t

    def blk4(a):
        a = jnp.pad(a, ((0, 0), (0, pad), (0, 0), (0, 0)))
        return a.reshape(b, nc, chunk, nh, a.shape[-1]).transpose(0, 1, 3, 2, 4)

    def blk3(a):
        a = jnp.pad(a, ((0, 0), (0, pad), (0, 0)))
        return a.reshape(b, nc, chunk, nh).transpose(0, 1, 3, 2)

    q, k, v = blk4(q), blk4(k), blk4(v)
    beta, g = blk3(beta), blk3(g)
    gc = jnp.cumsum(g, axis=-1)
    incl = jnp.tril(jnp.ones((chunk, chunk), dtype=bool))
    strict = jnp.tril(jnp.ones((chunk, chunk), dtype=bool), k=-1)
    decay = jnp.exp(jnp.where(incl, gc[..., :, None] - gc[..., None, :], -jnp.inf))
    kk = jnp.einsum('bnhid,bnhjd->bnhij', k, k)
    a_mat = jnp.where(strict, beta[..., :, None] * decay * kk, 0.0) + jnp.eye(chunk, dtype=jnp.float32)
    rhs = jnp.concatenate([beta[..., None] * v, (beta * jnp.exp(gc))[..., None] * k], axis=-1)
    sol = lax.linalg.triangular_solve(a_mat, rhs, left_side=True, lower=True, unit_diagonal=True)
    u0, w = sol[..., :dv], sol[..., dv:]
    qk = jnp.einsum('bnhid,bnhjd->bnhij', q, k) * decay
    qg = q * jnp.exp(gc)[..., None]
    g_last = gc[..., -1]
    kd = k * jnp.exp(g_last[..., None] - gc)[..., None]
    gl = jnp.exp(g_last)
    xs = tuple(jnp.moveaxis(a, 1, 0) for a in (u0, w, qk, qg, kd, gl))

    def step(s, inp):
        u0c, wc, qkc, qgc, kdc, glc = inp
        u = u0c - jnp.einsum('bhcd,bhde->bhce', wc, s)
        o = jnp.einsum('bhcd,bhde->bhce', qgc, s) + jnp.einsum('bhij,bhje->bhie', qkc, u)
        s = glc[..., None, None] * s + jnp.einsum('bhcd,bhce->bhde', kdc, u)
        return s, o

    s, o = lax.scan(step, s0.astype(jnp.float32), xs)
    o = o.transpose(1, 0, 3, 2, 4).reshape(b, nc * chunk, nh, dv)[:, :t]
    return o, s


def deltanet_mixer(h, conv_buf, s0, w_in, conv_w, a_log, dt_bias, norm_w, w_out):
    bsz, t, _ = h.shape
    f32 = jnp.float32
    proj = h @ w_in
    o1 = DN_CONV_DIM
    o2 = o1 + DN_VAL_DIM
    o3 = o2 + DN_V_HEADS
    qkv, new_buf = causal_dwconv(proj[..., :o1], conv_buf, conv_w)
    qkv = jax.nn.silu(qkv.astype(f32))
    rep = DN_V_HEADS // DN_QK_HEADS
    q = qkv[..., :DN_KEY_DIM].reshape(bsz, t, DN_QK_HEADS, DN_HEAD_DIM)
    k = qkv[..., DN_KEY_DIM:2 * DN_KEY_DIM].reshape(bsz, t, DN_QK_HEADS, DN_HEAD_DIM)
    v = qkv[..., 2 * DN_KEY_DIM:].reshape(bsz, t, DN_V_HEADS, DN_HEAD_DIM)
    q = jnp.repeat(l2_norm(q) * DN_HEAD_DIM ** -0.5, rep, axis=2)
    k = jnp.repeat(l2_norm(k), rep, axis=2)
    beta = jax.nn.sigmoid(proj[..., o2:o3].astype(f32))
    g = -jnp.exp(a_log.astype(f32)) * jax.nn.softplus(proj[..., o3:].astype(f32) + dt_bias.astype(f32))
    o, s = gated_delta_chunked(q, k, v, beta, g, s0, min(DN_CHUNK, t))
    z = proj[..., o1:o2].astype(f32).reshape(bsz, t, DN_V_HEADS, DN_HEAD_DIM)
    o = o * lax.rsqrt(jnp.mean(o * o, axis=-1, keepdims=True) + NORM_EPS) * norm_w.astype(f32) * jax.nn.silu(z)
    out = o.reshape(bsz, t, DN_VAL_DIM).astype(h.dtype) @ w_out
    return out, s.astype(s0.dtype), new_buf


def moba_select(q, k_means, q_pos, n_blocks):
    scores = jnp.einsum('bthd,bnhd->bthn', q.astype(jnp.float32), k_means)
    own = q_pos // MB_BLOCK
    cand = jnp.arange(n_blocks)[None, :] < own[:, None]
    scores = jnp.where(cand[None, :, None, :], scores, -jnp.inf)
    _, idx = lax.top_k(scores, min(MB_TOP_K, n_blocks))
    valid = idx < own[None, :, None, None]
    own_b = jnp.broadcast_to(own[None, :, None, None], idx.shape[:3] + (1,))
    sel = jnp.concatenate([idx, own_b], axis=-1).astype(jnp.int32)
    sel_valid = jnp.concatenate([valid, jnp.ones(own_b.shape, dtype=bool)], axis=-1)
    return sel, sel_valid


def moba_attend(q, kg, vg, mask):
    logits = jnp.einsum('bthd,bthjsd->bthjs', q, kg, preferred_element_type=jnp.float32) * MB_HEAD_DIM ** -0.5
    logits = jnp.where(mask, logits, -jnp.inf)
    shp = logits.shape
    p = jax.nn.softmax(logits.reshape(shp[:3] + (-1,)), axis=-1).reshape(shp)
    out = jnp.einsum('bthjs,bthjsd->bthd', p.astype(vg.dtype), vg, preferred_element_type=jnp.float32)
    return out.astype(q.dtype)


def moba_prompt(h, w_qkv, w_o):
    bsz, s, _ = h.shape
    q, k, v = jnp.split(h @ w_qkv, 3, axis=-1)
    shp = (bsz, s, MB_HEADS, MB_HEAD_DIM)
    pos = jnp.arange(s, dtype=jnp.int32)
    q, k, v = rope(q.reshape(shp), pos), rope(k.reshape(shp), pos), v.reshape(shp)
    nb = -(-s // MB_BLOCK)
    pad = ((0, 0), (0, nb * MB_BLOCK - s), (0, 0), (0, 0))
    kb = jnp.pad(k, pad).reshape(bsz, nb, MB_BLOCK, MB_HEADS, MB_HEAD_DIM)
    vb = jnp.pad(v, pad).reshape(bsz, nb, MB_BLOCK, MB_HEADS, MB_HEAD_DIM)
    k_means = kb.astype(jnp.float32).mean(axis=2)
    sel, sel_valid = moba_select(q, k_means, pos, nb)
    nq = s // MB_Q_CHUNK

    def to_chunks(a):
        return jnp.moveaxis(a.reshape((bsz, nq, MB_Q_CHUNK) + a.shape[2:]), 1, 0)

    b_idx = jnp.arange(bsz)[:, None, None, None]
    h_idx = jnp.arange(MB_HEADS)[None, None, :, None]
    offs = jnp.arange(MB_BLOCK, dtype=jnp.int32)

    def attend_chunk(args):
        qc, selc, validc, posc = args
        kg = kb[b_idx, selc, :, h_idx]
        vg = vb[b_idx, selc, :, h_idx]
        kpos = selc[..., None] * MB_BLOCK + offs
        mask = validc[..., None] & (kpos <= posc[None, :, None, None, None])
        return moba_attend(qc, kg, vg, mask)

    out = lax.map(attend_chunk, (to_chunks(q), to_chunks(sel), to_chunks(sel_valid), pos.reshape(nq, MB_Q_CHUNK)))
    out = jnp.moveaxis(out, 0, 1).reshape(bsz, s, D_MODEL)
    return out @ w_o, k, v


def gather_rows(pool, layer, page_table, new, kpos, past):
    bd, t = new.shape[0], new.shape[1]
    n_pages = page_table.shape[1]
    b = jnp.arange(bd)[:, None, None, None, None]
    hh = jnp.arange(MB_HEADS)[None, None, :, None, None]
    phys = page_table[b, jnp.clip(kpos // PAGE_SIZE, 0, n_pages - 1)]
    from_pool = pool[layer, phys, kpos % PAGE_SIZE, hh]
    from_new = new[b, jnp.clip(kpos - past, 0, t - 1), hh]
    return jnp.where((kpos < past)[..., None], from_pool, from_new)


def moba_sample(h, cache_k, cache_v, page_table, layer, w_qkv, w_o):
    bd, t, _ = h.shape
    f32 = jnp.float32
    n_pages = page_table.shape[1]
    past = n_pages * PAGE_SIZE
    q, k, v = jnp.split(h @ w_qkv, 3, axis=-1)
    shp = (bd, t, MB_HEADS, MB_HEAD_DIM)
    pos = past + jnp.arange(t, dtype=jnp.int32)
    q, k, v = rope(q.reshape(shp), pos), rope(k.reshape(shp), pos), v.reshape(shp)
    nb = -(-(past + t) // MB_BLOCK)
    page_sum = cache_k[layer, page_table].astype(f32).sum(axis=2)
    page_blk = (jnp.arange(n_pages) * PAGE_SIZE) // MB_BLOCK
    blk_sum = jnp.zeros((bd, nb, MB_HEADS, MB_HEAD_DIM), f32).at[:, page_blk].add(page_sum)
    blk_sum = blk_sum.at[:, pos // MB_BLOCK].add(k.astype(f32))
    sel, sel_valid = moba_select(q, blk_sum / MB_BLOCK, pos, nb)
    kpos = sel[..., None] * MB_BLOCK + jnp.arange(MB_BLOCK, dtype=jnp.int32)
    kg = gather_rows(cache_k, layer, page_table, k, kpos, past)
    vg = gather_rows(cache_v, layer, page_table, v, kpos, past)
    mask = sel_valid[..., None] & (kpos <= pos[None, :, None, None, None])
    out = moba_attend(q, kg, vg, mask).reshape(bd, t, D_MODEL)
    return out @ w_o, k, v


def conv_ffn(h, buf, w_in, conv_w, conv_b, w_out):
    u, new_buf = causal_dwconv(h @ w_in, buf, conv_w)
    gate, val = jnp.split(u + conv_b, 2, axis=-1)
    return (jax.nn.silu(gate) * val) @ w_out, new_buf


def setup_inputs(seed: int = 0) -> dict:
    key = jax.random.key(seed)
    ks = jax.random.split(key, 24)
    f32 = jnp.float32
    n_pages = PAST_LEN // PAGE_SIZE
    n_used = DEC_BATCH * n_pages
    n_pool = n_used + max(1, n_used // 4)

    def nrm(k, shape, scale):
        return jax.random.normal(k, shape, f32) * scale

    pool_shape = (N_MB_LAYERS, n_pool, PAGE_SIZE, MB_HEADS, MB_HEAD_DIM)
    dt = jnp.exp(jax.random.uniform(ks[14], (N_DN_LAYERS, DN_V_HEADS), f32, math.log(1e-3), math.log(1e-1)))
    return {
        'x_prompt': nrm(ks[0], (BATCH, SEQ, D_MODEL), 1.0),
        'x_sample': nrm(ks[1], (DEC_BATCH, DEC_SEQ, D_MODEL), 1.0),
        'cache_k': nrm(ks[2], pool_shape, 1.0),
        'cache_v': nrm(ks[3], pool_shape, 1.0),
        'state_dn': nrm(ks[4], (N_DN_LAYERS, DEC_BATCH, DN_V_HEADS, DN_HEAD_DIM, DN_HEAD_DIM), 0.1),
        'state_dn_conv': nrm(ks[5], (N_DN_LAYERS, DEC_BATCH, DN_CONV - 1, DN_CONV_DIM), 1.0),
        'state_ffn_conv': nrm(ks[6], (DEPTH, DEC_BATCH, FFN_CONV - 1, 2 * D_FF), 1.0),
        'page_table': jax.random.permutation(ks[7], n_pool)[:n_used].reshape(DEC_BATCH, n_pages).astype(jnp.int32),
        'norm_mix': 1.0 + nrm(ks[8], (DEPTH, D_MODEL), 0.02),
        'norm_ffn': 1.0 + nrm(ks[9], (DEPTH, D_MODEL), 0.02),
        'norm_out': 1.0 + nrm(ks[10], (D_MODEL,), 0.02),
        'dn_w_in': nrm(ks[11], (N_DN_LAYERS, D_MODEL, DN_PROJ_DIM), D_MODEL ** -0.5),
        'dn_conv_w': nrm(ks[12], (N_DN_LAYERS, DN_CONV, DN_CONV_DIM), DN_CONV ** -0.5),
        'dn_a_log': jnp.log(jax.random.uniform(ks[13], (N_DN_LAYERS, DN_V_HEADS), f32, 1.0, 16.0)),
        'dn_dt_bias': dt + jnp.log(-jnp.expm1(-dt)),
        'dn_norm': 1.0 + nrm(ks[15], (N_DN_LAYERS, DN_HEAD_DIM), 0.02),
        'dn_w_out': nrm(ks[16], (N_DN_LAYERS, DN_VAL_DIM, D_MODEL), DN_VAL_DIM ** -0.5),
        'mb_w_qkv': nrm(ks[17], (N_MB_LAYERS, D_MODEL, 3 * D_MODEL), D_MODEL ** -0.5),
        'mb_w_o': nrm(ks[18], (N_MB_LAYERS, D_MODEL, D_MODEL), D_MODEL ** -0.5),
        'ffn_w_in': nrm(ks[19], (DEPTH, D_MODEL, 2 * D_FF), D_MODEL ** -0.5),
        'ffn_conv_w': nrm(ks[20], (DEPTH, FFN_CONV, 2 * D_FF), FFN_CONV ** -0.5),
        'ffn_conv_b': nrm(ks[21], (DEPTH, 2 * D_FF), 0.01),
        'ffn_w_out': nrm(ks[22], (DEPTH, D_FF, D_MODEL), D_FF ** -0.5),
    }


def reference(x_prompt, x_sample, cache_k, cache_v, state_dn, state_dn_conv, state_ffn_conv, page_table,
              norm_mix, norm_ffn, norm_out, dn_w_in, dn_conv_w, dn_a_log, dn_dt_bias, dn_norm, dn_w_out,
              mb_w_qkv, mb_w_o, ffn_w_in, ffn_conv_w, ffn_conv_b, ffn_w_out):
    bsz = x_prompt.shape[0]
    yp, ys = x_prompt, x_sample
    kp_l, vp_l, ks_l, vs_l = [], [], [], []
    sp_l, ss_l, cp_l, cs_l = [], [], [], []
    fp_l, fs_l = [], []
    for layer in range(DEPTH):
        j = layer // N_MIXERS
        hp = rms_norm(yp, norm_mix[layer])
        hs = rms_norm(ys, norm_mix[layer])
        if layer % N_MIXERS == 0:
            dn = (dn_w_in[j], dn_conv_w[j], dn_a_log[j], dn_dt_bias[j], dn_norm[j], dn_w_out[j])
            buf0 = jnp.zeros((bsz, DN_CONV - 1, DN_CONV_DIM), x_prompt.dtype)
            s0 = jnp.zeros((bsz, DN_V_HEADS, DN_HEAD_DIM, DN_HEAD_DIM), state_dn.dtype)
            mp, sp, cp = deltanet_mixer(hp, buf0, s0, *dn)
            ms, ss, cs = deltanet_mixer(hs, state_dn_conv[j], state_dn[j], *dn)
            sp_l.append(sp); ss_l.append(ss); cp_l.append(cp); cs_l.append(cs)
        else:
            mp, kp, vp = moba_prompt(hp, mb_w_qkv[j], mb_w_o[j])
            ms, kn, vn = moba_sample(hs, cache_k, cache_v, page_table, j, mb_w_qkv[j], mb_w_o[j])
            kp_l.append(kp); vp_l.append(vp); ks_l.append(kn); vs_l.append(vn)
        yp = yp + mp
        ys = ys + ms
        ffn = (ffn_w_in[layer], ffn_conv_w[layer], ffn_conv_b[layer], ffn_w_out[layer])
        fbuf0 = jnp.zeros((bsz, FFN_CONV - 1, 2 * D_FF), x_prompt.dtype)
        fp, fbp = conv_ffn(rms_norm(yp, norm_ffn[layer]), fbuf0, *ffn)
        fs, fbs = conv_ffn(rms_norm(ys, norm_ffn[layer]), state_ffn_conv[layer], *ffn)
        yp = yp + fp
        ys = ys + fs
        fp_l.append(fbp); fs_l.append(fbs)
    y_prompt = rms_norm(yp, norm_out)
    y_sample = rms_norm(ys, norm_out)
    return (y_prompt, y_sample, jnp.stack(kp_l), jnp.stack(vp_l), jnp.stack(ks_l), jnp.stack(vs_l),
            jnp.stack(sp_l), jnp.stack(ss_l), jnp.stack(cp_l), jnp.stack(cs_l), jnp.stack(fp_l), jnp.stack(fs_l))
```

```python
import functools

import jax
import jax.numpy as jnp
from jax import lax
from jax.experimental import pallas as pl
from jax.experimental.pallas import tpu as pltpu

F32 = jnp.float32
BF16 = jnp.bfloat16
I32 = jnp.int32

NORM_EPS = 1e-6
ROPE_THETA = 10000.0
HEAD_DIM = 128
DN_QK_HEADS = 8
DN_V_HEADS = 16
DN_CONV = 4
DN_KEY_DIM = DN_QK_HEADS * HEAD_DIM
DN_VAL_DIM = DN_V_HEADS * HEAD_DIM
DN_CONV_DIM = 2 * DN_KEY_DIM + DN_VAL_DIM
DN_MAIN_DIM = DN_CONV_DIM + DN_VAL_DIM
DN_CHUNK = 128
DN_TBLOCK = 256
MB_HEADS = 8
MB_BLOCK = 256
MB_TOP_K = 3
PAGE_SIZE = 128
FFN_CONV = 3
SUBLANES = 8
LANES = 128
NEG_BIG = -1e30
VMEM_LIMIT = 56 * 1024 * 1024


def _cparams(*sem):
    return pltpu.CompilerParams(dimension_semantics=sem, vmem_limit_bytes=VMEM_LIMIT)


def _sigmoid(x):
    return 1.0 / (1.0 + jnp.exp(-x))


def _silu(x):
    return x * _sigmoid(x)


def _softplus(x):
    return jnp.maximum(x, 0.0) + jnp.log1p(jnp.exp(-jnp.abs(x)))


def _rms_rows(x, g):
    ms = jnp.mean(x * x, axis=-1, keepdims=True)
    return x * lax.rsqrt(ms + NORM_EPS) * g


def _dot(a, b):
    return jnp.dot(a, b, preferred_element_type=F32)


def _dot_nt(a, b):
    return lax.dot_general(a, b, (((1,), (1,)), ((), ())), preferred_element_type=F32)


def _dot_tn(a, b):
    return lax.dot_general(a, b, (((0,), (0,)), ((), ())), preferred_element_type=F32)


def _split2(a):
    hi = a.astype(BF16)
    lo = (a - hi.astype(F32)).astype(BF16)
    return hi, lo


def _mm3(a, b):
    ah, al = _split2(a)
    bh, bl = _split2(b)
    return _dot(ah, bh) + (_dot(ah, bl) + _dot(al, bh))


def _dot_exact01(m01, x):
    hi = x.astype(BF16)
    r1 = x - hi.astype(F32)
    mid = r1.astype(BF16)
    lo = (r1 - mid.astype(F32)).astype(BF16)
    return _dot(m01, hi) + (_dot(m01, mid) + _dot(m01, lo))


def _dot_exact01_r(x, m01):
    hi = x.astype(BF16)
    r1 = x - hi.astype(F32)
    mid = r1.astype(BF16)
    lo = (r1 - mid.astype(F32)).astype(BF16)
    return _dot(hi, m01) + (_dot(mid, m01) + _dot(lo, m01))


def _dn_inproj_kernel(x_ref, g_ref, w_ref, wbg_ref, par_ref, o_ref, bg_ref, *rest, transposed):
    if transposed:
        bgt_ref, h_scr = rest
    else:
        (h_scr,) = rest

    @pl.when(pl.program_id(1) == 0)
    def _():
        h = _rms_rows(x_ref[...], g_ref[...]).astype(BF16)
        h_scr[...] = h
        raw = _dot(h, wbg_ref[...])
        neg_a, dt_b, is_beta = par_ref[0:1, :], par_ref[1:2, :], par_ref[2:3, :]
        act = jnp.where(is_beta > 0.5, _sigmoid(raw), neg_a * _softplus(raw + dt_b))
        bg_ref[...] = act
        if transposed:
            bgt_ref[...] = act.T

    o_ref[...] = _dot(h_scr[...], w_ref[...])


def _dn_inproj(x, g, w_main, w_bg, par, *, transposed):
    m, d = x.shape
    n = w_main.shape[1]
    tm = min(512, m)
    tn = 2048
    out_shape = [jax.ShapeDtypeStruct((m, n), F32), jax.ShapeDtypeStruct((m, LANES), F32)]
    out_specs = [pl.BlockSpec((tm, tn), lambda i, j: (i, j)),
                 pl.BlockSpec((tm, LANES), lambda i, j: (i, 0))]
    if transposed:
        out_shape.append(jax.ShapeDtypeStruct((LANES, m), F32))
        out_specs.append(pl.BlockSpec((LANES, tm), lambda i, j: (0, i)))
    return pl.pallas_call(
        functools.partial(_dn_inproj_kernel, transposed=transposed),
        grid=(m // tm, n // tn),
        in_specs=[pl.BlockSpec((tm, d), lambda i, j: (i, 0)),
                  pl.BlockSpec((1, d), lambda i, j: (0, 0)),
                  pl.BlockSpec((d, tn), lambda i, j: (0, j)),
                  pl.BlockSpec((d, LANES), lambda i, j: (0, 0)),
                  pl.BlockSpec((SUBLANES, LANES), lambda i, j: (0, 0))],
        out_specs=out_specs,
        out_shape=out_shape,
        scratch_shapes=[pltpu.VMEM((tm, d), BF16)],
        compiler_params=_cparams("parallel", "arbitrary"),
        name="dn_inproj",
    )(x, g, w_main, w_bg, par)


def _unit_lower_inverse(low, c):
    row = lax.broadcasted_iota(I32, (c, c), 0)
    col = lax.broadcasted_iota(I32, (c, c), 1)

    def same_block(shift):
        return (row >> shift) == (col >> shift)

    neg = jnp.where(same_block(3), -low, 0.0)
    inv = jnp.where(row == col, 1.0, 0.0) + neg
    power = neg
    for _ in range(2):
        power = _mm3(power, power)
        inv = inv + _mm3(inv, power)
    shift = 3
    while (1 << shift) < c:
        off = jnp.where(same_block(shift + 1) & jnp.logical_not(same_block(shift)), low, 0.0)
        inv = inv - _mm3(inv, _mm3(off, inv))
        shift += 1
    return inv


def _dn_core_kernel(q_ref, k_ref, v_ref, z_ref, bg_ref, bgt_ref, cwq_ref, cwk_ref, cwv_ref, nw_ref,
                    o_ref, s_out_ref, xe_scr, s_scr, *, tb, chunk):
    hq = pl.program_id(1)
    t = pl.program_id(2)
    hd = HEAD_DIM

    @pl.when(t == 0)
    def _():
        s_scr[...] = jnp.zeros(s_scr.shape, F32)
        xe_scr[0:SUBLANES, :] = jnp.zeros((SUBLANES, 4 * hd), F32)

    xe_scr[SUBLANES:SUBLANES + tb, 0:hd] = q_ref[...]
    xe_scr[SUBLANES:SUBLANES + tb, hd:2 * hd] = k_ref[...]
    xe_scr[SUBLANES:SUBLANES + tb, 2 * hd:4 * hd] = v_ref[...]

    def conv(lo, hi, cw_ref):
        acc = cw_ref[DN_CONV - 1:DN_CONV, :] * xe_scr[SUBLANES:SUBLANES + tb, lo:hi]
        for d in range(1, DN_CONV):
            acc = acc + (cw_ref[DN_CONV - 1 - d:DN_CONV - d, :]
                         * xe_scr[SUBLANES - d:SUBLANES - d + tb, lo:hi])
        return _silu(acc)

    qc = conv(0, hd, cwq_ref)
    kc = conv(hd, 2 * hd, cwk_ref)
    vc = conv(2 * hd, 4 * hd, cwv_ref)
    xe_scr[0:SUBLANES, :] = xe_scr[tb:tb + SUBLANES, :]

    qn = qc * lax.rsqrt(jnp.sum(qc * qc, axis=-1, keepdims=True) + NORM_EPS) * (hd ** -0.5)
    kn = kc * lax.rsqrt(jnp.sum(kc * kc, axis=-1, keepdims=True) + NORM_EPS)

    bgr = pltpu.roll(bg_ref[...], (LANES - SUBLANES * hq) % LANES, axis=1)
    bgt = bgt_ref[...]
    ri = lax.broadcasted_iota(I32, (tb, tb), 0)
    ci = lax.broadcasted_iota(I32, (tb, tb), 1)
    cshift = chunk.bit_length() - 1
    same_chunk = (ri >> cshift) == (ci >> cshift)
    upper = jnp.where(same_chunk & (ri <= ci), 1.0, 0.0).astype(BF16)
    lower = jnp.where(same_chunk & (ri >= ci), 1.0, 0.0).astype(BF16)
    gc_rows = _dot_exact01_r(bgt, upper)
    gc_cols = _dot_exact01(lower, bgr)

    rc = lax.broadcasted_iota(I32, (chunk, chunk), 0)
    cc = lax.broadcasted_iota(I32, (chunk, chunk), 1)
    nw = nw_ref[...]

    for c in range(tb // chunk):
        r0, r1 = c * chunk, (c + 1) * chunk
        q_c, k_c = qn[r0:r1], kn[r0:r1]
        kb = k_c.astype(BF16)
        kk = _dot_nt(kb, kb)
        qk = _dot_nt(q_c.astype(BF16), kb)
        for x in range(2):
            g_col = gc_cols[r0:r1, 2 + x:3 + x]
            g_row = gc_rows[2 + x:3 + x, r0:r1]
            b_col = bgr[r0:r1, x:x + 1]
            decay = jnp.exp(jnp.where(rc >= cc, g_col - g_row, -jnp.inf))
            low = jnp.where(rc > cc, b_col * decay * kk, 0.0)
            inv = _unit_lower_inverse(low, chunk)
            eg = jnp.exp(g_col)
            v_c = vc[r0:r1, x * hd:(x + 1) * hd]
            rhs = jnp.concatenate([b_col * v_c, (b_col * eg) * k_c], axis=1).astype(BF16)
            sol = _dot(inv.astype(BF16), rhs)
            u0, w = sol[:, :hd], sol[:, hd:]
            s_old = s_scr[x]
            sb = s_old.astype(BF16)
            u = u0 - _dot(w.astype(BF16), sb)
            ub = u.astype(BF16)
            o = _dot((q_c * eg).astype(BF16), sb) + _dot((qk * decay).astype(BF16), ub)
            g_last = g_col[chunk - 1:chunk, :]
            kd = k_c * jnp.exp(g_last - g_col)
            s_scr[x] = jnp.exp(g_last) * s_old + _dot_tn(kd.astype(BF16), ub)
            z_c = z_ref[r0:r1, x * hd:(x + 1) * hd]
            on = o * lax.rsqrt(jnp.mean(o * o, axis=-1, keepdims=True) + NORM_EPS) * nw * _silu(z_c)
            o_ref[r0:r1, x * hd:(x + 1) * hd] = on.astype(o_ref.dtype)

    @pl.when(t == pl.num_programs(2) - 1)
    def _():
        s_out_ref[...] = s_scr[...]


def _dn_core(proj, bg, bgt, conv_w, norm_w, bsz, seq):
    tb, hd = DN_TBLOCK, HEAD_DIM
    nt = seq // tb
    kernel = functools.partial(_dn_core_kernel, tb=tb, chunk=DN_CHUNK)
    tok = lambda b, h, t: b * nt + t
    return pl.pallas_call(
        kernel,
        grid=(bsz, DN_QK_HEADS, nt),
        in_specs=[
            pl.BlockSpec((tb, hd), lambda b, h, t: (tok(b, h, t), h)),
            pl.BlockSpec((tb, hd), lambda b, h, t: (tok(b, h, t), DN_QK_HEADS + h)),
            pl.BlockSpec((tb, 2 * hd), lambda b, h, t: (tok(b, h, t), DN_QK_HEADS + h)),
            pl.BlockSpec((tb, 2 * hd), lambda b, h, t: (tok(b, h, t), 2 * DN_QK_HEADS + h)),
            pl.BlockSpec((tb, LANES), lambda b, h, t: (tok(b, h, t), 0)),
            pl.BlockSpec((SUBLANES, tb), lambda b, h, t: (h, tok(b, h, t))),
            pl.BlockSpec((DN_CONV, hd), lambda b, h, t: (0, h)),
            pl.BlockSpec((DN_CONV, hd), lambda b, h, t: (0, DN_QK_HEADS + h)),
            pl.BlockSpec((DN_CONV, 2 * hd), lambda b, h, t: (0, DN_QK_HEADS + h)),
            pl.BlockSpec((1, hd), lambda b, h, t: (0, 0)),
        ],
        out_specs=[
            pl.BlockSpec((tb, 2 * hd), lambda b, h, t: (tok(b, h, t), h)),
            pl.BlockSpec((None, 2, hd, hd), lambda b, h, t: (b, h, 0, 0)),
        ],
        out_shape=[jax.ShapeDtypeStruct((bsz * seq, DN_VAL_DIM), BF16),
                   jax.ShapeDtypeStruct((bsz, DN_V_HEADS, hd, hd), F32)],
        scratch_shapes=[pltpu.VMEM((tb + SUBLANES, 4 * hd), F32),
                        pltpu.VMEM((2, hd, hd), F32)],
        compiler_params=_cparams("parallel", "parallel", "arbitrary"),
        name="dn_core",
    )(proj, proj, proj, proj, bg, bgt, conv_w, conv_w, conv_w, norm_w)


def _dn_sample_kernel(p_ref, bg_ref, cst_ref, cw_ref, s_ref, nw_ref,
                      o_ref, s_out_ref, cst_out_ref, r_scr):
    hd = HEAD_DIM
    x = p_ref[:, 0:DN_CONV_DIM]
    buf = cst_ref[...]
    y = x * cw_ref[DN_CONV - 1:DN_CONV, :]
    for i in range(DN_CONV - 1):
        y = y + buf[i:i + 1, :] * cw_ref[i:i + 1, :]
    cst_out_ref[0:DN_CONV - 2, :] = buf[1:DN_CONV - 1, :]
    cst_out_ref[DN_CONV - 2:DN_CONV - 1, :] = x
    a = _silu(y)

    r_scr[...] = jnp.zeros(r_scr.shape, F32)
    qs, ks = [], []
    for h in range(DN_QK_HEADS):
        qh = a[:, h * hd:(h + 1) * hd]
        kh = a[:, DN_KEY_DIM + h * hd:DN_KEY_DIM + (h + 1) * hd]
        qh = qh * lax.rsqrt(jnp.sum(qh * qh, axis=-1, keepdims=True) + NORM_EPS) * (hd ** -0.5)
        kh = kh * lax.rsqrt(jnp.sum(kh * kh, axis=-1, keepdims=True) + NORM_EPS)
        qs.append(qh)
        ks.append(kh)
        r_scr[h:h + 1, :] = qh
        r_scr[DN_QK_HEADS + h:DN_QK_HEADS + h + 1, :] = kh
    rt = r_scr[...].T

    bg = bg_ref[...]
    nw = nw_ref[...]
    rep = DN_V_HEADS // DN_QK_HEADS
    for h in range(DN_V_HEADS):
        hq, x_in = h // rep, h % rep
        beta = bg[:, SUBLANES * hq + x_in:SUBLANES * hq + x_in + 1]
        g = bg[:, SUBLANES * hq + 2 + x_in:SUBLANES * hq + 3 + x_in]
        eg = jnp.exp(g)
        q_col = rt[:, hq:hq + 1]
        k_col = rt[:, DN_QK_HEADS + hq:DN_QK_HEADS + hq + 1]
        s_old = s_ref[h]
        k_s = jnp.sum(k_col * s_old, axis=0, keepdims=True)
        q_s = jnp.sum(q_col * s_old, axis=0, keepdims=True)
        v_h = a[:, 2 * DN_KEY_DIM + h * hd:2 * DN_KEY_DIM + (h + 1) * hd]
        u = beta * v_h - (beta * eg) * k_s
        qk = jnp.sum(qs[hq] * ks[hq], axis=-1, keepdims=True)
        o = eg * q_s + qk * u
        s_out_ref[h] = eg * s_old + k_col * u
        z_h = p_ref[:, DN_CONV_DIM + h * hd:DN_CONV_DIM + (h + 1) * hd]
        on = o * lax.rsqrt(jnp.mean(o * o, axis=-1, keepdims=True) + NORM_EPS) * nw * _silu(z_h)
        o_ref[:, h * hd:(h + 1) * hd] = on.astype(o_ref.dtype)


def _dn_sample(proj, bg, conv_state, conv_w, state, norm_w):
    bd = proj.shape[0]
    hd = HEAD_DIM
    o, s_new, c_new = pl.pallas_call(
        _dn_sample_kernel,
        grid=(bd,),
        in_specs=[
            pl.BlockSpec((None, 1, DN_MAIN_DIM), lambda b: (b, 0, 0)),
            pl.BlockSpec((None, 1, LANES), lambda b: (b, 0, 0)),
            pl.BlockSpec((None, DN_CONV - 1, DN_CONV_DIM), lambda b: (b, 0, 0)),
            pl.BlockSpec((DN_CONV, DN_CONV_DIM), lambda b: (0, 0)),
            pl.BlockSpec((None, DN_V_HEADS, hd, hd), lambda b: (b, 0, 0, 0)),
            pl.BlockSpec((1, hd), lambda b: (0, 0)),
        ],
        out_specs=[
            pl.BlockSpec((None, 1, DN_VAL_DIM), lambda b: (b, 0, 0)),
            pl.BlockSpec((None, DN_V_HEADS, hd, hd), lambda b: (b, 0, 0, 0)),
            pl.BlockSpec((None, DN_CONV - 1, DN_CONV_DIM), lambda b: (b, 0, 0)),
        ],
        out_shape=[jax.ShapeDtypeStruct((bd, 1, DN_VAL_DIM), BF16),
                   jax.ShapeDtypeStruct(state.shape, F32),
                   jax.ShapeDtypeStruct(conv_state.shape, F32)],
        scratch_shapes=[pltpu.VMEM((LANES, LANES), F32)],
        compiler_params=_cparams("parallel"),
        name="dn_sample",
    )(proj.reshape(bd, 1, DN_MAIN_DIM), bg.reshape(bd, 1, LANES), conv_state, conv_w, state, norm_w)
    return o.reshape(bd, DN_VAL_DIM), s_new, c_new


def _matmul_residual_kernel(a_ref, w_ref, y_ref, o_ref):
    o_ref[...] = y_ref[...] + _dot(a_ref[...], w_ref[...])


def _matmul_residual(a, w, y):
    m, k = a.shape
    n = w.shape[1]
    tm = min(512, m)
    return pl.pallas_call(
        _matmul_residual_kernel,
        grid=(m // tm,),
        in_specs=[pl.BlockSpec((tm, k), lambda i: (i, 0)),
                  pl.BlockSpec((k, n), lambda i: (0, 0)),
                  pl.BlockSpec((tm, n), lambda i: (i, 0))],
        out_specs=pl.BlockSpec((tm, n), lambda i: (i, 0)),
        out_shape=jax.ShapeDtypeStruct((m, n), F32),
        compiler_params=_cparams("parallel"),
        name="matmul_residual",
    )(a, w, y)


def _mb_qkv_kernel(x_ref, g_ref, w_ref, cos_ref, sin_ref, q_ref, k_ref, v_ref, *rest, with_blocks):
    if with_blocks:
        kb_ref, vt_ref, km_ref, h_scr = rest
    else:
        (h_scr,) = rest
    j = pl.program_id(1)
    hd = HEAD_DIM

    @pl.when(j == 0)
    def _():
        h_scr[...] = _rms_rows(x_ref[...], g_ref[...]).astype(BF16)

    y = _dot(h_scr[...], w_ref[...])

    def rope(val):
        cos, sin = cos_ref[...], sin_ref[...]
        parts = []
        for h in range(MB_HEADS):
            yh = val[:, h * hd:(h + 1) * hd]
            parts.append(yh * cos + pltpu.roll(yh, hd // 2, axis=1) * sin)
        return jnp.concatenate(parts, axis=1)

    @pl.when(j == 0)
    def _():
        q_ref[...] = rope(y).astype(q_ref.dtype)

    @pl.when(j == 1)
    def _():
        k = rope(y)
        k_ref[...] = k
        if with_blocks:
            kb_ref[...] = k.astype(BF16)
            for r in range(k.shape[0] // MB_BLOCK):
                blk = k[r * MB_BLOCK:(r + 1) * MB_BLOCK, :]
                km_ref[r] = jnp.sum(blk, axis=0, keepdims=True) * (1.0 / MB_BLOCK)

    @pl.when(j == 2)
    def _():
        v_ref[...] = y
        if with_blocks:
            for r in range(y.shape[0] // MB_BLOCK):
                vt_ref[r] = y[r * MB_BLOCK:(r + 1) * MB_BLOCK, :].T.astype(BF16)


def _mb_qkv(x, g, w, cos, sin, *, with_blocks):
    m, d = x.shape
    tm = min(512, m)
    n_pos_tiles = cos.shape[0] // tm
    out_shape = [jax.ShapeDtypeStruct((m, d), BF16 if with_blocks else F32),
                 jax.ShapeDtypeStruct((m, d), F32), jax.ShapeDtypeStruct((m, d), F32)]
    row = pl.BlockSpec((tm, d), lambda i, j: (i, 0))
    out_specs = [row, row, row]
    if with_blocks:
        nb = tm // MB_BLOCK
        out_shape += [jax.ShapeDtypeStruct((m, d), BF16),
                      jax.ShapeDtypeStruct((m // MB_BLOCK, d, MB_BLOCK), BF16),
                      jax.ShapeDtypeStruct((m // MB_BLOCK, 1, d), F32)]
        out_specs += [row,
                      pl.BlockSpec((nb, d, MB_BLOCK), lambda i, j: (i, 0, 0)),
                      pl.BlockSpec((nb, 1, d), lambda i, j: (i, 0, 0))]
    return pl.pallas_call(
        functools.partial(_mb_qkv_kernel, with_blocks=with_blocks),
        grid=(m // tm, 3),
        in_specs=[row,
                  pl.BlockSpec((1, d), lambda i, j: (0, 0)),
                  pl.BlockSpec((d, d), lambda i, j: (0, j)),
                  pl.BlockSpec((tm, HEAD_DIM), lambda i, j: (i % n_pos_tiles, 0)),
                  pl.BlockSpec((tm, HEAD_DIM), lambda i, j: (i % n_pos_tiles, 0))],
        out_specs=out_specs,
        out_shape=out_shape,
        scratch_shapes=[pltpu.VMEM((tm, d), BF16)],
        compiler_params=_cparams("parallel", "arbitrary"),
        name="mb_qkv",
    )(x, g, w, cos, sin)


def _select_top_blocks(scores, own, n_rows):
    blk = lax.broadcasted_iota(I32, scores.shape, 0)
    s = jnp.where(blk < own, scores, -jnp.inf)
    sel = jnp.zeros(scores.shape, F32)
    for _ in range(MB_TOP_K):
        m = jnp.max(s, axis=0, keepdims=True)
        idx = jnp.min(jnp.where(s == m, blk, n_rows), axis=0, keepdims=True)
        hit = blk == idx
        sel = jnp.where(hit & (idx < own), 1.0, sel)
        s = jnp.where(hit, -jnp.inf, s)
    return sel


def _mb_attn_kernel(q_ref, k_ref, vt_ref, km_ref, o_ref, sel_scr):
    n = pl.program_id(2)
    bq = MB_BLOCK
    scale = HEAD_DIM ** -0.5
    q = q_ref[...]
    nb = km_ref.shape[0]
    scores = _dot_nt(km_ref[...].astype(BF16), q)
    sel_scr[...] = _select_top_blocks(scores, n, nb)

    def logits(j):
        kj = k_ref[pl.ds(pl.multiple_of(j * bq, bq), bq), :]
        return _dot_nt(kj, q) * scale

    kpos = lax.broadcasted_iota(I32, (bq, bq), 0)
    qpos = lax.broadcasted_iota(I32, (bq, bq), 1)
    st = jnp.where(kpos <= qpos, logits(n), NEG_BIG)
    m = jnp.max(st, axis=0, keepdims=True)
    p = jnp.exp(st - m)
    l = jnp.sum(p, axis=0, keepdims=True)
    acc = _dot(vt_ref[n], p.astype(BF16))

    def body(j, carry):
        m, l, acc = carry
        st = jnp.where(sel_scr[pl.ds(j, 1), :] > 0.5, logits(j), NEG_BIG)
        m_new = jnp.maximum(m, jnp.max(st, axis=0, keepdims=True))
        alpha = jnp.exp(m - m_new)
        p = jnp.exp(st - m_new)
        l = alpha * l + jnp.sum(p, axis=0, keepdims=True)
        acc = alpha * acc + _dot(vt_ref[j], p.astype(BF16))
        return m_new, l, acc

    m, l, acc = lax.fori_loop(0, n, body, (m, l, acc))
    o_ref[...] = (acc / l).T.astype(o_ref.dtype)


def _mb_attn(q, kb, vt, km, bsz, seq):
    hd, bq = HEAD_DIM, MB_BLOCK
    nb = seq // bq
    d = MB_HEADS * hd
    return pl.pallas_call(
        _mb_attn_kernel,
        grid=(bsz, MB_HEADS, nb),
        in_specs=[
            pl.BlockSpec((bq, hd), lambda b, h, n: (b * nb + n, h)),
            pl.BlockSpec((seq, hd), lambda b, h, n: (b, h)),
            pl.BlockSpec((nb, hd, bq), lambda b, h, n: (b, h, 0)),
            pl.BlockSpec((nb, None, hd), lambda b, h, n: (b, 0, h)),
        ],
        out_specs=pl.BlockSpec((bq, hd), lambda b, h, n: (b * nb + n, h)),
        out_shape=jax.ShapeDtypeStruct((bsz * seq, d), BF16),
        scratch_shapes=[pltpu.VMEM((nb, bq), F32)],
        compiler_params=_cparams("parallel", "parallel", "arbitrary"),
        name="mb_attn",
    )(q, kb, vt, km)


PAGES_PER_STEP = 16


def _page_sum_kernel(pt_ref, *refs):
    pages, o_ref = refs[:PAGES_PER_STEP], refs[PAGES_PER_STEP]
    per_blk = MB_BLOCK // PAGE_SIZE
    for r in range(PAGES_PER_STEP // per_blk):
        s = jnp.sum(pages[per_blk * r][...], axis=0, keepdims=True)
        for i in range(1, per_blk):
            s = s + jnp.sum(pages[per_blk * r + i][...], axis=0, keepdims=True)
        o_ref[r:r + 1, :] = s


def _page_block_sums(pool, page_table, layer):
    bd, n_pages = page_table.shape
    d = pool.shape[-1]
    steps = n_pages // PAGES_PER_STEP
    rows = PAGES_PER_STEP * PAGE_SIZE // MB_BLOCK

    def page_spec(i):
        return pl.BlockSpec(
            (None, None, PAGE_SIZE, d),
            lambda b, s, pt: (layer, pt[b * n_pages + s * PAGES_PER_STEP + i], 0, 0))

    return pl.pallas_call(
        _page_sum_kernel,
        grid_spec=pltpu.PrefetchScalarGridSpec(
            num_scalar_prefetch=1,
            grid=(bd, steps),
            in_specs=[page_spec(i) for i in range(PAGES_PER_STEP)],
            out_specs=pl.BlockSpec((None, rows, d), lambda b, s, pt: (b, s, 0)),
        ),
        out_shape=jax.ShapeDtypeStruct((bd, steps * rows, d), F32),
        compiler_params=_cparams("parallel", "arbitrary"),
        name="mb_page_sums",
    )(page_table.reshape(-1), *([pool] * PAGES_PER_STEP))


def _mb_sample_select_kernel(q_ref, kn_ref, bs_ref, sel_ref, *, own):
    hd = HEAD_DIM
    nb = bs_ref.shape[0]
    n_rows = nb + SUBLANES
    row8 = lax.broadcasted_iota(I32, (SUBLANES, 1), 0)
    blk = lax.broadcasted_iota(I32, (n_rows, 1), 0)
    out_row = lax.broadcasted_iota(I32, (SUBLANES, LANES), 0)
    out_lane = lax.broadcasted_iota(I32, (SUBLANES, LANES), 1)
    out = jnp.zeros((SUBLANES, LANES), I32)
    inv_blk = 1.0 / MB_BLOCK
    for h in range(MB_HEADS):
        qh = q_ref[h:h + 1, :]
        means = bs_ref[:, h * hd:(h + 1) * hd] * inv_blk
        sc = jnp.sum(means * qh, axis=-1, keepdims=True)
        own_sc = jnp.sum(kn_ref[h:h + 1, :] * inv_blk * qh, axis=-1, keepdims=True)
        tail = jnp.where(row8 == 0, own_sc, -jnp.inf)
        s = jnp.concatenate([sc, tail], axis=0)
        s = jnp.where(blk < own, s, -jnp.inf)
        for r in range(MB_TOP_K):
            m = jnp.max(s, axis=0, keepdims=True)
            idx = jnp.min(jnp.where(s == m, blk, n_rows), axis=0, keepdims=True)
            valid = (idx < own).astype(I32)
            out = jnp.where((out_row == h) & (out_lane == r), idx, out)
            out = jnp.where((out_row == h) & (out_lane == MB_TOP_K + 1 + r), valid, out)
            s = jnp.where(blk == idx, -jnp.inf, s)
    sel_ref[...] = out


def _mb_sample_select(q, k_new, blk_sums, own):
    bd = q.shape[0]
    nb, d = blk_sums.shape[1:]
    return pl.pallas_call(
        functools.partial(_mb_sample_select_kernel, own=own),
        grid=(bd,),
        in_specs=[pl.BlockSpec((None, MB_HEADS, HEAD_DIM), lambda b: (b, 0, 0)),
                  pl.BlockSpec((None, MB_HEADS, HEAD_DIM), lambda b: (b, 0, 0)),
                  pl.BlockSpec((None, nb, d), lambda b: (b, 0, 0))],
        out_specs=pl.BlockSpec((None, SUBLANES, LANES), lambda b: (b, 0, 0)),
        out_shape=jax.ShapeDtypeStruct((bd, SUBLANES, LANES), I32),
        compiler_params=_cparams("parallel"),
        name="mb_sample_select",
    )(q, k_new, blk_sums)


def _mb_sample_attn_kernel(pt_ref, sel_ref, q_ref, kn_ref, vn_ref, *refs):
    n_pg = MB_TOP_K * (MB_BLOCK // PAGE_SIZE)
    k_pages, v_pages, o_ref = refs[:n_pg], refs[n_pg:2 * n_pg], refs[2 * n_pg]
    b, h = pl.program_id(0), pl.program_id(1)
    scale = HEAD_DIM ** -0.5
    q = q_ref[...]
    q8 = jnp.broadcast_to(q, (SUBLANES, HEAD_DIM)).astype(BF16)
    keys = jnp.concatenate([r[...] for r in k_pages], axis=0).astype(BF16)
    vals = jnp.concatenate([r[...] for r in v_pages], axis=0).astype(BF16)
    lg = _dot_nt(q8, keys) * scale
    lane = lax.broadcasted_iota(I32, lg.shape, 1)
    base = (b * MB_HEADS + h) * SUBLANES
    for r in range(MB_TOP_K):
        penalty = jnp.where(sel_ref[base + MB_TOP_K + 1 + r] == 0, NEG_BIG, 0.0)
        in_blk = (lane >= r * MB_BLOCK) & (lane < (r + 1) * MB_BLOCK)
        lg = jnp.where(in_blk, lg + penalty, lg)
    qb = q.astype(BF16).astype(F32)
    lo = jnp.sum(qb * kn_ref[...].astype(BF16).astype(F32), axis=-1, keepdims=True) * scale
    m = jnp.maximum(jnp.max(lg, axis=-1, keepdims=True), lo)
    p = jnp.exp(lg - m)
    po = jnp.exp(lo - m)
    l = jnp.sum(p, axis=-1, keepdims=True) + po
    vn = vn_ref[...].astype(BF16).astype(F32)
    out = (_dot(p.astype(BF16), vals) + po.astype(BF16).astype(F32) * vn) / l
    o_ref[...] = out[0:1, :]


def _mb_sample_attn(pool_k, pool_v, page_table, sel, q, k_new, v_new, layer):
    bd, n_pages = page_table.shape
    hd = HEAD_DIM
    per_blk = MB_BLOCK // PAGE_SIZE

    def page_spec(r, half):
        def imap(b, h, pt, sl):
            blk = sl[(b * MB_HEADS + h) * SUBLANES + r]
            return (layer, pt[b * n_pages + blk * per_blk + half], 0, h)
        return pl.BlockSpec((None, None, PAGE_SIZE, hd), imap)

    pages = [page_spec(r, half) for r in range(MB_TOP_K) for half in range(per_blk)]
    vec = pl.BlockSpec((None, None, 1, hd), lambda b, h, pt, sl: (b, h, 0, 0))
    q4 = q.reshape(bd, MB_HEADS, 1, hd)
    out = pl.pallas_call(
        _mb_sample_attn_kernel,
        grid_spec=pltpu.PrefetchScalarGridSpec(
            num_scalar_prefetch=2,
            grid=(bd, MB_HEADS),
            in_specs=[vec, vec, vec] + pages + pages,
            out_specs=vec,
        ),
        out_shape=jax.ShapeDtypeStruct((bd, MB_HEADS, 1, hd), F32),
        compiler_params=_cparams("parallel", "arbitrary"),
        name="mb_sample_attn",
    )(page_table.reshape(-1), sel[:, :, :SUBLANES].reshape(-1), q4, k_new.reshape(q4.shape), v_new.reshape(q4.shape),
      *([pool_k] * len(pages)), *([pool_v] * len(pages)))
    return out.reshape(bd, MB_HEADS * hd)


def _ffn_kernel(y_ref, g_ref, wg_ref, wv_ref, cwg_ref, cwv_ref, cbg_ref, cbv_ref, wo_ref,
                out_ref, bufg_ref, bufv_ref, h_scr, ug_scr, uv_scr, cg_scr, cv_scr, acc_scr,
                *, tiles_per_seq):
    i, f = pl.program_id(0), pl.program_id(1)
    tm = y_ref.shape[0]
    first = (i % tiles_per_seq) == 0

    @pl.when(f == 0)
    def _():
        h_scr[...] = _rms_rows(y_ref[...], g_ref[...]).astype(BF16)
        acc_scr[...] = jnp.zeros(acc_scr.shape, F32)

    @pl.when(first)
    def _():
        cg_scr[f] = jnp.zeros(cg_scr.shape[1:], F32)
        cv_scr[f] = jnp.zeros(cv_scr.shape[1:], F32)

    def branch(w_ref, cw_ref, cb_ref, u_scr, carry, buf_ref):
        u = _dot(h_scr[...], w_ref[...])
        u_scr[0:SUBLANES, :] = carry[f]
        u_scr[SUBLANES:SUBLANES + tm, :] = u
        r = cw_ref[FFN_CONV - 1:FFN_CONV, :] * u + cb_ref[...]
        for d in range(1, FFN_CONV):
            r = r + cw_ref[FFN_CONV - 1 - d:FFN_CONV - d, :] * u_scr[SUBLANES - d:SUBLANES - d + tm, :]
        last = u_scr[tm:tm + SUBLANES, :]
        carry[f] = last
        buf_ref[...] = last
        return r

    gate = branch(wg_ref, cwg_ref, cbg_ref, ug_scr, cg_scr, bufg_ref)
    val = branch(wv_ref, cwv_ref, cbv_ref, uv_scr, cv_scr, bufv_ref)
    act = (_silu(gate) * val).astype(BF16)
    acc_scr[...] += _dot(act, wo_ref[...])

    @pl.when(f == pl.num_programs(1) - 1)
    def _():
        out_ref[...] = y_ref[...] + acc_scr[...]


def _ffn_tile(dff):
    for cand in (1408, 1024, 512, 256, 128):
        if dff % cand == 0:
            return cand
    raise ValueError(f"unsupported d_ff {dff}")


def _ffn_prompt(y, g, w_in, conv_w, conv_b, w_out, bsz, seq):
    m, d = y.shape
    dff = w_out.shape[0]
    tm = min(512, seq)
    tf = _ffn_tile(dff)
    nf = dff // tf
    tiles_per_seq = seq // tm
    gate = lambda i, f: (0, f)
    val = lambda i, f: (0, f + nf)
    buf_spec = pl.BlockSpec((None, SUBLANES, tf), lambda i, f: (i, 0, f))
    out, bufg, bufv = pl.pallas_call(
        functools.partial(_ffn_kernel, tiles_per_seq=tiles_per_seq),
        grid=(m // tm, nf),
        in_specs=[pl.BlockSpec((tm, d), lambda i, f: (i, 0)),
                  pl.BlockSpec((1, d), lambda i, f: (0, 0)),
                  pl.BlockSpec((d, tf), gate), pl.BlockSpec((d, tf), val),
                  pl.BlockSpec((FFN_CONV, tf), gate), pl.BlockSpec((FFN_CONV, tf), val),
                  pl.BlockSpec((1, tf), gate), pl.BlockSpec((1, tf), val),
                  pl.BlockSpec((tf, d), lambda i, f: (f, 0))],
        out_specs=[pl.BlockSpec((tm, d), lambda i, f: (i, 0)), buf_spec, buf_spec],
        out_shape=[jax.ShapeDtypeStruct((m, d), F32),
                   jax.ShapeDtypeStruct((m // tm, SUBLANES, dff), F32),
                   jax.ShapeDtypeStruct((m // tm, SUBLANES, dff), F32)],
        scratch_shapes=[pltpu.VMEM((tm, d), BF16),
                        pltpu.VMEM((tm + SUBLANES, tf), F32), pltpu.VMEM((tm + SUBLANES, tf), F32),
                        pltpu.VMEM((nf, SUBLANES, tf), F32), pltpu.VMEM((nf, SUBLANES, tf), F32),
                        pltpu.VMEM((tm, d), F32)],
        compiler_params=_cparams("arbitrary", "arbitrary"),
        name="ffn_prompt",
    )(y, g, w_in, w_in, conv_w, conv_w, conv_b, conv_b, w_out)
    keep = FFN_CONV - 1
    last = slice(tiles_per_seq - 1, None, tiles_per_seq)
    new_buf = jnp.concatenate([bufg[last, SUBLANES - keep:], bufv[last, SUBLANES - keep:]], axis=-1)
    return out, new_buf


def _ffn_sample_kernel(y_ref, g_ref, wg_ref, wv_ref, cwg_ref, cwv_ref, cbg_ref, cbv_ref,
                       sg_ref, sv_ref, wo_ref, out_ref, ng_ref, nv_ref, h_scr, acc_scr):
    f = pl.program_id(0)

    @pl.when(f == 0)
    def _():
        h_scr[...] = _rms_rows(y_ref[...], g_ref[...]).astype(BF16)
        acc_scr[...] = jnp.zeros(acc_scr.shape, F32)

    def branch(w_ref, cw_ref, cb_ref, st_ref, new_ref):
        u = _dot(h_scr[...], w_ref[...])
        r = cw_ref[FFN_CONV - 1:FFN_CONV, :] * u + cb_ref[...]
        for i in range(FFN_CONV - 1):
            r = r + cw_ref[i:i + 1, :] * st_ref[i]
        for i in range(FFN_CONV - 2):
            new_ref[i] = st_ref[i + 1]
        new_ref[FFN_CONV - 2] = u
        return r

    gate = branch(wg_ref, cwg_ref, cbg_ref, sg_ref, ng_ref)
    val = branch(wv_ref, cwv_ref, cbv_ref, sv_ref, nv_ref)
    acc_scr[...] += _dot((_silu(gate) * val).astype(BF16), wo_ref[...])

    @pl.when(f == pl.num_programs(0) - 1)
    def _():
        out_ref[...] = y_ref[...] + acc_scr[...]


def _ffn_sample(y, g, w_in, conv_w, conv_b, w_out, state):
    bd, d = y.shape
    dff = w_out.shape[0]
    tf = _ffn_tile(dff)
    nf = dff // tf
    keep = FFN_CONV - 1
    st = jnp.transpose(state, (1, 0, 2))
    gate = lambda f: (0, f)
    val = lambda f: (0, f + nf)
    st_gate = pl.BlockSpec((keep, bd, tf), lambda f: (0, 0, f))
    st_val = pl.BlockSpec((keep, bd, tf), lambda f: (0, 0, f + nf))
    new_spec = pl.BlockSpec((keep, bd, tf), lambda f: (0, 0, f))
    out, ng, nv = pl.pallas_call(
        _ffn_sample_kernel,
        grid=(nf,),
        in_specs=[pl.BlockSpec((bd, d), lambda f: (0, 0)),
                  pl.BlockSpec((1, d), lambda f: (0, 0)),
                  pl.BlockSpec((d, tf), gate), pl.BlockSpec((d, tf), val),
                  pl.BlockSpec((FFN_CONV, tf), gate), pl.BlockSpec((FFN_CONV, tf), val),
                  pl.BlockSpec((1, tf), gate), pl.BlockSpec((1, tf), val),
                  st_gate, st_val,
                  pl.BlockSpec((tf, d), lambda f: (f, 0))],
        out_specs=[pl.BlockSpec((bd, d), lambda f: (0, 0)), new_spec, new_spec],
        out_shape=[jax.ShapeDtypeStruct((bd, d), F32),
                   jax.ShapeDtypeStruct((keep, bd, dff), F32),
                   jax.ShapeDtypeStruct((keep, bd, dff), F32)],
        scratch_shapes=[pltpu.VMEM((bd, d), BF16), pltpu.VMEM((bd, d), F32)],
        compiler_params=_cparams("arbitrary"),
        name="ffn_sample",
    )(y, g, w_in, w_in, conv_w, conv_w, conv_b, conv_b, st, st, w_out)
    new_state = jnp.transpose(jnp.concatenate([ng, nv], axis=-1), (1, 0, 2))
    return out, new_state


def _final_norm_kernel(x_ref, g_ref, o_ref):
    o_ref[...] = _rms_rows(x_ref[...], g_ref[...])


def _final_norm(x, g):
    m, d = x.shape
    tm = min(1024, m)
    return pl.pallas_call(
        _final_norm_kernel,
        grid=(m // tm,),
        in_specs=[pl.BlockSpec((tm, d), lambda i: (i, 0)), pl.BlockSpec((1, d), lambda i: (0, 0))],
        out_specs=pl.BlockSpec((tm, d), lambda i: (i, 0)),
        out_shape=jax.ShapeDtypeStruct((m, d), F32),
        compiler_params=_cparams("parallel"),
        name="final_norm",
    )(x, g)


def _rope_tables(pos):
    half = HEAD_DIM // 2
    inv = ROPE_THETA ** (-jnp.arange(half, dtype=F32) / half)
    ang = pos.astype(F32)[:, None] * inv[None, :]
    cos, sin = jnp.cos(ang), jnp.sin(ang)
    return jnp.concatenate([cos, cos], axis=-1), jnp.concatenate([-sin, sin], axis=-1)


def _dn_gate_layout(w_in, a_log, dt_bias):
    d = w_in.shape[0]
    rep = DN_V_HEADS // DN_QK_HEADS
    w_beta = w_in[:, DN_MAIN_DIM:DN_MAIN_DIM + DN_V_HEADS].reshape(d, DN_QK_HEADS, rep)
    w_dec = w_in[:, DN_MAIN_DIM + DN_V_HEADS:].reshape(d, DN_QK_HEADS, rep)
    pad = jnp.zeros((d, DN_QK_HEADS, SUBLANES - 2 * rep), w_in.dtype)
    w_bg = jnp.concatenate([w_beta, w_dec, pad], axis=-1).reshape(d, DN_QK_HEADS * SUBLANES)
    w_bg = jnp.pad(w_bg, ((0, 0), (0, LANES - w_bg.shape[1])))

    def lanes(per_head, fill):
        v = per_head.reshape(DN_QK_HEADS, rep)
        z = jnp.full((DN_QK_HEADS, rep), fill, F32)
        p = jnp.full((DN_QK_HEADS, SUBLANES - 2 * rep), fill, F32)
        return v, z, p

    a, zero, padz = lanes(-jnp.exp(a_log.astype(F32)), 0.0)
    neg_a = jnp.concatenate([zero, a, padz], axis=-1).reshape(-1)
    dt, _, _ = lanes(dt_bias.astype(F32), 0.0)
    dt_b = jnp.concatenate([zero, dt, padz], axis=-1).reshape(-1)
    is_beta = jnp.concatenate([zero + 1.0, zero, padz], axis=-1).reshape(-1)
    par = jnp.stack([neg_a, dt_b, is_beta])
    par = jnp.pad(par, ((0, SUBLANES - par.shape[0]), (0, LANES - par.shape[1])))
    return w_bg.astype(BF16), par


def kernel(x_prompt, x_sample, cache_k, cache_v, state_dn, state_dn_conv, state_ffn_conv, page_table,
           norm_mix, norm_ffn, norm_out, dn_w_in, dn_conv_w, dn_a_log, dn_dt_bias, dn_norm, dn_w_out,
           mb_w_qkv, mb_w_o, ffn_w_in, ffn_conv_w, ffn_conv_b, ffn_w_out):
    bsz, seq, d = x_prompt.shape
    bd = x_sample.shape[0]
    depth = norm_mix.shape[0]
    n_pages = page_table.shape[1]
    past = n_pages * PAGE_SIZE
    own_blk = past // MB_BLOCK
    hd = HEAD_DIM

    yp = x_prompt.reshape(bsz * seq, d)
    ys = x_sample.reshape(bd, d)
    cos_p, sin_p = _rope_tables(jnp.arange(seq, dtype=I32))
    cos_s, sin_s = _rope_tables(jnp.full((bd,), past, dtype=I32))
    pool_shape = cache_k.shape[:3] + (MB_HEADS * hd,)
    pool_k = cache_k.reshape(pool_shape)
    pool_v = cache_v.reshape(pool_shape)

    kp_l, vp_l, ks_l, vs_l = [], [], [], []
    sp_l, ss_l, cp_l, cs_l = [], [], [], []
    fp_l, fs_l = [], []
    for layer in range(depth):
        j = layer // 2
        g_mix = norm_mix[layer].reshape(1, d)
        if layer % 2 == 0:
            w_main = dn_w_in[j][:, :DN_MAIN_DIM].astype(BF16)
            w_bg, par = _dn_gate_layout(dn_w_in[j], dn_a_log[j], dn_dt_bias[j])
            w_out = dn_w_out[j].astype(BF16)
            nw = dn_norm[j].reshape(1, hd)
            proj, bg, bgt = _dn_inproj(yp, g_mix, w_main, w_bg, par, transposed=True)
            o, s_new = _dn_core(proj, bg, bgt, dn_conv_w[j], nw, bsz, seq)
            yp = _matmul_residual(o, w_out, yp)
            sp_l.append(s_new)
            cp_l.append(proj.reshape(bsz, seq, DN_MAIN_DIM)[:, seq - (DN_CONV - 1):, :DN_CONV_DIM])

            proj_s, bg_s = _dn_inproj(ys, g_mix, w_main, w_bg, par, transposed=False)
            o_s, s_s, c_s = _dn_sample(proj_s, bg_s, state_dn_conv[j], dn_conv_w[j], state_dn[j], nw)
            ys = _matmul_residual(o_s, w_out, ys)
            ss_l.append(s_s)
            cs_l.append(c_s)
        else:
            w_qkv = mb_w_qkv[j].astype(BF16)
            w_o = mb_w_o[j].astype(BF16)
            q, k, v, kb, vt, km = _mb_qkv(yp, g_mix, w_qkv, cos_p, sin_p, with_blocks=True)
            att = _mb_attn(q, kb, vt, km, bsz, seq)
            yp = _matmul_residual(att, w_o, yp)
            kp_l.append(k.reshape(bsz, seq, MB_HEADS, hd))
            vp_l.append(v.reshape(bsz, seq, MB_HEADS, hd))

            q_s, k_s, v_s = _mb_qkv(ys, g_mix, w_qkv, cos_s, sin_s, with_blocks=False)
            blk_sums = _page_block_sums(pool_k, page_table, j)
            q3 = q_s.reshape(bd, MB_HEADS, hd)
            k3 = k_s.reshape(bd, MB_HEADS, hd)
            sel = _mb_sample_select(q3, k3, blk_sums, own_blk)
            att_s = _mb_sample_attn(pool_k, pool_v, page_table, sel, q3, k3, v_s.reshape(bd, MB_HEADS, hd), j)
            ys = _matmul_residual(att_s.astype(BF16), w_o, ys)
            ks_l.append(k_s.reshape(bd, 1, MB_HEADS, hd))
            vs_l.append(v_s.reshape(bd, 1, MB_HEADS, hd))

        g_ffn = norm_ffn[layer].reshape(1, d)
        w_fi = ffn_w_in[layer].astype(BF16)
        w_fo = ffn_w_out[layer].astype(BF16)
        cb = ffn_conv_b[layer].reshape(1, -1)
        yp, fbp = _ffn_prompt(yp, g_ffn, w_fi, ffn_conv_w[layer], cb, w_fo, bsz, seq)
        ys, fbs = _ffn_sample(ys, g_ffn, w_fi, ffn_conv_w[layer], cb, w_fo, state_ffn_conv[layer])
        fp_l.append(fbp)
        fs_l.append(fbs)

    g_out = norm_out.reshape(1, d)
    y_prompt = _final_norm(yp, g_out).reshape(bsz, seq, d)
    y_sample = _final_norm(ys, g_out).reshape(bd, 1, d)
    return (y_prompt, y_sample, jnp.stack(kp_l), jnp.stack(vp_l), jnp.stack(ks_l), jnp.stack(vs_l),
            jnp.stack(sp_l), jnp.stack(ss_l), jnp.stack(cp_l), jnp.stack(cs_l), jnp.stack(fp_l), jnp.stack(fs_l))
```

```python
import functools

import jax
import jax.numpy as jnp
from jax import lax
from jax.experimental import pallas as pl
from jax.experimental.pallas import tpu as pltpu

F32 = jnp.float32
BF16 = jnp.bfloat16
I32 = jnp.int32

NORM_EPS = 1e-6
ROPE_THETA = 10000.0
HEAD_DIM = 128
DN_QK_HEADS = 8
DN_V_HEADS = 16
DN_CONV = 4
DN_KEY_DIM = DN_QK_HEADS * HEAD_DIM
DN_VAL_DIM = DN_V_HEADS * HEAD_DIM
DN_CONV_DIM = 2 * DN_KEY_DIM + DN_VAL_DIM
DN_MAIN_DIM = DN_CONV_DIM + DN_VAL_DIM
DN_CHUNK = 128
DN_TBLOCK = 512
MB_HEADS = 8
MB_BLOCK = 256
MB_TOP_K = 3
PAGE_SIZE = 128
FFN_CONV = 3
SUBLANES = 8
LANES = 128
NEG_BIG = -1e30
VMEM_LIMIT = 56 * 1024 * 1024


def _cparams(*sem):
    return pltpu.CompilerParams(dimension_semantics=sem, vmem_limit_bytes=VMEM_LIMIT)


def _sigmoid(x):
    return 1.0 / (1.0 + jnp.exp(-x))


def _silu(x):
    return x * _sigmoid(x)


def _softplus(x):
    return jnp.maximum(x, 0.0) + jnp.log1p(jnp.exp(-jnp.abs(x)))


def _rms_rows(x, g):
    ms = jnp.mean(x * x, axis=-1, keepdims=True)
    return x * lax.rsqrt(ms + NORM_EPS) * g


def _dot(a, b):
    return jnp.dot(a, b, preferred_element_type=F32)


def _dot_nt(a, b):
    return lax.dot_general(a, b, (((1,), (1,)), ((), ())), preferred_element_type=F32)


def _dot_tn(a, b):
    return lax.dot_general(a, b, (((0,), (0,)), ((), ())), preferred_element_type=F32)


def _split2(a):
    hi = a.astype(BF16)
    lo = (a - hi.astype(F32)).astype(BF16)
    return hi, lo


def _mm3(a, b):
    (ah, al), (bh, bl) = a, b
    return _dot(ah, bh) + (_dot(ah, bl) + _dot(al, bh))


def _dot_exact01(m01, x):
    hi = x.astype(BF16)
    r1 = x - hi.astype(F32)
    mid = r1.astype(BF16)
    lo = (r1 - mid.astype(F32)).astype(BF16)
    return _dot(m01, hi) + (_dot(m01, mid) + _dot(m01, lo))


def _dot_exact01_r(x, m01):
    hi = x.astype(BF16)
    r1 = x - hi.astype(F32)
    mid = r1.astype(BF16)
    lo = (r1 - mid.astype(F32)).astype(BF16)
    return _dot(hi, m01) + (_dot(mid, m01) + _dot(lo, m01))


def _dn_inproj_kernel(x_ref, g_ref, w_ref, wbg_ref, par_ref, o_ref, bg_ref, *rest, transposed):
    if transposed:
        bgt_ref, h_scr = rest
    else:
        (h_scr,) = rest

    @pl.when(pl.program_id(1) == 0)
    def _():
        h = _rms_rows(x_ref[...], g_ref[...]).astype(BF16)
        h_scr[...] = h
        raw = _dot(h, wbg_ref[...])
        neg_a, dt_b, is_beta = par_ref[0:1, :], par_ref[1:2, :], par_ref[2:3, :]
        act = jnp.where(is_beta > 0.5, _sigmoid(raw), neg_a * _softplus(raw + dt_b))
        bg_ref[...] = act
        if transposed:
            bgt_ref[...] = act.T

    o_ref[...] = _dot(h_scr[...], w_ref[...])


def _dn_inproj(x, g, w_main, w_bg, par, *, transposed):
    m, d = x.shape
    n = w_main.shape[1]
    tm = min(512, m)
    tn = 2048
    out_shape = [jax.ShapeDtypeStruct((m, n), F32), jax.ShapeDtypeStruct((m, LANES), F32)]
    out_specs = [pl.BlockSpec((tm, tn), lambda i, j: (i, j)),
                 pl.BlockSpec((tm, LANES), lambda i, j: (i, 0))]
    if transposed:
        out_shape.append(jax.ShapeDtypeStruct((LANES, m), F32))
        out_specs.append(pl.BlockSpec((LANES, tm), lambda i, j: (0, i)))
    return pl.pallas_call(
        functools.partial(_dn_inproj_kernel, transposed=transposed),
        grid=(m // tm, n // tn),
        in_specs=[pl.BlockSpec((tm, d), lambda i, j: (i, 0)),
                  pl.BlockSpec((1, d), lambda i, j: (0, 0)),
                  pl.BlockSpec((d, tn), lambda i, j: (0, j)),
                  pl.BlockSpec((d, LANES), lambda i, j: (0, 0)),
                  pl.BlockSpec((SUBLANES, LANES), lambda i, j: (0, 0))],
        out_specs=out_specs,
        out_shape=out_shape,
        scratch_shapes=[pltpu.VMEM((tm, d), BF16)],
        compiler_params=_cparams("parallel", "arbitrary"),
        name="dn_inproj",
    )(x, g, w_main, w_bg, par)


def _unit_lower_inverses(lows, c):
    row = lax.broadcasted_iota(I32, (c, c), 0)
    col = lax.broadcasted_iota(I32, (c, c), 1)

    def same_block(shift):
        return (row >> shift) == (col >> shift)

    low_parts = [_split2(low) for low in lows]
    base = same_block(3)
    negs = [jnp.where(base, -low, 0.0) for low in lows]
    eye = jnp.where(row == col, 1.0, 0.0)
    invs = [eye + neg for neg in negs]
    powers = [_split2(neg) for neg in negs]
    for _ in range(2):
        powers = [_split2(_mm3(p, p)) for p in powers]
        invs = [inv + _mm3(_split2(inv), p) for inv, p in zip(invs, powers)]
    shift = 3
    while (1 << shift) < c:
        mask = same_block(shift + 1) & jnp.logical_not(same_block(shift))
        inv_parts = [_split2(inv) for inv in invs]
        tmps = [_split2(jnp.where(mask, _mm3(lp, ip), 0.0)) for lp, ip in zip(low_parts, inv_parts)]
        invs = [inv - _mm3(ip, t) for inv, ip, t in zip(invs, inv_parts, tmps)]
        shift += 1
    return invs


def _dn_core_kernel(q_ref, k_ref, v_ref, z_ref, bg_ref, bgt_ref, cwq_ref, cwk_ref, cwv_ref, nw_ref,
                    o_ref, s_out_ref, xe_scr, s_scr, *, tb, chunk):
    hq = pl.program_id(1)
    t = pl.program_id(2)
    hd = HEAD_DIM

    @pl.when(t == 0)
    def _():
        s_scr[...] = jnp.zeros(s_scr.shape, F32)
        xe_scr[0:SUBLANES, :] = jnp.zeros((SUBLANES, 4 * hd), F32)

    xe_scr[SUBLANES:SUBLANES + tb, 0:hd] = q_ref[...]
    xe_scr[SUBLANES:SUBLANES + tb, hd:2 * hd] = k_ref[...]
    xe_scr[SUBLANES:SUBLANES + tb, 2 * hd:4 * hd] = v_ref[...]

    def conv(lo, hi, cw_ref):
        acc = cw_ref[DN_CONV - 1:DN_CONV, :] * xe_scr[SUBLANES:SUBLANES + tb, lo:hi]
        for d in range(1, DN_CONV):
            acc = acc + (cw_ref[DN_CONV - 1 - d:DN_CONV - d, :]
                         * xe_scr[SUBLANES - d:SUBLANES - d + tb, lo:hi])
        return _silu(acc)

    qc = conv(0, hd, cwq_ref)
    kc = conv(hd, 2 * hd, cwk_ref)
    vc = conv(2 * hd, 4 * hd, cwv_ref)
    xe_scr[0:SUBLANES, :] = xe_scr[tb:tb + SUBLANES, :]

    qn = qc * lax.rsqrt(jnp.sum(qc * qc, axis=-1, keepdims=True) + NORM_EPS) * (hd ** -0.5)
    kn = kc * lax.rsqrt(jnp.sum(kc * kc, axis=-1, keepdims=True) + NORM_EPS)

    bgr = pltpu.roll(bg_ref[...], (LANES - SUBLANES * hq) % LANES, axis=1)
    bgt = bgt_ref[...]
    ri = lax.broadcasted_iota(I32, (tb, tb), 0)
    ci = lax.broadcasted_iota(I32, (tb, tb), 1)
    cshift = chunk.bit_length() - 1
    same_chunk = (ri >> cshift) == (ci >> cshift)
    upper = jnp.where(same_chunk & (ri <= ci), 1.0, 0.0).astype(BF16)
    lower = jnp.where(same_chunk & (ri >= ci), 1.0, 0.0).astype(BF16)
    gc_rows = _dot_exact01_r(bgt, upper)
    gc_cols = _dot_exact01(lower, bgr)

    rc = lax.broadcasted_iota(I32, (chunk, chunk), 0)
    cc = lax.broadcasted_iota(I32, (chunk, chunk), 1)
    nw = nw_ref[...]

    items = []
    for c in range(tb // chunk):
        r0, r1 = c * chunk, (c + 1) * chunk
        q_c, k_c = qn[r0:r1], kn[r0:r1]
        kb = k_c.astype(BF16)
        kk = _dot_nt(kb, kb)
        qk = _dot_nt(q_c.astype(BF16), kb)
        for x in range(2):
            g_col = gc_cols[r0:r1, 2 + x:3 + x]
            g_row = gc_rows[2 + x:3 + x, r0:r1]
            b_col = bgr[r0:r1, x:x + 1]
            decay = jnp.exp(jnp.where(rc >= cc, g_col - g_row, -jnp.inf))
            eg = jnp.exp(g_col)
            g_last = g_col[chunk - 1:chunk, :]
            v_c = vc[r0:r1, x * hd:(x + 1) * hd]
            items.append(dict(
                c=c, x=x,
                low=jnp.where(rc > cc, b_col * decay * kk, 0.0),
                rhs=jnp.concatenate([b_col * v_c, (b_col * eg) * k_c], axis=1).astype(BF16),
                qg=(q_c * eg).astype(BF16),
                qkd=(qk * decay).astype(BF16),
                kd=(k_c * jnp.exp(g_last - g_col)).astype(BF16),
                gl=jnp.exp(g_last)))
    invs = _unit_lower_inverses([it["low"] for it in items], chunk)
    sols = [_dot(inv.astype(BF16), it["rhs"]) for inv, it in zip(invs, items)]

    states = [s_scr[x] for x in range(2)]
    for c in range(tb // chunk):
        r0, r1 = c * chunk, (c + 1) * chunk
        pair = [(it, sol) for it, sol in zip(items, sols) if it["c"] == c]
        sbs = [states[it["x"]].astype(BF16) for it, _ in pair]
        us = [sol[:, :hd] - _dot(sol[:, hd:].astype(BF16), sb) for (_, sol), sb in zip(pair, sbs)]
        ubs = [u.astype(BF16) for u in us]
        outs = [_dot(it["qg"], sb) + _dot(it["qkd"], ub) for (it, _), sb, ub in zip(pair, sbs, ubs)]
        for (it, _), ub, o in zip(pair, ubs, outs):
            x = it["x"]
            states[x] = it["gl"] * states[x] + _dot_tn(it["kd"], ub)
            z_c = z_ref[r0:r1, x * hd:(x + 1) * hd]
            on = o * lax.rsqrt(jnp.mean(o * o, axis=-1, keepdims=True) + NORM_EPS) * nw * _silu(z_c)
            o_ref[r0:r1, x * hd:(x + 1) * hd] = on.astype(o_ref.dtype)
    for x in range(2):
        s_scr[x] = states[x]

    @pl.when(t == pl.num_programs(2) - 1)
    def _():
        s_out_ref[...] = s_scr[...]


def _dn_core(proj, bg, bgt, conv_w, norm_w, bsz, seq):
    tb, hd = DN_TBLOCK, HEAD_DIM
    nt = seq // tb
    kernel = functools.partial(_dn_core_kernel, tb=tb, chunk=DN_CHUNK)
    tok = lambda b, h, t: b * nt + t
    return pl.pallas_call(
        kernel,
        grid=(bsz, DN_QK_HEADS, nt),
        in_specs=[
            pl.BlockSpec((tb, hd), lambda b, h, t: (tok(b, h, t), h)),
            pl.BlockSpec((tb, hd), lambda b, h, t: (tok(b, h, t), DN_QK_HEADS + h)),
            pl.BlockSpec((tb, 2 * hd), lambda b, h, t: (tok(b, h, t), DN_QK_HEADS + h)),
            pl.BlockSpec((tb, 2 * hd), lambda b, h, t: (tok(b, h, t), 2 * DN_QK_HEADS + h)),
            pl.BlockSpec((tb, LANES), lambda b, h, t: (tok(b, h, t), 0)),
            pl.BlockSpec((SUBLANES, tb), lambda b, h, t: (h, tok(b, h, t))),
            pl.BlockSpec((DN_CONV, hd), lambda b, h, t: (0, h)),
            pl.BlockSpec((DN_CONV, hd), lambda b, h, t: (0, DN_QK_HEADS + h)),
            pl.BlockSpec((DN_CONV, 2 * hd), lambda b, h, t: (0, DN_QK_HEADS + h)),
            pl.BlockSpec((1, hd), lambda b, h, t: (0, 0)),
        ],
        out_specs=[
            pl.BlockSpec((tb, 2 * hd), lambda b, h, t: (tok(b, h, t), h)),
            pl.BlockSpec((None, 2, hd, hd), lambda b, h, t: (b, h, 0, 0)),
        ],
        out_shape=[jax.ShapeDtypeStruct((bsz * seq, DN_VAL_DIM), BF16),
                   jax.ShapeDtypeStruct((bsz, DN_V_HEADS, hd, hd), F32)],
        scratch_shapes=[pltpu.VMEM((tb + SUBLANES, 4 * hd), F32),
                        pltpu.VMEM((2, hd, hd), F32)],
        compiler_params=_cparams("parallel", "parallel", "arbitrary"),
        name="dn_core",
    )(proj, proj, proj, proj, bg, bgt, conv_w, conv_w, conv_w, norm_w)


def _dn_sample_kernel(p_ref, bg_ref, cst_ref, cw_ref, s_ref, nw_ref,
                      o_ref, s_out_ref, cst_out_ref, r_scr):
    hd = HEAD_DIM
    x = p_ref[:, 0:DN_CONV_DIM]
    buf = cst_ref[...]
    y = x * cw_ref[DN_CONV - 1:DN_CONV, :]
    for i in range(DN_CONV - 1):
        y = y + buf[i:i + 1, :] * cw_ref[i:i + 1, :]
    cst_out_ref[0:DN_CONV - 2, :] = buf[1:DN_CONV - 1, :]
    cst_out_ref[DN_CONV - 2:DN_CONV - 1, :] = x
    a = _silu(y)

    r_scr[...] = jnp.zeros(r_scr.shape, F32)
    qs, ks = [], []
    for h in range(DN_QK_HEADS):
        qh = a[:, h * hd:(h + 1) * hd]
        kh = a[:, DN_KEY_DIM + h * hd:DN_KEY_DIM + (h + 1) * hd]
        qh = qh * lax.rsqrt(jnp.sum(qh * qh, axis=-1, keepdims=True) + NORM_EPS) * (hd ** -0.5)
        kh = kh * lax.rsqrt(jnp.sum(kh * kh, axis=-1, keepdims=True) + NORM_EPS)
        qs.append(qh)
        ks.append(kh)
        r_scr[h:h + 1, :] = qh
        r_scr[DN_QK_HEADS + h:DN_QK_HEADS + h + 1, :] = kh
    rt = r_scr[...].T

    bg = bg_ref[...]
    nw = nw_ref[...]
    rep = DN_V_HEADS // DN_QK_HEADS
    for h in range(DN_V_HEADS):
        hq, x_in = h // rep, h % rep
        beta = bg[:, SUBLANES * hq + x_in:SUBLANES * hq + x_in + 1]
        g = bg[:, SUBLANES * hq + 2 + x_in:SUBLANES * hq + 3 + x_in]
        eg = jnp.exp(g)
        q_col = rt[:, hq:hq + 1]
        k_col = rt[:, DN_QK_HEADS + hq:DN_QK_HEADS + hq + 1]
        s_old = s_ref[h]
        k_s = jnp.sum(k_col * s_old, axis=0, keepdims=True)
        q_s = jnp.sum(q_col * s_old, axis=0, keepdims=True)
        v_h = a[:, 2 * DN_KEY_DIM + h * hd:2 * DN_KEY_DIM + (h + 1) * hd]
        u = beta * v_h - (beta * eg) * k_s
        qk = jnp.sum(qs[hq] * ks[hq], axis=-1, keepdims=True)
        o = eg * q_s + qk * u
        s_out_ref[h] = eg * s_old + k_col * u
        z_h = p_ref[:, DN_CONV_DIM + h * hd:DN_CONV_DIM + (h + 1) * hd]
        on = o * lax.rsqrt(jnp.mean(o * o, axis=-1, keepdims=True) + NORM_EPS) * nw * _silu(z_h)
        o_ref[:, h * hd:(h + 1) * hd] = on.astype(o_ref.dtype)


def _dn_sample(proj, bg, conv_state, conv_w, state, norm_w):
    bd = proj.shape[0]
    hd = HEAD_DIM
    o, s_new, c_new = pl.pallas_call(
        _dn_sample_kernel,
        grid=(bd,),
        in_specs=[
            pl.BlockSpec((None, 1, DN_MAIN_DIM), lambda b: (b, 0, 0)),
            pl.BlockSpec((None, 1, LANES), lambda b: (b, 0, 0)),
            pl.BlockSpec((None, DN_CONV - 1, DN_CONV_DIM), lambda b: (b, 0, 0)),
            pl.BlockSpec((DN_CONV, DN_CONV_DIM), lambda b: (0, 0)),
            pl.BlockSpec((None, DN_V_HEADS, hd, hd), lambda b: (b, 0, 0, 0)),
            pl.BlockSpec((1, hd), lambda b: (0, 0)),
        ],
        out_specs=[
            pl.BlockSpec((None, 1, DN_VAL_DIM), lambda b: (b, 0, 0)),
            pl.BlockSpec((None, DN_V_HEADS, hd, hd), lambda b: (b, 0, 0, 0)),
            pl.BlockSpec((None, DN_CONV - 1, DN_CONV_DIM), lambda b: (b, 0, 0)),
        ],
        out_shape=[jax.ShapeDtypeStruct((bd, 1, DN_VAL_DIM), BF16),
                   jax.ShapeDtypeStruct(state.shape, F32),
                   jax.ShapeDtypeStruct(conv_state.shape, F32)],
        scratch_shapes=[pltpu.VMEM((LANES, LANES), F32)],
        compiler_params=_cparams("parallel"),
        name="dn_sample",
    )(proj.reshape(bd, 1, DN_MAIN_DIM), bg.reshape(bd, 1, LANES), conv_state, conv_w, state, norm_w)
    return o.reshape(bd, DN_VAL_DIM), s_new, c_new


def _matmul_residual_kernel(a_ref, w_ref, y_ref, o_ref):
    o_ref[...] = y_ref[...] + _dot(a_ref[...], w_ref[...])


def _matmul_residual(a, w, y):
    m, k = a.shape
    n = w.shape[1]
    tm = min(512, m)
    return pl.pallas_call(
        _matmul_residual_kernel,
        grid=(m // tm,),
        in_specs=[pl.BlockSpec((tm, k), lambda i: (i, 0)),
                  pl.BlockSpec((k, n), lambda i: (0, 0)),
                  pl.BlockSpec((tm, n), lambda i: (i, 0))],
        out_specs=pl.BlockSpec((tm, n), lambda i: (i, 0)),
        out_shape=jax.ShapeDtypeStruct((m, n), F32),
        compiler_params=_cparams("parallel"),
        name="matmul_residual",
    )(a, w, y)


def _mb_qkv_kernel(x_ref, g_ref, w_ref, cos_ref, sin_ref, q_ref, k_ref, v_ref, *rest,
                   with_blocks, blocks_per_seq):
    if with_blocks:
        kb_ref, vt_ref, km_ref, h_scr = rest
    else:
        (h_scr,) = rest
    i, j = pl.program_id(0), pl.program_id(1)
    hd = HEAD_DIM

    @pl.when(j == 0)
    def _():
        h_scr[...] = _rms_rows(x_ref[...], g_ref[...]).astype(BF16)

    y = _dot(h_scr[...], w_ref[...])
    tm = y.shape[0]

    def rope(val):
        cos, sin = cos_ref[...], sin_ref[...]
        parts = []
        for h in range(MB_HEADS):
            yh = val[:, h * hd:(h + 1) * hd]
            parts.append(yh * cos + pltpu.roll(yh, hd // 2, axis=1) * sin)
        return jnp.concatenate(parts, axis=1)

    @pl.when(j == 0)
    def _():
        q_ref[...] = rope(y).astype(q_ref.dtype)

    @pl.when(j == 1)
    def _():
        k = rope(y)
        k_ref[...] = k
        if with_blocks:
            row = lax.broadcasted_iota(I32, (tm, LANES), 0)
            lane = lax.broadcasted_iota(I32, (tm, LANES), 1)
            first_blk = (i * (tm // MB_BLOCK)) % blocks_per_seq
            onehot = jnp.where(lane == first_blk + row // MB_BLOCK, 1.0, 0.0).astype(BF16)
            for h in range(MB_HEADS):
                kb_ref[:, 2 * h * hd:(2 * h + 1) * hd] = k[:, h * hd:(h + 1) * hd].astype(BF16)
                kb_ref[:, (2 * h + 1) * hd:(2 * h + 2) * hd] = onehot
            for r in range(tm // MB_BLOCK):
                blk = k[r * MB_BLOCK:(r + 1) * MB_BLOCK, :]
                km_ref[r] = jnp.sum(blk, axis=0, keepdims=True) * (1.0 / MB_BLOCK)

    @pl.when(j == 2)
    def _():
        v_ref[...] = y
        if with_blocks:
            ones = jnp.ones((SUBLANES, MB_BLOCK), BF16)
            rows = hd + SUBLANES
            for r in range(tm // MB_BLOCK):
                vt = y[r * MB_BLOCK:(r + 1) * MB_BLOCK, :].T.astype(BF16)
                for h in range(MB_HEADS):
                    vt_ref[r, h * rows:h * rows + hd, :] = vt[h * hd:(h + 1) * hd, :]
                    vt_ref[r, h * rows + hd:(h + 1) * rows, :] = ones


def _mb_qkv(x, g, w, cos, sin, *, with_blocks):
    m, d = x.shape
    tm = min(512, m)
    n_pos_tiles = cos.shape[0] // tm
    out_shape = [jax.ShapeDtypeStruct((m, d), BF16 if with_blocks else F32),
                 jax.ShapeDtypeStruct((m, d), F32), jax.ShapeDtypeStruct((m, d), F32)]
    row = pl.BlockSpec((tm, d), lambda i, j: (i, 0))
    out_specs = [row, row, row]
    if with_blocks:
        nb = tm // MB_BLOCK
        vt_rows = MB_HEADS * (HEAD_DIM + SUBLANES)
        out_shape += [jax.ShapeDtypeStruct((m, 2 * d), BF16),
                      jax.ShapeDtypeStruct((m // MB_BLOCK, vt_rows, MB_BLOCK), BF16),
                      jax.ShapeDtypeStruct((m // MB_BLOCK, 1, d), F32)]
        out_specs += [pl.BlockSpec((tm, 2 * d), lambda i, j: (i, 0)),
                      pl.BlockSpec((nb, vt_rows, MB_BLOCK), lambda i, j: (i, 0, 0)),
                      pl.BlockSpec((nb, 1, d), lambda i, j: (i, 0, 0))]
    return pl.pallas_call(
        functools.partial(_mb_qkv_kernel, with_blocks=with_blocks,
                          blocks_per_seq=cos.shape[0] // MB_BLOCK if with_blocks else 0),
        grid=(m // tm, 3),
        in_specs=[row,
                  pl.BlockSpec((1, d), lambda i, j: (0, 0)),
                  pl.BlockSpec((d, d), lambda i, j: (0, j)),
                  pl.BlockSpec((tm, HEAD_DIM), lambda i, j: (i % n_pos_tiles, 0)),
                  pl.BlockSpec((tm, HEAD_DIM), lambda i, j: (i % n_pos_tiles, 0))],
        out_specs=out_specs,
        out_shape=out_shape,
        scratch_shapes=[pltpu.VMEM((tm, d), BF16)],
        compiler_params=_cparams("parallel", "arbitrary"),
        name="mb_qkv",
    )(x, g, w, cos, sin)


def _mb_attn_kernel(q_ref, k_ref, vt_ref, km_ref, o_ref, st_a, st_b, *, group):
    n = pl.program_id(2)
    bq, hd = MB_BLOCK, HEAD_DIM
    c = (hd ** -0.5) * 1.4426950408889634
    q = q_ref[...]
    nb = km_ref.shape[0]

    blk = lax.broadcasted_iota(I32, (nb, bq), 0)
    s = jnp.where(blk < n, _dot_nt(km_ref[...].astype(BF16), q), -jnp.inf)
    bias = jnp.full((nb, bq), NEG_BIG, F32)
    for _ in range(MB_TOP_K):
        top = jnp.max(s, axis=0, keepdims=True)
        idx = jnp.min(jnp.where(s == top, blk, nb), axis=0, keepdims=True)
        hit = blk == idx
        bias = jnp.where(hit & (idx < n), 0.0, bias)
        s = jnp.where(hit, -jnp.inf, s)
    bias = jnp.concatenate([bias, jnp.full((LANES - nb, bq), NEG_BIG, F32)], axis=0)
    qa = jnp.concatenate([q, bias.T.astype(BF16)], axis=1)

    kpos = lax.broadcasted_iota(I32, (bq, bq), 0)
    qpos = lax.broadcasted_iota(I32, (bq, bq), 1)
    k_own = k_ref[pl.ds(pl.multiple_of(n * bq, bq), bq), 0:hd]
    st = jnp.where(kpos <= qpos, _dot_nt(k_own, q), NEG_BIG)
    m = jnp.max(st, axis=0, keepdims=True)
    p = jnp.exp2((st - m) * c)
    acc = _dot(vt_ref[n], p.astype(BF16))

    rows = group * bq
    last_group = nb // group - 1

    def fill(buf, g):
        g = jnp.minimum(g, last_group)
        kg = k_ref[pl.ds(pl.multiple_of(g * rows, rows), rows), :]
        buf[...] = _dot_nt(kg, qa)

    def consume(buf, g, m, acc):
        m_new = jnp.maximum(m, jnp.max(buf[...], axis=0, keepdims=True))
        p = jnp.exp2((buf[...] - m_new) * c).astype(BF16)
        vt = jnp.concatenate([vt_ref[g * group + i] for i in range(group)], axis=1)
        return m_new, acc * jnp.exp2((m - m_new) * c) + _dot(vt, p)

    def body(g2, carry):
        m, acc = carry
        fill(st_b, 2 * g2 + 1)
        m, acc = consume(st_a, 2 * g2, m, acc)
        fill(st_a, 2 * g2 + 2)
        return consume(st_b, 2 * g2 + 1, m, acc)

    fill(st_a, 0)
    m, acc = lax.fori_loop(0, (n + 2 * group - 1) // (2 * group), body, (m, acc))
    o_ref[...] = (acc[0:hd, :] / acc[hd:hd + 1, :]).T.astype(o_ref.dtype)


def _mb_attn(q, kb, vt, km, bsz, seq):
    hd, bq = HEAD_DIM, MB_BLOCK
    nb = seq // bq
    d = MB_HEADS * hd
    group = 2
    assert nb % (2 * group) == 0
    return pl.pallas_call(
        functools.partial(_mb_attn_kernel, group=group),
        grid=(bsz, MB_HEADS, nb),
        in_specs=[
            pl.BlockSpec((bq, hd), lambda b, h, n: (b * nb + n, h)),
            pl.BlockSpec((seq, 2 * hd), lambda b, h, n: (b, h)),
            pl.BlockSpec((nb, hd + SUBLANES, bq), lambda b, h, n: (b, h, 0)),
            pl.BlockSpec((nb, None, hd), lambda b, h, n: (b, 0, h)),
        ],
        out_specs=pl.BlockSpec((bq, hd), lambda b, h, n: (b * nb + n, h)),
        out_shape=jax.ShapeDtypeStruct((bsz * seq, d), BF16),
        scratch_shapes=[pltpu.VMEM((group * bq, bq), F32), pltpu.VMEM((group * bq, bq), F32)],
        compiler_params=_cparams("parallel", "parallel", "arbitrary"),
        name="mb_attn",
    )(q, kb, vt, km)


PAGES_PER_STEP = 16


def _page_sum_kernel(pt_ref, *refs):
    pages, o_ref = refs[:PAGES_PER_STEP], refs[PAGES_PER_STEP]
    per_blk = MB_BLOCK // PAGE_SIZE
    for r in range(PAGES_PER_STEP // per_blk):
        s = jnp.sum(pages[per_blk * r][...], axis=0)
        for i in range(1, per_blk):
            s = s + jnp.sum(pages[per_blk * r + i][...], axis=0)
        o_ref[r] = s


def _page_block_sums(pool, page_table, layer):
    bd, n_pages = page_table.shape
    heads, hd = pool.shape[-2:]
    steps = n_pages // PAGES_PER_STEP
    rows = PAGES_PER_STEP * PAGE_SIZE // MB_BLOCK

    def page_spec(i):
        return pl.BlockSpec(
            (None, None, PAGE_SIZE, heads, hd),
            lambda b, s, pt: (layer, pt[b * n_pages + s * PAGES_PER_STEP + i], 0, 0, 0))

    return pl.pallas_call(
        _page_sum_kernel,
        grid_spec=pltpu.PrefetchScalarGridSpec(
            num_scalar_prefetch=1,
            grid=(bd, steps),
            in_specs=[page_spec(i) for i in range(PAGES_PER_STEP)],
            out_specs=pl.BlockSpec((None, rows, heads, hd), lambda b, s, pt: (b, s, 0, 0)),
        ),
        out_shape=jax.ShapeDtypeStruct((bd, steps * rows, heads, hd), F32),
        compiler_params=_cparams("parallel", "arbitrary"),
        name="mb_page_sums",
    )(page_table.reshape(-1), *([pool] * PAGES_PER_STEP))


def _mb_sample_select_kernel(q_ref, kn_ref, bs_ref, sel_ref, *, own):
    nb = bs_ref.shape[0]
    n_rows = nb + SUBLANES
    inv_blk = 1.0 / MB_BLOCK
    q = q_ref[...]
    sc = jnp.sum(bs_ref[...] * inv_blk * q[None], axis=-1, keepdims=True)
    own_sc = jnp.sum(kn_ref[...] * inv_blk * q, axis=-1, keepdims=True)
    tail_row = lax.broadcasted_iota(I32, (SUBLANES, MB_HEADS, 1), 0)
    tail = jnp.where(tail_row == 0, own_sc[None], -jnp.inf)
    blk = lax.broadcasted_iota(I32, (n_rows, MB_HEADS, 1), 0)
    s = jnp.where(blk < own, jnp.concatenate([sc, tail], axis=0), -jnp.inf)
    out_lane = lax.broadcasted_iota(I32, (MB_HEADS, LANES), 1)
    out = jnp.zeros((MB_HEADS, LANES), I32)
    for r in range(MB_TOP_K):
        m = jnp.max(s, axis=0, keepdims=True)
        idx = jnp.min(jnp.where(s == m, blk, n_rows), axis=0, keepdims=True)
        s = jnp.where(blk == idx, -jnp.inf, s)
        idx = idx[0]
        out = jnp.where(out_lane == r, idx, out)
        out = jnp.where(out_lane == MB_TOP_K + 1 + r, (idx < own).astype(I32), out)
    sel_ref[...] = out


def _mb_sample_select(q, k_new, blk_sums, own):
    bd = q.shape[0]
    nb = blk_sums.shape[1]
    return pl.pallas_call(
        functools.partial(_mb_sample_select_kernel, own=own),
        grid=(bd,),
        in_specs=[pl.BlockSpec((None, MB_HEADS, HEAD_DIM), lambda b: (b, 0, 0)),
                  pl.BlockSpec((None, MB_HEADS, HEAD_DIM), lambda b: (b, 0, 0)),
                  pl.BlockSpec((None, nb, MB_HEADS, HEAD_DIM), lambda b: (b, 0, 0, 0))],
        out_specs=pl.BlockSpec((None, MB_HEADS, LANES), lambda b: (b, 0, 0)),
        out_shape=jax.ShapeDtypeStruct((bd, MB_HEADS, LANES), I32),
        compiler_params=_cparams("parallel"),
        name="mb_sample_select",
    )(q, k_new, blk_sums)


def _mb_sample_attn_kernel(pt_ref, sel_ref, q_ref, kn_ref, vn_ref, *refs):
    n_pg = MB_TOP_K * (MB_BLOCK // PAGE_SIZE)
    k_pages, v_pages, o_ref = refs[:n_pg], refs[n_pg:2 * n_pg], refs[2 * n_pg]
    b, h = pl.program_id(0), pl.program_id(1)
    scale = HEAD_DIM ** -0.5
    rows_pg = PAGE_SIZE * MB_HEADS
    q = q_ref[...]
    q8 = jnp.broadcast_to(q, (SUBLANES, HEAD_DIM)).astype(BF16)
    flat = lambda r: r[...].reshape(rows_pg, HEAD_DIM).astype(BF16)
    keys = jnp.concatenate([flat(r) for r in k_pages], axis=0)
    vals = jnp.concatenate([flat(r) for r in v_pages], axis=0)
    lg = _dot_nt(q8, keys) * scale
    lane = lax.broadcasted_iota(I32, lg.shape, 1)
    lg = jnp.where((lane & (MB_HEADS - 1)) == h, lg, NEG_BIG)
    base = (b * MB_HEADS + h) * SUBLANES
    blk_cols = rows_pg * (MB_BLOCK // PAGE_SIZE)
    for r in range(MB_TOP_K):
        penalty = jnp.where(sel_ref[base + MB_TOP_K + 1 + r] == 0, NEG_BIG, 0.0)
        in_blk = (lane >= r * blk_cols) & (lane < (r + 1) * blk_cols)
        lg = jnp.where(in_blk, lg + penalty, lg)
    qb = q.astype(BF16).astype(F32)
    lo = jnp.sum(qb * kn_ref[...].astype(BF16).astype(F32), axis=-1, keepdims=True) * scale
    m = jnp.maximum(jnp.max(lg, axis=-1, keepdims=True), lo)
    p = jnp.exp(lg - m)
    po = jnp.exp(lo - m)
    l = jnp.sum(p, axis=-1, keepdims=True) + po
    vn = vn_ref[...].astype(BF16).astype(F32)
    out = (_dot(p.astype(BF16), vals) + po.astype(BF16).astype(F32) * vn) / l
    o_ref[...] = out[0:1, :]


def _mb_sample_attn(pool_k, pool_v, page_table, sel, q, k_new, v_new, layer):
    bd, n_pages = page_table.shape
    hd = HEAD_DIM
    per_blk = MB_BLOCK // PAGE_SIZE

    def page_spec(r, half):
        def imap(b, h, pt, sl):
            blk = sl[(b * MB_HEADS + h) * SUBLANES + r]
            return (layer, pt[b * n_pages + blk * per_blk + half], 0, 0, 0)
        return pl.BlockSpec((None, None, PAGE_SIZE, MB_HEADS, hd), imap)

    pages = [page_spec(r, half) for r in range(MB_TOP_K) for half in range(per_blk)]
    vec = pl.BlockSpec((None, None, 1, hd), lambda b, h, pt, sl: (b, h, 0, 0))
    q4 = q.reshape(bd, MB_HEADS, 1, hd)
    out = pl.pallas_call(
        _mb_sample_attn_kernel,
        grid_spec=pltpu.PrefetchScalarGridSpec(
            num_scalar_prefetch=2,
            grid=(bd, MB_HEADS),
            in_specs=[vec, vec, vec] + pages + pages,
            out_specs=vec,
        ),
        out_shape=jax.ShapeDtypeStruct((bd, MB_HEADS, 1, hd), F32),
        compiler_params=_cparams("parallel", "arbitrary"),
        name="mb_sample_attn",
    )(page_table.reshape(-1), sel[:, :, :SUBLANES].reshape(-1), q4, k_new.reshape(q4.shape), v_new.reshape(q4.shape),
      *([pool_k] * len(pages)), *([pool_v] * len(pages)))
    return out.reshape(bd, MB_HEADS * hd)


def _ffn_kernel(y_ref, g_ref, wg_ref, wv_ref, cwg_ref, cwv_ref, cbg_ref, cbv_ref, wo_ref,
                out_ref, bufg_ref, bufv_ref, h_scr, ug_scr, uv_scr, cg_scr, cv_scr, acc_scr,
                *, tiles_per_seq):
    i, f = pl.program_id(0), pl.program_id(1)
    tm = y_ref.shape[0]
    first = (i % tiles_per_seq) == 0

    @pl.when(f == 0)
    def _():
        h_scr[...] = _rms_rows(y_ref[...], g_ref[...]).astype(BF16)
        acc_scr[...] = jnp.zeros(acc_scr.shape, F32)

    @pl.when(first)
    def _():
        cg_scr[f] = jnp.zeros(cg_scr.shape[1:], F32)
        cv_scr[f] = jnp.zeros(cv_scr.shape[1:], F32)

    def branch(w_ref, cw_ref, cb_ref, u_scr, carry, buf_ref):
        u = _dot(h_scr[...], w_ref[...])
        u_scr[0:SUBLANES, :] = carry[f]
        u_scr[SUBLANES:SUBLANES + tm, :] = u
        r = cw_ref[FFN_CONV - 1:FFN_CONV, :] * u + cb_ref[...]
        for d in range(1, FFN_CONV):
            r = r + cw_ref[FFN_CONV - 1 - d:FFN_CONV - d, :] * u_scr[SUBLANES - d:SUBLANES - d + tm, :]
        last = u_scr[tm:tm + SUBLANES, :]
        carry[f] = last
        buf_ref[...] = last
        return r

    gate = branch(wg_ref, cwg_ref, cbg_ref, ug_scr, cg_scr, bufg_ref)
    val = branch(wv_ref, cwv_ref, cbv_ref, uv_scr, cv_scr, bufv_ref)
    act = (_silu(gate) * val).astype(BF16)
    acc_scr[...] += _dot(act, wo_ref[...])

    @pl.when(f == pl.num_programs(1) - 1)
    def _():
        out_ref[...] = y_ref[...] + acc_scr[...]


def _ffn_tile(dff):
    for cand in (1408, 1024, 512, 256, 128):
        if dff % cand == 0:
            return cand
    raise ValueError(f"unsupported d_ff {dff}")


def _ffn_prompt(y, g, w_in, conv_w, conv_b, w_out, bsz, seq):
    m, d = y.shape
    dff = w_out.shape[0]
    tm = min(512, seq)
    tf = _ffn_tile(dff)
    nf = dff // tf
    tiles_per_seq = seq // tm
    gate = lambda i, f: (0, f)
    val = lambda i, f: (0, f + nf)
    buf_spec = pl.BlockSpec((None, SUBLANES, tf), lambda i, f: (i, 0, f))
    out, bufg, bufv = pl.pallas_call(
        functools.partial(_ffn_kernel, tiles_per_seq=tiles_per_seq),
        grid=(m // tm, nf),
        in_specs=[pl.BlockSpec((tm, d), lambda i, f: (i, 0)),
                  pl.BlockSpec((1, d), lambda i, f: (0, 0)),
                  pl.BlockSpec((d, tf), gate), pl.BlockSpec((d, tf), val),
                  pl.BlockSpec((FFN_CONV, tf), gate), pl.BlockSpec((FFN_CONV, tf), val),
                  pl.BlockSpec((1, tf), gate), pl.BlockSpec((1, tf), val),
                  pl.BlockSpec((tf, d), lambda i, f: (f, 0))],
        out_specs=[pl.BlockSpec((tm, d), lambda i, f: (i, 0)), buf_spec, buf_spec],
        out_shape=[jax.ShapeDtypeStruct((m, d), F32),
                   jax.ShapeDtypeStruct((m // tm, SUBLANES, dff), F32),
                   jax.ShapeDtypeStruct((m // tm, SUBLANES, dff), F32)],
        scratch_shapes=[pltpu.VMEM((tm, d), BF16),
                        pltpu.VMEM((tm + SUBLANES, tf), F32), pltpu.VMEM((tm + SUBLANES, tf), F32),
                        pltpu.VMEM((nf, SUBLANES, tf), F32), pltpu.VMEM((nf, SUBLANES, tf), F32),
                        pltpu.VMEM((tm, d), F32)],
        compiler_params=_cparams("arbitrary", "arbitrary"),
        name="ffn_prompt",
    )(y, g, w_in, w_in, conv_w, conv_w, conv_b, conv_b, w_out)
    keep = FFN_CONV - 1
    last = slice(tiles_per_seq - 1, None, tiles_per_seq)
    new_buf = jnp.concatenate([bufg[last, SUBLANES - keep:], bufv[last, SUBLANES - keep:]], axis=-1)
    return out, new_buf


def _ffn_sample_kernel(y_ref, g_ref, wg_ref, wv_ref, cwg_ref, cwv_ref, cbg_ref, cbv_ref,
                       sg_ref, sv_ref, wo_ref, out_ref, ng_ref, nv_ref, h_scr, acc_scr):
    f = pl.program_id(0)

    @pl.when(f == 0)
    def _():
        h_scr[...] = _rms_rows(y_ref[...], g_ref[...]).astype(BF16)
        acc_scr[...] = jnp.zeros(acc_scr.shape, F32)

    def branch(w_ref, cw_ref, cb_ref, st_ref, new_ref):
        u = _dot(h_scr[...], w_ref[...])
        r = cw_ref[FFN_CONV - 1:FFN_CONV, :] * u + cb_ref[...]
        for i in range(FFN_CONV - 1):
            r = r + cw_ref[i:i + 1, :] * st_ref[i]
        for i in range(FFN_CONV - 2):
            new_ref[i] = st_ref[i + 1]
        new_ref[FFN_CONV - 2] = u
        return r

    gate = branch(wg_ref, cwg_ref, cbg_ref, sg_ref, ng_ref)
    val = branch(wv_ref, cwv_ref, cbv_ref, sv_ref, nv_ref)
    acc_scr[...] += _dot((_silu(gate) * val).astype(BF16), wo_ref[...])

    @pl.when(f == pl.num_programs(0) - 1)
    def _():
        out_ref[...] = y_ref[...] + acc_scr[...]


def _ffn_sample(y, g, w_in, conv_w, conv_b, w_out, state):
    bd, d = y.shape
    dff = w_out.shape[0]
    tf = _ffn_tile(dff)
    nf = dff // tf
    keep = FFN_CONV - 1
    st = jnp.transpose(state, (1, 0, 2))
    gate = lambda f: (0, f)
    val = lambda f: (0, f + nf)
    st_gate = pl.BlockSpec((keep, bd, tf), lambda f: (0, 0, f))
    st_val = pl.BlockSpec((keep, bd, tf), lambda f: (0, 0, f + nf))
    new_spec = pl.BlockSpec((keep, bd, tf), lambda f: (0, 0, f))
    out, ng, nv = pl.pallas_call(
        _ffn_sample_kernel,
        grid=(nf,),
        in_specs=[pl.BlockSpec((bd, d), lambda f: (0, 0)),
                  pl.BlockSpec((1, d), lambda f: (0, 0)),
                  pl.BlockSpec((d, tf), gate), pl.BlockSpec((d, tf), val),
                  pl.BlockSpec((FFN_CONV, tf), gate), pl.BlockSpec((FFN_CONV, tf), val),
                  pl.BlockSpec((1, tf), gate), pl.BlockSpec((1, tf), val),
                  st_gate, st_val,
                  pl.BlockSpec((tf, d), lambda f: (f, 0))],
        out_specs=[pl.BlockSpec((bd, d), lambda f: (0, 0)), new_spec, new_spec],
        out_shape=[jax.ShapeDtypeStruct((bd, d), F32),
                   jax.ShapeDtypeStruct((keep, bd, dff), F32),
                   jax.ShapeDtypeStruct((keep, bd, dff), F32)],
        scratch_shapes=[pltpu.VMEM((bd, d), BF16), pltpu.VMEM((bd, d), F32)],
        compiler_params=_cparams("arbitrary"),
        name="ffn_sample",
    )(y, g, w_in, w_in, conv_w, conv_w, conv_b, conv_b, st, st, w_out)
    new_state = jnp.transpose(jnp.concatenate([ng, nv], axis=-1), (1, 0, 2))
    return out, new_state


def _final_norm_kernel(x_ref, g_ref, o_ref):
    o_ref[...] = _rms_rows(x_ref[...], g_ref[...])


def _final_norm(x, g):
    m, d = x.shape
    tm = min(1024, m)
    return pl.pallas_call(
        _final_norm_kernel,
        grid=(m // tm,),
        in_specs=[pl.BlockSpec((tm, d), lambda i: (i, 0)), pl.BlockSpec((1, d), lambda i: (0, 0))],
        out_specs=pl.BlockSpec((tm, d), lambda i: (i, 0)),
        out_shape=jax.ShapeDtypeStruct((m, d), F32),
        compiler_params=_cparams("parallel"),
        name="final_norm",
    )(x, g)


def _rope_tables(pos):
    half = HEAD_DIM // 2
    inv = ROPE_THETA ** (-jnp.arange(half, dtype=F32) / half)
    ang = pos.astype(F32)[:, None] * inv[None, :]
    cos, sin = jnp.cos(ang), jnp.sin(ang)
    return jnp.concatenate([cos, cos], axis=-1), jnp.concatenate([-sin, sin], axis=-1)


def _dn_gate_layout(w_in, a_log, dt_bias):
    d = w_in.shape[0]
    rep = DN_V_HEADS // DN_QK_HEADS
    w_beta = w_in[:, DN_MAIN_DIM:DN_MAIN_DIM + DN_V_HEADS].reshape(d, DN_QK_HEADS, rep)
    w_dec = w_in[:, DN_MAIN_DIM + DN_V_HEADS:].reshape(d, DN_QK_HEADS, rep)
    pad = jnp.zeros((d, DN_QK_HEADS, SUBLANES - 2 * rep), w_in.dtype)
    w_bg = jnp.concatenate([w_beta, w_dec, pad], axis=-1).reshape(d, DN_QK_HEADS * SUBLANES)
    w_bg = jnp.pad(w_bg, ((0, 0), (0, LANES - w_bg.shape[1])))

    def lanes(per_head, fill):
        v = per_head.reshape(DN_QK_HEADS, rep)
        z = jnp.full((DN_QK_HEADS, rep), fill, F32)
        p = jnp.full((DN_QK_HEADS, SUBLANES - 2 * rep), fill, F32)
        return v, z, p

    a, zero, padz = lanes(-jnp.exp(a_log.astype(F32)), 0.0)
    neg_a = jnp.concatenate([zero, a, padz], axis=-1).reshape(-1)
    dt, _, _ = lanes(dt_bias.astype(F32), 0.0)
    dt_b = jnp.concatenate([zero, dt, padz], axis=-1).reshape(-1)
    is_beta = jnp.concatenate([zero + 1.0, zero, padz], axis=-1).reshape(-1)
    par = jnp.stack([neg_a, dt_b, is_beta])
    par = jnp.pad(par, ((0, SUBLANES - par.shape[0]), (0, LANES - par.shape[1])))
    return w_bg.astype(BF16), par


def kernel(x_prompt, x_sample, cache_k, cache_v, state_dn, state_dn_conv, state_ffn_conv, page_table,
           norm_mix, norm_ffn, norm_out, dn_w_in, dn_conv_w, dn_a_log, dn_dt_bias, dn_norm, dn_w_out,
           mb_w_qkv, mb_w_o, ffn_w_in, ffn_conv_w, ffn_conv_b, ffn_w_out):
    bsz, seq, d = x_prompt.shape
    bd = x_sample.shape[0]
    depth = norm_mix.shape[0]
    n_pages = page_table.shape[1]
    past = n_pages * PAGE_SIZE
    own_blk = past // MB_BLOCK
    hd = HEAD_DIM

    yp = x_prompt.reshape(bsz * seq, d)
    ys = x_sample.reshape(bd, d)
    cos_p, sin_p = _rope_tables(jnp.arange(seq, dtype=I32))
    cos_s, sin_s = _rope_tables(jnp.full((bd,), past, dtype=I32))
    pool_k, pool_v = cache_k, cache_v

    kp_l, vp_l, ks_l, vs_l = [], [], [], []
    sp_l, ss_l, cp_l, cs_l = [], [], [], []
    fp_l, fs_l = [], []
    for layer in range(depth):
        j = layer // 2
        g_mix = norm_mix[layer].reshape(1, d)
        if layer % 2 == 0:
            w_main = dn_w_in[j][:, :DN_MAIN_DIM].astype(BF16)
            w_bg, par = _dn_gate_layout(dn_w_in[j], dn_a_log[j], dn_dt_bias[j])
            w_out = dn_w_out[j].astype(BF16)
            nw = dn_norm[j].reshape(1, hd)
            proj, bg, bgt = _dn_inproj(yp, g_mix, w_main, w_bg, par, transposed=True)
            o, s_new = _dn_core(proj, bg, bgt, dn_conv_w[j], nw, bsz, seq)
            yp = _matmul_residual(o, w_out, yp)
            sp_l.append(s_new)
            cp_l.append(proj.reshape(bsz, seq, DN_MAIN_DIM)[:, seq - (DN_CONV - 1):, :DN_CONV_DIM])

            proj_s, bg_s = _dn_inproj(ys, g_mix, w_main, w_bg, par, transposed=False)
            o_s, s_s, c_s = _dn_sample(proj_s, bg_s, state_dn_conv[j], dn_conv_w[j], state_dn[j], nw)
            ys = _matmul_residual(o_s, w_out, ys)
            ss_l.append(s_s)
            cs_l.append(c_s)
        else:
            w_qkv = mb_w_qkv[j].astype(BF16)
            w_o = mb_w_o[j].astype(BF16)
            q, k, v, kb, vt, km = _mb_qkv(yp, g_mix, w_qkv, cos_p, sin_p, with_blocks=True)
            att = _mb_attn(q, kb, vt, km, bsz, seq)
            yp = _matmul_residual(att, w_o, yp)
            kp_l.append(k.reshape(bsz, seq, MB_HEADS, hd))
            vp_l.append(v.reshape(bsz, seq, MB_HEADS, hd))

            q_s, k_s, v_s = _mb_qkv(ys, g_mix, w_qkv, cos_s, sin_s, with_blocks=False)
            blk_sums = _page_block_sums(pool_k, page_table, j)
            q3 = q_s.reshape(bd, MB_HEADS, hd)
            k3 = k_s.reshape(bd, MB_HEADS, hd)
            sel = _mb_sample_select(q3, k3, blk_sums, own_blk)
            att_s = _mb_sample_attn(pool_k, pool_v, page_table, sel, q3, k3, v_s.reshape(bd, MB_HEADS, hd), j)
            ys = _matmul_residual(att_s.astype(BF16), w_o, ys)
            ks_l.append(k_s.reshape(bd, 1, MB_HEADS, hd))
            vs_l.append(v_s.reshape(bd, 1, MB_HEADS, hd))

        g_ffn = norm_ffn[layer].reshape(1, d)
        w_fi = ffn_w_in[layer].astype(BF16)
        w_fo = ffn_w_out[layer].astype(BF16)
        cb = ffn_conv_b[layer].reshape(1, -1)
        yp, fbp = _ffn_prompt(yp, g_ffn, w_fi, ffn_conv_w[layer], cb, w_fo, bsz, seq)
        ys, fbs = _ffn_sample(ys, g_ffn, w_fi, ffn_conv_w[layer], cb, w_fo, state_ffn_conv[layer])
        fp_l.append(fbp)
        fs_l.append(fbs)

    g_out = norm_out.reshape(1, d)
    y_prompt = _final_norm(yp, g_out).reshape(bsz, seq, d)
    y_sample = _final_norm(ys, g_out).reshape(bd, 1, d)
    return (y_prompt, y_sample, jnp.stack(kp_l), jnp.stack(vp_l), jnp.stack(ks_l), jnp.stack(vs_l),
            jnp.stack(sp_l), jnp.stack(ss_l), jnp.stack(cp_l), jnp.stack(cs_l), jnp.stack(fp_l), jnp.stack(fs_l))
```

```python
import functools

import jax
import jax.numpy as jnp
from jax import lax
from jax.experimental import pallas as pl
from jax.experimental.pallas import tpu as pltpu

F32 = jnp.float32
BF16 = jnp.bfloat16
I32 = jnp.int32

NORM_EPS = 1e-6
ROPE_THETA = 10000.0
HEAD_DIM = 128
DN_QK_HEADS = 8
DN_V_HEADS = 16
DN_CONV = 4
DN_KEY_DIM = DN_QK_HEADS * HEAD_DIM
DN_VAL_DIM = DN_V_HEADS * HEAD_DIM
DN_CONV_DIM = 2 * DN_KEY_DIM + DN_VAL_DIM
DN_MAIN_DIM = DN_CONV_DIM + DN_VAL_DIM
DN_CHUNK = 128
DN_TBLOCK = 512
MB_HEADS = 8
MB_BLOCK = 256
MB_TOP_K = 3
PAGE_SIZE = 128
FFN_CONV = 3
SUBLANES = 8
LANES = 128
NEG_BIG = -1e30
VMEM_LIMIT = 56 * 1024 * 1024


def _cparams(*sem):
    return pltpu.CompilerParams(dimension_semantics=sem, vmem_limit_bytes=VMEM_LIMIT)


def _sigmoid(x):
    return 1.0 / (1.0 + jnp.exp(-x))


def _silu(x):
    return x * _sigmoid(x)


def _softplus(x):
    return jnp.maximum(x, 0.0) + jnp.log1p(jnp.exp(-jnp.abs(x)))


def _rms_rows(x, g):
    ms = jnp.mean(x * x, axis=-1, keepdims=True)
    return x * lax.rsqrt(ms + NORM_EPS) * g


def _dot(a, b):
    return jnp.dot(a, b, preferred_element_type=F32)


def _dot_nt(a, b):
    return lax.dot_general(a, b, (((1,), (1,)), ((), ())), preferred_element_type=F32)


def _dot_tn(a, b):
    return lax.dot_general(a, b, (((0,), (0,)), ((), ())), preferred_element_type=F32)


def _split2(a):
    hi = a.astype(BF16)
    lo = (a - hi.astype(F32)).astype(BF16)
    return hi, lo


def _mm3(a, b):
    (ah, al), (bh, bl) = a, b
    n = bh.shape[1]
    lhs = jnp.concatenate([ah, al], axis=1)
    rhs = jnp.concatenate([jnp.concatenate([bh, bl], axis=1),
                           jnp.concatenate([bh, jnp.zeros_like(bl)], axis=1)], axis=0)
    out = _dot(lhs, rhs)
    return out[:, :n] + out[:, n:]


def _dot_exact01(m01, x):
    hi = x.astype(BF16)
    r1 = x - hi.astype(F32)
    mid = r1.astype(BF16)
    lo = (r1 - mid.astype(F32)).astype(BF16)
    return _dot(m01, hi) + (_dot(m01, mid) + _dot(m01, lo))


def _dot_exact01_r(x, m01):
    hi = x.astype(BF16)
    r1 = x - hi.astype(F32)
    mid = r1.astype(BF16)
    lo = (r1 - mid.astype(F32)).astype(BF16)
    return _dot(hi, m01) + (_dot(mid, m01) + _dot(lo, m01))


def _dn_inproj_kernel(x_ref, g_ref, w_ref, wbg_ref, par_ref, o_ref, bg_ref, *rest, transposed):
    if transposed:
        bgt_ref, h_scr = rest
    else:
        (h_scr,) = rest

    @pl.when(pl.program_id(1) == 0)
    def _():
        h = _rms_rows(x_ref[...], g_ref[...]).astype(BF16)
        h_scr[...] = h
        raw = _dot(h, wbg_ref[...])
        neg_a, dt_b, is_beta = par_ref[0:1, :], par_ref[1:2, :], par_ref[2:3, :]
        act = jnp.where(is_beta > 0.5, _sigmoid(raw), neg_a * _softplus(raw + dt_b))
        bg_ref[...] = act
        if transposed:
            bgt_ref[...] = act.T

    o_ref[...] = _dot(h_scr[...], w_ref[...])


def _dn_inproj(x, g, w_main, w_bg, par, *, transposed):
    m, d = x.shape
    n = w_main.shape[1]
    tm = min(512, m)
    tn = 2048
    out_shape = [jax.ShapeDtypeStruct((m, n), F32), jax.ShapeDtypeStruct((m, LANES), F32)]
    out_specs = [pl.BlockSpec((tm, tn), lambda i, j: (i, j)),
                 pl.BlockSpec((tm, LANES), lambda i, j: (i, 0))]
    if transposed:
        out_shape.append(jax.ShapeDtypeStruct((LANES, m), F32))
        out_specs.append(pl.BlockSpec((LANES, tm), lambda i, j: (0, i)))
    return pl.pallas_call(
        functools.partial(_dn_inproj_kernel, transposed=transposed),
        grid=(m // tm, n // tn),
        in_specs=[pl.BlockSpec((tm, d), lambda i, j: (i, 0)),
                  pl.BlockSpec((1, d), lambda i, j: (0, 0)),
                  pl.BlockSpec((d, tn), lambda i, j: (0, j)),
                  pl.BlockSpec((d, LANES), lambda i, j: (0, 0)),
                  pl.BlockSpec((SUBLANES, LANES), lambda i, j: (0, 0))],
        out_specs=out_specs,
        out_shape=out_shape,
        scratch_shapes=[pltpu.VMEM((tm, d), BF16)],
        compiler_params=_cparams("parallel", "arbitrary"),
        name="dn_inproj",
    )(x, g, w_main, w_bg, par)


def _unit_lower_inverses(lows, c):
    row = lax.broadcasted_iota(I32, (c, c), 0)
    col = lax.broadcasted_iota(I32, (c, c), 1)

    def same_block(shift):
        return (row >> shift) == (col >> shift)

    low_parts = [_split2(low) for low in lows]
    base = same_block(3)
    negs = [jnp.where(base, -low, 0.0) for low in lows]
    eye = jnp.where(row == col, 1.0, 0.0)
    invs = [eye + neg for neg in negs]
    powers = [_split2(neg) for neg in negs]
    for _ in range(2):
        powers = [_split2(_mm3(p, p)) for p in powers]
        invs = [inv + _mm3(_split2(inv), p) for inv, p in zip(invs, powers)]
    shift = 3
    while (1 << shift) < c:
        mask = same_block(shift + 1) & jnp.logical_not(same_block(shift))
        inv_parts = [_split2(inv) for inv in invs]
        tmps = [_split2(jnp.where(mask, _mm3(lp, ip), 0.0)) for lp, ip in zip(low_parts, inv_parts)]
        invs = [inv - _mm3(ip, t) for inv, ip, t in zip(invs, inv_parts, tmps)]
        shift += 1
    return invs


def _dn_core_kernel(q_ref, k_ref, v_ref, z_ref, bg_ref, bgt_ref, cwq_ref, cwk_ref, cwv_ref, nw_ref,
                    o_ref, s_out_ref, xe_scr, s_scr, *, tb, chunk):
    hq = pl.program_id(1)
    t = pl.program_id(2)
    hd = HEAD_DIM

    @pl.when(t == 0)
    def _():
        s_scr[...] = jnp.zeros(s_scr.shape, F32)
        xe_scr[0:SUBLANES, :] = jnp.zeros((SUBLANES, 4 * hd), F32)

    xe_scr[SUBLANES:SUBLANES + tb, 0:hd] = q_ref[...]
    xe_scr[SUBLANES:SUBLANES + tb, hd:2 * hd] = k_ref[...]
    xe_scr[SUBLANES:SUBLANES + tb, 2 * hd:4 * hd] = v_ref[...]

    def conv(lo, hi, cw_ref):
        acc = cw_ref[DN_CONV - 1:DN_CONV, :] * xe_scr[SUBLANES:SUBLANES + tb, lo:hi]
        for d in range(1, DN_CONV):
            acc = acc + (cw_ref[DN_CONV - 1 - d:DN_CONV - d, :]
                         * xe_scr[SUBLANES - d:SUBLANES - d + tb, lo:hi])
        return _silu(acc)

    qc = conv(0, hd, cwq_ref)
    kc = conv(hd, 2 * hd, cwk_ref)
    vc = conv(2 * hd, 4 * hd, cwv_ref)
    xe_scr[0:SUBLANES, :] = xe_scr[tb:tb + SUBLANES, :]

    qn = qc * lax.rsqrt(jnp.sum(qc * qc, axis=-1, keepdims=True) + NORM_EPS) * (hd ** -0.5)
    kn = kc * lax.rsqrt(jnp.sum(kc * kc, axis=-1, keepdims=True) + NORM_EPS)

    bgr = pltpu.roll(bg_ref[...], (LANES - SUBLANES * hq) % LANES, axis=1)
    bgt = bgt_ref[...]
    ri = lax.broadcasted_iota(I32, (tb, tb), 0)
    ci = lax.broadcasted_iota(I32, (tb, tb), 1)
    cshift = chunk.bit_length() - 1
    same_chunk = (ri >> cshift) == (ci >> cshift)
    upper = jnp.where(same_chunk & (ri <= ci), 1.0, 0.0).astype(BF16)
    lower = jnp.where(same_chunk & (ri >= ci), 1.0, 0.0).astype(BF16)
    gc_rows = _dot_exact01_r(bgt, upper)
    gc_cols = _dot_exact01(lower, bgr)

    rc = lax.broadcasted_iota(I32, (chunk, chunk), 0)
    cc = lax.broadcasted_iota(I32, (chunk, chunk), 1)
    nw = nw_ref[...]

    items = []
    for c in range(tb // chunk):
        r0, r1 = c * chunk, (c + 1) * chunk
        q_c, k_c = qn[r0:r1], kn[r0:r1]
        kb = k_c.astype(BF16)
        kk = _dot_nt(kb, kb)
        qk = _dot_nt(q_c.astype(BF16), kb)
        for x in range(2):
            g_col = gc_cols[r0:r1, 2 + x:3 + x]
            g_row = gc_rows[2 + x:3 + x, r0:r1]
            b_col = bgr[r0:r1, x:x + 1]
            decay = jnp.exp(jnp.where(rc >= cc, g_col - g_row, -jnp.inf))
            eg = jnp.exp(g_col)
            g_last = g_col[chunk - 1:chunk, :]
            v_c = vc[r0:r1, x * hd:(x + 1) * hd]
            items.append(dict(
                c=c, x=x,
                low=jnp.where(rc > cc, b_col * decay * kk, 0.0),
                rhs=jnp.concatenate([b_col * v_c, (b_col * eg) * k_c], axis=1).astype(BF16),
                qg=(q_c * eg).astype(BF16),
                qkd=(qk * decay).astype(BF16),
                kd=(k_c * jnp.exp(g_last - g_col)).astype(BF16),
                gl=jnp.exp(g_last)))
    invs = _unit_lower_inverses([it["low"] for it in items], chunk)
    sols = [_dot(inv.astype(BF16), it["rhs"]) for inv, it in zip(invs, items)]

    states = [s_scr[x] for x in range(2)]
    for c in range(tb // chunk):
        r0, r1 = c * chunk, (c + 1) * chunk
        pair = [(it, sol) for it, sol in zip(items, sols) if it["c"] == c]
        sbs = [states[it["x"]].astype(BF16) for it, _ in pair]
        us = [sol[:, :hd] - _dot(sol[:, hd:].astype(BF16), sb) for (_, sol), sb in zip(pair, sbs)]
        ubs = [u.astype(BF16) for u in us]
        outs = [_dot(it["qg"], sb) + _dot(it["qkd"], ub) for (it, _), sb, ub in zip(pair, sbs, ubs)]
        for (it, _), ub, o in zip(pair, ubs, outs):
            x = it["x"]
            states[x] = it["gl"] * states[x] + _dot_tn(it["kd"], ub)
            z_c = z_ref[r0:r1, x * hd:(x + 1) * hd]
            on = o * lax.rsqrt(jnp.mean(o * o, axis=-1, keepdims=True) + NORM_EPS) * nw * _silu(z_c)
            o_ref[r0:r1, x * hd:(x + 1) * hd] = on.astype(o_ref.dtype)
    for x in range(2):
        s_scr[x] = states[x]

    @pl.when(t == pl.num_programs(2) - 1)
    def _():
        s_out_ref[...] = s_scr[...]


def _dn_core(proj, bg, bgt, conv_w, norm_w, bsz, seq):
    tb, hd = DN_TBLOCK, HEAD_DIM
    nt = seq // tb
    kernel = functools.partial(_dn_core_kernel, tb=tb, chunk=DN_CHUNK)
    tok = lambda b, h, t: b * nt + t
    return pl.pallas_call(
        kernel,
        grid=(bsz, DN_QK_HEADS, nt),
        in_specs=[
            pl.BlockSpec((tb, hd), lambda b, h, t: (tok(b, h, t), h)),
            pl.BlockSpec((tb, hd), lambda b, h, t: (tok(b, h, t), DN_QK_HEADS + h)),
            pl.BlockSpec((tb, 2 * hd), lambda b, h, t: (tok(b, h, t), DN_QK_HEADS + h)),
            pl.BlockSpec((tb, 2 * hd), lambda b, h, t: (tok(b, h, t), 2 * DN_QK_HEADS + h)),
            pl.BlockSpec((tb, LANES), lambda b, h, t: (tok(b, h, t), 0)),
            pl.BlockSpec((SUBLANES, tb), lambda b, h, t: (h, tok(b, h, t))),
            pl.BlockSpec((DN_CONV, hd), lambda b, h, t: (0, h)),
            pl.BlockSpec((DN_CONV, hd), lambda b, h, t: (0, DN_QK_HEADS + h)),
            pl.BlockSpec((DN_CONV, 2 * hd), lambda b, h, t: (0, DN_QK_HEADS + h)),
            pl.BlockSpec((1, hd), lambda b, h, t: (0, 0)),
        ],
        out_specs=[
            pl.BlockSpec((tb, 2 * hd), lambda b, h, t: (tok(b, h, t), h)),
            pl.BlockSpec((None, 2, hd, hd), lambda b, h, t: (b, h, 0, 0)),
        ],
        out_shape=[jax.ShapeDtypeStruct((bsz * seq, DN_VAL_DIM), BF16),
                   jax.ShapeDtypeStruct((bsz, DN_V_HEADS, hd, hd), F32)],
        scratch_shapes=[pltpu.VMEM((tb + SUBLANES, 4 * hd), F32),
                        pltpu.VMEM((2, hd, hd), F32)],
        compiler_params=_cparams("parallel", "parallel", "arbitrary"),
        name="dn_core",
    )(proj, proj, proj, proj, bg, bgt, conv_w, conv_w, conv_w, norm_w)


def _dn_sample_kernel(p_ref, bg_ref, cst_ref, cw_ref, s_ref, nw_ref,
                      o_ref, s_out_ref, cst_out_ref, r_scr):
    hd = HEAD_DIM
    x = p_ref[:, 0:DN_CONV_DIM]
    buf = cst_ref[...]
    y = x * cw_ref[DN_CONV - 1:DN_CONV, :]
    for i in range(DN_CONV - 1):
        y = y + buf[i:i + 1, :] * cw_ref[i:i + 1, :]
    cst_out_ref[0:DN_CONV - 2, :] = buf[1:DN_CONV - 1, :]
    cst_out_ref[DN_CONV - 2:DN_CONV - 1, :] = x
    a = _silu(y)

    r_scr[...] = jnp.zeros(r_scr.shape, F32)
    qs, ks = [], []
    for h in range(DN_QK_HEADS):
        qh = a[:, h * hd:(h + 1) * hd]
        kh = a[:, DN_KEY_DIM + h * hd:DN_KEY_DIM + (h + 1) * hd]
        qh = qh * lax.rsqrt(jnp.sum(qh * qh, axis=-1, keepdims=True) + NORM_EPS) * (hd ** -0.5)
        kh = kh * lax.rsqrt(jnp.sum(kh * kh, axis=-1, keepdims=True) + NORM_EPS)
        qs.append(qh)
        ks.append(kh)
        r_scr[h:h + 1, :] = qh
        r_scr[DN_QK_HEADS + h:DN_QK_HEADS + h + 1, :] = kh
    rt = r_scr[...].T

    bg = bg_ref[...]
    nw = nw_ref[...]
    rep = DN_V_HEADS // DN_QK_HEADS
    for h in range(DN_V_HEADS):
        hq, x_in = h // rep, h % rep
        beta = bg[:, SUBLANES * hq + x_in:SUBLANES * hq + x_in + 1]
        g = bg[:, SUBLANES * hq + 2 + x_in:SUBLANES * hq + 3 + x_in]
        eg = jnp.exp(g)
        q_col = rt[:, hq:hq + 1]
        k_col = rt[:, DN_QK_HEADS + hq:DN_QK_HEADS + hq + 1]
        s_old = s_ref[h]
        k_s = jnp.sum(k_col * s_old, axis=0, keepdims=True)
        q_s = jnp.sum(q_col * s_old, axis=0, keepdims=True)
        v_h = a[:, 2 * DN_KEY_DIM + h * hd:2 * DN_KEY_DIM + (h + 1) * hd]
        u = beta * v_h - (beta * eg) * k_s
        qk = jnp.sum(qs[hq] * ks[hq], axis=-1, keepdims=True)
        o = eg * q_s + qk * u
        s_out_ref[h] = eg * s_old + k_col * u
        z_h = p_ref[:, DN_CONV_DIM + h * hd:DN_CONV_DIM + (h + 1) * hd]
        on = o * lax.rsqrt(jnp.mean(o * o, axis=-1, keepdims=True) + NORM_EPS) * nw * _silu(z_h)
        o_ref[:, h * hd:(h + 1) * hd] = on.astype(o_ref.dtype)


def _dn_sample(proj, bg, conv_state, conv_w, state, norm_w):
    bd = proj.shape[0]
    hd = HEAD_DIM
    o, s_new, c_new = pl.pallas_call(
        _dn_sample_kernel,
        grid=(bd,),
        in_specs=[
            pl.BlockSpec((None, 1, DN_MAIN_DIM), lambda b: (b, 0, 0)),
            pl.BlockSpec((None, 1, LANES), lambda b: (b, 0, 0)),
            pl.BlockSpec((None, DN_CONV - 1, DN_CONV_DIM), lambda b: (b, 0, 0)),
            pl.BlockSpec((DN_CONV, DN_CONV_DIM), lambda b: (0, 0)),
            pl.BlockSpec((None, DN_V_HEADS, hd, hd), lambda b: (b, 0, 0, 0)),
            pl.BlockSpec((1, hd), lambda b: (0, 0)),
        ],
        out_specs=[
            pl.BlockSpec((None, 1, DN_VAL_DIM), lambda b: (b, 0, 0)),
            pl.BlockSpec((None, DN_V_HEADS, hd, hd), lambda b: (b, 0, 0, 0)),
            pl.BlockSpec((None, DN_CONV - 1, DN_CONV_DIM), lambda b: (b, 0, 0)),
        ],
        out_shape=[jax.ShapeDtypeStruct((bd, 1, DN_VAL_DIM), BF16),
                   jax.ShapeDtypeStruct(state.shape, F32),
                   jax.ShapeDtypeStruct(conv_state.shape, F32)],
        scratch_shapes=[pltpu.VMEM((LANES, LANES), F32)],
        compiler_params=_cparams("parallel"),
        name="dn_sample",
    )(proj.reshape(bd, 1, DN_MAIN_DIM), bg.reshape(bd, 1, LANES), conv_state, conv_w, state, norm_w)
    return o.reshape(bd, DN_VAL_DIM), s_new, c_new


def _matmul_residual_kernel(a_ref, w_ref, y_ref, o_ref):
    o_ref[...] = y_ref[...] + _dot(a_ref[...], w_ref[...])


def _matmul_residual(a, w, y):
    m, k = a.shape
    n = w.shape[1]
    tm = min(512, m)
    return pl.pallas_call(
        _matmul_residual_kernel,
        grid=(m // tm,),
        in_specs=[pl.BlockSpec((tm, k), lambda i: (i, 0)),
                  pl.BlockSpec((k, n), lambda i: (0, 0)),
                  pl.BlockSpec((tm, n), lambda i: (i, 0))],
        out_specs=pl.BlockSpec((tm, n), lambda i: (i, 0)),
        out_shape=jax.ShapeDtypeStruct((m, n), F32),
        compiler_params=_cparams("parallel"),
        name="matmul_residual",
    )(a, w, y)


def _mb_qkv_kernel(x_ref, g_ref, w_ref, cos_ref, sin_ref, q_ref, k_ref, v_ref, *rest,
                   with_blocks, blocks_per_seq):
    if with_blocks:
        kb_ref, vt_ref, km_ref, h_scr = rest
    else:
        (h_scr,) = rest
    i, j = pl.program_id(0), pl.program_id(1)
    hd = HEAD_DIM

    @pl.when(j == 0)
    def _():
        h_scr[...] = _rms_rows(x_ref[...], g_ref[...]).astype(BF16)

    y = _dot(h_scr[...], w_ref[...])
    tm = y.shape[0]

    def rope(val):
        cos, sin = cos_ref[...], sin_ref[...]
        parts = []
        for h in range(MB_HEADS):
            yh = val[:, h * hd:(h + 1) * hd]
            parts.append(yh * cos + pltpu.roll(yh, hd // 2, axis=1) * sin)
        return jnp.concatenate(parts, axis=1)

    @pl.when(j == 0)
    def _():
        q_ref[...] = rope(y).astype(q_ref.dtype)

    @pl.when(j == 1)
    def _():
        k = rope(y)
        k_ref[...] = k
        if with_blocks:
            row = lax.broadcasted_iota(I32, (tm, LANES), 0)
            lane = lax.broadcasted_iota(I32, (tm, LANES), 1)
            first_blk = (i * (tm // MB_BLOCK)) % blocks_per_seq
            onehot = jnp.where(lane == first_blk + row // MB_BLOCK, 1.0, 0.0).astype(BF16)
            for h in range(MB_HEADS):
                kb_ref[:, 2 * h * hd:(2 * h + 1) * hd] = k[:, h * hd:(h + 1) * hd].astype(BF16)
                kb_ref[:, (2 * h + 1) * hd:(2 * h + 2) * hd] = onehot
            for r in range(tm // MB_BLOCK):
                blk = k[r * MB_BLOCK:(r + 1) * MB_BLOCK, :]
                km_ref[r] = jnp.sum(blk, axis=0, keepdims=True) * (1.0 / MB_BLOCK)

    @pl.when(j == 2)
    def _():
        v_ref[...] = y
        if with_blocks:
            ones = jnp.ones((SUBLANES, MB_BLOCK), BF16)
            rows = hd + SUBLANES
            for r in range(tm // MB_BLOCK):
                vt = y[r * MB_BLOCK:(r + 1) * MB_BLOCK, :].T.astype(BF16)
                for h in range(MB_HEADS):
                    vt_ref[r, h * rows:h * rows + hd, :] = vt[h * hd:(h + 1) * hd, :]
                    vt_ref[r, h * rows + hd:(h + 1) * rows, :] = ones


def _mb_qkv(x, g, w, cos, sin, *, with_blocks):
    m, d = x.shape
    tm = min(512, m)
    n_pos_tiles = cos.shape[0] // tm
    out_shape = [jax.ShapeDtypeStruct((m, d), BF16 if with_blocks else F32),
                 jax.ShapeDtypeStruct((m, d), F32), jax.ShapeDtypeStruct((m, d), F32)]
    row = pl.BlockSpec((tm, d), lambda i, j: (i, 0))
    out_specs = [row, row, row]
    if with_blocks:
        nb = tm // MB_BLOCK
        vt_rows = MB_HEADS * (HEAD_DIM + SUBLANES)
        out_shape += [jax.ShapeDtypeStruct((m, 2 * d), BF16),
                      jax.ShapeDtypeStruct((m // MB_BLOCK, vt_rows, MB_BLOCK), BF16),
                      jax.ShapeDtypeStruct((m // MB_BLOCK, 1, d), F32)]
        out_specs += [pl.BlockSpec((tm, 2 * d), lambda i, j: (i, 0)),
                      pl.BlockSpec((nb, vt_rows, MB_BLOCK), lambda i, j: (i, 0, 0)),
                      pl.BlockSpec((nb, 1, d), lambda i, j: (i, 0, 0))]
    return pl.pallas_call(
        functools.partial(_mb_qkv_kernel, with_blocks=with_blocks,
                          blocks_per_seq=cos.shape[0] // MB_BLOCK if with_blocks else 0),
        grid=(m // tm, 3),
        in_specs=[row,
                  pl.BlockSpec((1, d), lambda i, j: (0, 0)),
                  pl.BlockSpec((d, d), lambda i, j: (0, j)),
                  pl.BlockSpec((tm, HEAD_DIM), lambda i, j: (i % n_pos_tiles, 0)),
                  pl.BlockSpec((tm, HEAD_DIM), lambda i, j: (i % n_pos_tiles, 0))],
        out_specs=out_specs,
        out_shape=out_shape,
        scratch_shapes=[pltpu.VMEM((tm, d), BF16)],
        compiler_params=_cparams("parallel", "arbitrary"),
        name="mb_qkv",
    )(x, g, w, cos, sin)


def _mb_attn_kernel(q_ref, k_ref, vt_ref, km_ref, o_ref, *logit_bufs):
    n = pl.program_id(2)
    bq, hd = MB_BLOCK, HEAD_DIM
    c = (hd ** -0.5) * 1.4426950408889634
    q = q_ref[...]
    nb = km_ref.shape[0]

    blk = lax.broadcasted_iota(I32, (nb, bq), 0)
    s = jnp.where(blk < n, _dot_nt(km_ref[...].astype(BF16), q), -jnp.inf)
    bias = jnp.full((nb, bq), NEG_BIG, F32)
    for _ in range(MB_TOP_K):
        top = jnp.max(s, axis=0, keepdims=True)
        idx = jnp.min(jnp.where(s == top, blk, nb), axis=0, keepdims=True)
        hit = blk == idx
        bias = jnp.where(hit & (idx < n), 0.0, bias)
        s = jnp.where(hit, -jnp.inf, s)
    bias = jnp.concatenate([bias, jnp.full((LANES - nb, bq), NEG_BIG, F32)], axis=0)
    qa = jnp.concatenate([q, bias.T.astype(BF16)], axis=1)

    kpos = lax.broadcasted_iota(I32, (bq, bq), 0)
    qpos = lax.broadcasted_iota(I32, (bq, bq), 1)
    k_own = k_ref[pl.ds(pl.multiple_of(n * bq, bq), bq), 0:hd]
    st = jnp.where(kpos <= qpos, _dot_nt(k_own, q), NEG_BIG)
    m = jnp.max(st, axis=0, keepdims=True)
    p = jnp.exp2((st - m) * c)
    acc = _dot(vt_ref[n], p.astype(BF16))

    set_x, set_y = logit_bufs[:4], logit_bufs[4:]

    def fill(bufs, j0):
        for i, buf in enumerate(bufs):
            j = jnp.minimum(j0 + i, nb - 1)
            buf[...] = _dot_nt(k_ref[pl.ds(pl.multiple_of(j * bq, bq), bq), :], qa)

    def consume(bufs, j0, m, acc):
        m_new = m
        for buf in bufs:
            m_new = jnp.maximum(m_new, jnp.max(buf[...], axis=0, keepdims=True))
        p = jnp.concatenate([jnp.exp2((buf[...] - m_new) * c).astype(BF16) for buf in bufs], axis=0)
        vt = jnp.concatenate([vt_ref[j0 + i] for i in range(len(bufs))], axis=1)
        return m_new, acc * jnp.exp2((m - m_new) * c) + _dot(vt, p)

    def body(t, carry):
        m, acc = carry
        j = 8 * t
        fill(set_y, j + 4)
        m, acc = consume(set_x, j, m, acc)
        fill(set_x, j + 8)
        return consume(set_y, j + 4, m, acc)

    half_trips = (n + 3) // 4
    fill(set_x, 0)
    m, acc = lax.fori_loop(0, half_trips // 2, body, (m, acc))
    m, acc = lax.cond(half_trips % 2 == 1,
                      lambda: consume(set_x, 8 * (half_trips // 2), m, acc),
                      lambda: (m, acc))
    o_ref[...] = (acc[0:hd, :] / acc[hd:hd + 1, :]).T.astype(o_ref.dtype)


def _mb_attn(q, kb, vt, km, bsz, seq):
    hd, bq = HEAD_DIM, MB_BLOCK
    nb = seq // bq
    d = MB_HEADS * hd
    assert nb % 4 == 0
    return pl.pallas_call(
        _mb_attn_kernel,
        grid=(bsz, MB_HEADS, nb),
        in_specs=[
            pl.BlockSpec((bq, hd), lambda b, h, n: (b * nb + n, h)),
            pl.BlockSpec((seq, 2 * hd), lambda b, h, n: (b, h)),
            pl.BlockSpec((nb, hd + SUBLANES, bq), lambda b, h, n: (b, h, 0)),
            pl.BlockSpec((nb, None, hd), lambda b, h, n: (b, 0, h)),
        ],
        out_specs=pl.BlockSpec((bq, hd), lambda b, h, n: (b * nb + n, h)),
        out_shape=jax.ShapeDtypeStruct((bsz * seq, d), BF16),
        scratch_shapes=[pltpu.VMEM((bq, bq), F32)] * 8,
        compiler_params=_cparams("parallel", "parallel", "arbitrary"),
        name="mb_attn",
    )(q, kb, vt, km)


PAGES_PER_STEP = 16


def _page_sum_kernel(pt_ref, *refs):
    pages, o_ref = refs[:PAGES_PER_STEP], refs[PAGES_PER_STEP]
    per_blk = MB_BLOCK // PAGE_SIZE
    for r in range(PAGES_PER_STEP // per_blk):
        s = jnp.sum(pages[per_blk * r][...], axis=0)
        for i in range(1, per_blk):
            s = s + jnp.sum(pages[per_blk * r + i][...], axis=0)
        o_ref[r] = s


def _page_block_sums(pool, page_table, layer):
    bd, n_pages = page_table.shape
    heads, hd = pool.shape[-2:]
    steps = n_pages // PAGES_PER_STEP
    rows = PAGES_PER_STEP * PAGE_SIZE // MB_BLOCK

    def page_spec(i):
        return pl.BlockSpec(
            (None, None, PAGE_SIZE, heads, hd),
            lambda b, s, pt: (layer, pt[b * n_pages + s * PAGES_PER_STEP + i], 0, 0, 0))

    return pl.pallas_call(
        _page_sum_kernel,
        grid_spec=pltpu.PrefetchScalarGridSpec(
            num_scalar_prefetch=1,
            grid=(bd, steps),
            in_specs=[page_spec(i) for i in range(PAGES_PER_STEP)],
            out_specs=pl.BlockSpec((None, rows, heads, hd), lambda b, s, pt: (b, s, 0, 0)),
        ),
        out_shape=jax.ShapeDtypeStruct((bd, steps * rows, heads, hd), F32),
        compiler_params=_cparams("parallel", "arbitrary"),
        name="mb_page_sums",
    )(page_table.reshape(-1), *([pool] * PAGES_PER_STEP))


def _mb_sample_select_kernel(q_ref, kn_ref, bs_ref, sel_ref, *, own):
    nb = bs_ref.shape[0]
    n_rows = nb + SUBLANES
    inv_blk = 1.0 / MB_BLOCK
    q = q_ref[...]
    sc = jnp.sum(bs_ref[...] * inv_blk * q[None], axis=-1, keepdims=True)
    own_sc = jnp.sum(kn_ref[...] * inv_blk * q, axis=-1, keepdims=True)
    tail_row = lax.broadcasted_iota(I32, (SUBLANES, MB_HEADS, 1), 0)
    tail = jnp.where(tail_row == 0, own_sc[None], -jnp.inf)
    blk = lax.broadcasted_iota(I32, (n_rows, MB_HEADS, 1), 0)
    s = jnp.where(blk < own, jnp.concatenate([sc, tail], axis=0), -jnp.inf)
    out_lane = lax.broadcasted_iota(I32, (MB_HEADS, LANES), 1)
    out = jnp.zeros((MB_HEADS, LANES), I32)
    for r in range(MB_TOP_K):
        m = jnp.max(s, axis=0, keepdims=True)
        idx = jnp.min(jnp.where(s == m, blk, n_rows), axis=0, keepdims=True)
        s = jnp.where(blk == idx, -jnp.inf, s)
        idx = idx[0]
        out = jnp.where(out_lane == r, idx, out)
        out = jnp.where(out_lane == MB_TOP_K + 1 + r, (idx < own).astype(I32), out)
    sel_ref[...] = out


def _mb_sample_select(q, k_new, blk_sums, own):
    bd = q.shape[0]
    nb = blk_sums.shape[1]
    return pl.pallas_call(
        functools.partial(_mb_sample_select_kernel, own=own),
        grid=(bd,),
        in_specs=[pl.BlockSpec((None, MB_HEADS, HEAD_DIM), lambda b: (b, 0, 0)),
                  pl.BlockSpec((None, MB_HEADS, HEAD_DIM), lambda b: (b, 0, 0)),
                  pl.BlockSpec((None, nb, MB_HEADS, HEAD_DIM), lambda b: (b, 0, 0, 0))],
        out_specs=pl.BlockSpec((None, MB_HEADS, LANES), lambda b: (b, 0, 0)),
        out_shape=jax.ShapeDtypeStruct((bd, MB_HEADS, LANES), I32),
        compiler_params=_cparams("parallel"),
        name="mb_sample_select",
    )(q, k_new, blk_sums)


def _mb_sample_attn_kernel(pt_ref, sel_ref, q_ref, kn_ref, vn_ref, *refs):
    n_pg = MB_TOP_K * (MB_BLOCK // PAGE_SIZE)
    k_pages, v_pages, o_ref = refs[:n_pg], refs[n_pg:2 * n_pg], refs[2 * n_pg]
    b, h = pl.program_id(0), pl.program_id(1)
    scale = HEAD_DIM ** -0.5
    rows_pg = PAGE_SIZE * MB_HEADS
    q = q_ref[...]
    q8 = jnp.broadcast_to(q, (SUBLANES, HEAD_DIM)).astype(BF16)
    flat = lambda r: r[...].reshape(rows_pg, HEAD_DIM).astype(BF16)
    keys = jnp.concatenate([flat(r) for r in k_pages], axis=0)
    vals = jnp.concatenate([flat(r) for r in v_pages], axis=0)
    lg = _dot_nt(q8, keys) * scale
    lane = lax.broadcasted_iota(I32, lg.shape, 1)
    lg = jnp.where((lane & (MB_HEADS - 1)) == h, lg, NEG_BIG)
    base = (b * MB_HEADS + h) * SUBLANES
    blk_cols = rows_pg * (MB_BLOCK // PAGE_SIZE)
    for r in range(MB_TOP_K):
        penalty = jnp.where(sel_ref[base + MB_TOP_K + 1 + r] == 0, NEG_BIG, 0.0)
        in_blk = (lane >= r * blk_cols) & (lane < (r + 1) * blk_cols)
        lg = jnp.where(in_blk, lg + penalty, lg)
    qb = q.astype(BF16).astype(F32)
    lo = jnp.sum(qb * kn_ref[...].astype(BF16).astype(F32), axis=-1, keepdims=True) * scale
    m = jnp.maximum(jnp.max(lg, axis=-1, keepdims=True), lo)
    p = jnp.exp(lg - m)
    po = jnp.exp(lo - m)
    l = jnp.sum(p, axis=-1, keepdims=True) + po
    vn = vn_ref[...].astype(BF16).astype(F32)
    out = (_dot(p.astype(BF16), vals) + po.astype(BF16).astype(F32) * vn) / l
    o_ref[...] = out[0:1, :]


def _mb_sample_attn(pool_k, pool_v, page_table, sel, q, k_new, v_new, layer):
    bd, n_pages = page_table.shape
    hd = HEAD_DIM
    per_blk = MB_BLOCK // PAGE_SIZE

    def page_spec(r, half):
        def imap(b, h, pt, sl):
            blk = sl[(b * MB_HEADS + h) * SUBLANES + r]
            return (layer, pt[b * n_pages + blk * per_blk + half], 0, 0, 0)
        return pl.BlockSpec((None, None, PAGE_SIZE, MB_HEADS, hd), imap)

    pages = [page_spec(r, half) for r in range(MB_TOP_K) for half in range(per_blk)]
    vec = pl.BlockSpec((None, None, 1, hd), lambda b, h, pt, sl: (b, h, 0, 0))
    q4 = q.reshape(bd, MB_HEADS, 1, hd)
    out = pl.pallas_call(
        _mb_sample_attn_kernel,
        grid_spec=pltpu.PrefetchScalarGridSpec(
            num_scalar_prefetch=2,
            grid=(bd, MB_HEADS),
            in_specs=[vec, vec, vec] + pages + pages,
            out_specs=vec,
        ),
        out_shape=jax.ShapeDtypeStruct((bd, MB_HEADS, 1, hd), F32),
        compiler_params=_cparams("parallel", "arbitrary"),
        name="mb_sample_attn",
    )(page_table.reshape(-1), sel[:, :, :SUBLANES].reshape(-1), q4, k_new.reshape(q4.shape), v_new.reshape(q4.shape),
      *([pool_k] * len(pages)), *([pool_v] * len(pages)))
    return out.reshape(bd, MB_HEADS * hd)


def _ffn_kernel(y_ref, g_ref, wg_ref, wv_ref, cwg_ref, cwv_ref, cbg_ref, cbv_ref, wo_ref,
                out_ref, bufg_ref, bufv_ref, h_scr, ug_scr, uv_scr, cg_scr, cv_scr, acc_scr,
                *, tiles_per_seq):
    i, f = pl.program_id(0), pl.program_id(1)
    tm = y_ref.shape[0]
    first = (i % tiles_per_seq) == 0

    @pl.when(f == 0)
    def _():
        h_scr[...] = _rms_rows(y_ref[...], g_ref[...]).astype(BF16)
        acc_scr[...] = jnp.zeros(acc_scr.shape, F32)

    @pl.when(first)
    def _():
        cg_scr[f] = jnp.zeros(cg_scr.shape[1:], F32)
        cv_scr[f] = jnp.zeros(cv_scr.shape[1:], F32)

    def branch(w_ref, cw_ref, cb_ref, u_scr, carry, buf_ref):
        u = _dot(h_scr[...], w_ref[...])
        u_scr[0:SUBLANES, :] = carry[f]
        u_scr[SUBLANES:SUBLANES + tm, :] = u
        r = cw_ref[FFN_CONV - 1:FFN_CONV, :] * u + cb_ref[...]
        for d in range(1, FFN_CONV):
            r = r + cw_ref[FFN_CONV - 1 - d:FFN_CONV - d, :] * u_scr[SUBLANES - d:SUBLANES - d + tm, :]
        last = u_scr[tm:tm + SUBLANES, :]
        carry[f] = last
        buf_ref[...] = last
        return r

    gate = branch(wg_ref, cwg_ref, cbg_ref, ug_scr, cg_scr, bufg_ref)
    val = branch(wv_ref, cwv_ref, cbv_ref, uv_scr, cv_scr, bufv_ref)
    act = (_silu(gate) * val).astype(BF16)
    acc_scr[...] += _dot(act, wo_ref[...])

    @pl.when(f == pl.num_programs(1) - 1)
    def _():
        out_ref[...] = y_ref[...] + acc_scr[...]


def _ffn_tile(dff):
    for cand in (1408, 1024, 512, 256, 128):
        if dff % cand == 0:
            return cand
    raise ValueError(f"unsupported d_ff {dff}")


def _ffn_prompt(y, g, w_in, conv_w, conv_b, w_out, bsz, seq):
    m, d = y.shape
    dff = w_out.shape[0]
    tm = min(512, seq)
    tf = _ffn_tile(dff)
    nf = dff // tf
    tiles_per_seq = seq // tm
    gate = lambda i, f: (0, f)
    val = lambda i, f: (0, f + nf)
    buf_spec = pl.BlockSpec((None, SUBLANES, tf), lambda i, f: (i, 0, f))
    out, bufg, bufv = pl.pallas_call(
        functools.partial(_ffn_kernel, tiles_per_seq=tiles_per_seq),
        grid=(m // tm, nf),
        in_specs=[pl.BlockSpec((tm, d), lambda i, f: (i, 0)),
                  pl.BlockSpec((1, d), lambda i, f: (0, 0)),
                  pl.BlockSpec((d, tf), gate), pl.BlockSpec((d, tf), val),
                  pl.BlockSpec((FFN_CONV, tf), gate), pl.BlockSpec((FFN_CONV, tf), val),
                  pl.BlockSpec((1, tf), gate), pl.BlockSpec((1, tf), val),
                  pl.BlockSpec((tf, d), lambda i, f: (f, 0))],
        out_specs=[pl.BlockSpec((tm, d), lambda i, f: (i, 0)), buf_spec, buf_spec],
        out_shape=[jax.ShapeDtypeStruct((m, d), F32),
                   jax.ShapeDtypeStruct((m // tm, SUBLANES, dff), F32),
                   jax.ShapeDtypeStruct((m // tm, SUBLANES, dff), F32)],
        scratch_shapes=[pltpu.VMEM((tm, d), BF16),
                        pltpu.VMEM((tm + SUBLANES, tf), F32), pltpu.VMEM((tm + SUBLANES, tf), F32),
                        pltpu.VMEM((nf, SUBLANES, tf), F32), pltpu.VMEM((nf, SUBLANES, tf), F32),
                        pltpu.VMEM((tm, d), F32)],
        compiler_params=_cparams("arbitrary", "arbitrary"),
        name="ffn_prompt",
    )(y, g, w_in, w_in, conv_w, conv_w, conv_b, conv_b, w_out)
    keep = FFN_CONV - 1
    last = slice(tiles_per_seq - 1, None, tiles_per_seq)
    new_buf = jnp.concatenate([bufg[last, SUBLANES - keep:], bufv[last, SUBLANES - keep:]], axis=-1)
    return out, new_buf


def _ffn_sample_kernel(y_ref, g_ref, wg_ref, wv_ref, cwg_ref, cwv_ref, cbg_ref, cbv_ref,
                       sg_ref, sv_ref, wo_ref, out_ref, ng_ref, nv_ref, h_scr, acc_scr):
    f = pl.program_id(0)

    @pl.when(f == 0)
    def _():
        h_scr[...] = _rms_rows(y_ref[...], g_ref[...]).astype(BF16)
        acc_scr[...] = jnp.zeros(acc_scr.shape, F32)

    def branch(w_ref, cw_ref, cb_ref, st_ref, new_ref):
        u = _dot(h_scr[...], w_ref[...])
        r = cw_ref[FFN_CONV - 1:FFN_CONV, :] * u + cb_ref[...]
        for i in range(FFN_CONV - 1):
            r = r + cw_ref[i:i + 1, :] * st_ref[i]
        for i in range(FFN_CONV - 2):
            new_ref[i] = st_ref[i + 1]
        new_ref[FFN_CONV - 2] = u
        return r

    gate = branch(wg_ref, cwg_ref, cbg_ref, sg_ref, ng_ref)
    val = branch(wv_ref, cwv_ref, cbv_ref, sv_ref, nv_ref)
    acc_scr[...] += _dot((_silu(gate) * val).astype(BF16), wo_ref[...])

    @pl.when(f == pl.num_programs(0) - 1)
    def _():
        out_ref[...] = y_ref[...] + acc_scr[...]


def _ffn_sample(y, g, w_in, conv_w, conv_b, w_out, state):
    bd, d = y.shape
    dff = w_out.shape[0]
    tf = _ffn_tile(dff)
    nf = dff // tf
    keep = FFN_CONV - 1
    st = jnp.transpose(state, (1, 0, 2))
    gate = lambda f: (0, f)
    val = lambda f: (0, f + nf)
    st_gate = pl.BlockSpec((keep, bd, tf), lambda f: (0, 0, f))
    st_val = pl.BlockSpec((keep, bd, tf), lambda f: (0, 0, f + nf))
    new_spec = pl.BlockSpec((keep, bd, tf), lambda f: (0, 0, f))
    out, ng, nv = pl.pallas_call(
        _ffn_sample_kernel,
        grid=(nf,),
        in_specs=[pl.BlockSpec((bd, d), lambda f: (0, 0)),
                  pl.BlockSpec((1, d), lambda f: (0, 0)),
                  pl.BlockSpec((d, tf), gate), pl.BlockSpec((d, tf), val),
                  pl.BlockSpec((FFN_CONV, tf), gate), pl.BlockSpec((FFN_CONV, tf), val),
                  pl.BlockSpec((1, tf), gate), pl.BlockSpec((1, tf), val),
                  st_gate, st_val,
                  pl.BlockSpec((tf, d), lambda f: (f, 0))],
        out_specs=[pl.BlockSpec((bd, d), lambda f: (0, 0)), new_spec, new_spec],
        out_shape=[jax.ShapeDtypeStruct((bd, d), F32),
                   jax.ShapeDtypeStruct((keep, bd, dff), F32),
                   jax.ShapeDtypeStruct((keep, bd, dff), F32)],
        scratch_shapes=[pltpu.VMEM((bd, d), BF16), pltpu.VMEM((bd, d), F32)],
        compiler_params=_cparams("arbitrary"),
        name="ffn_sample",
    )(y, g, w_in, w_in, conv_w, conv_w, conv_b, conv_b, st, st, w_out)
    new_state = jnp.transpose(jnp.concatenate([ng, nv], axis=-1), (1, 0, 2))
    return out, new_state


def _final_norm_kernel(x_ref, g_ref, o_ref):
    o_ref[...] = _rms_rows(x_ref[...], g_ref[...])


def _final_norm(x, g):
    m, d = x.shape
    tm = min(1024, m)
    return pl.pallas_call(
        _final_norm_kernel,
        grid=(m // tm,),
        in_specs=[pl.BlockSpec((tm, d), lambda i: (i, 0)), pl.BlockSpec((1, d), lambda i: (0, 0))],
        out_specs=pl.BlockSpec((tm, d), lambda i: (i, 0)),
        out_shape=jax.ShapeDtypeStruct((m, d), F32),
        compiler_params=_cparams("parallel"),
        name="final_norm",
    )(x, g)


def _rope_tables(pos):
    half = HEAD_DIM // 2
    inv = ROPE_THETA ** (-jnp.arange(half, dtype=F32) / half)
    ang = pos.astype(F32)[:, None] * inv[None, :]
    cos, sin = jnp.cos(ang), jnp.sin(ang)
    return jnp.concatenate([cos, cos], axis=-1), jnp.concatenate([-sin, sin], axis=-1)


def _dn_gate_layout(w_in, a_log, dt_bias):
    d = w_in.shape[0]
    rep = DN_V_HEADS // DN_QK_HEADS
    w_beta = w_in[:, DN_MAIN_DIM:DN_MAIN_DIM + DN_V_HEADS].reshape(d, DN_QK_HEADS, rep)
    w_dec = w_in[:, DN_MAIN_DIM + DN_V_HEADS:].reshape(d, DN_QK_HEADS, rep)
    pad = jnp.zeros((d, DN_QK_HEADS, SUBLANES - 2 * rep), w_in.dtype)
    w_bg = jnp.concatenate([w_beta, w_dec, pad], axis=-1).reshape(d, DN_QK_HEADS * SUBLANES)
    w_bg = jnp.pad(w_bg, ((0, 0), (0, LANES - w_bg.shape[1])))

    def lanes(per_head, fill):
        v = per_head.reshape(DN_QK_HEADS, rep)
        z = jnp.full((DN_QK_HEADS, rep), fill, F32)
        p = jnp.full((DN_QK_HEADS, SUBLANES - 2 * rep), fill, F32)
        return v, z, p

    a, zero, padz = lanes(-jnp.exp(a_log.astype(F32)), 0.0)
    neg_a = jnp.concatenate([zero, a, padz], axis=-1).reshape(-1)
    dt, _, _ = lanes(dt_bias.astype(F32), 0.0)
    dt_b = jnp.concatenate([zero, dt, padz], axis=-1).reshape(-1)
    is_beta = jnp.concatenate([zero + 1.0, zero, padz], axis=-1).reshape(-1)
    par = jnp.stack([neg_a, dt_b, is_beta])
    par = jnp.pad(par, ((0, SUBLANES - par.shape[0]), (0, LANES - par.shape[1])))
    return w_bg.astype(BF16), par


def kernel(x_prompt, x_sample, cache_k, cache_v, state_dn, state_dn_conv, state_ffn_conv, page_table,
           norm_mix, norm_ffn, norm_out, dn_w_in, dn_conv_w, dn_a_log, dn_dt_bias, dn_norm, dn_w_out,
           mb_w_qkv, mb_w_o, ffn_w_in, ffn_conv_w, ffn_conv_b, ffn_w_out):
    bsz, seq, d = x_prompt.shape
    bd = x_sample.shape[0]
    depth = norm_mix.shape[0]
    n_pages = page_table.shape[1]
    past = n_pages * PAGE_SIZE
    own_blk = past // MB_BLOCK
    hd = HEAD_DIM

    yp = x_prompt.reshape(bsz * seq, d)
    ys = x_sample.reshape(bd, d)
    cos_p, sin_p = _rope_tables(jnp.arange(seq, dtype=I32))
    cos_s, sin_s = _rope_tables(jnp.full((bd,), past, dtype=I32))
    pool_k, pool_v = cache_k, cache_v

    kp_l, vp_l, ks_l, vs_l = [], [], [], []
    sp_l, ss_l, cp_l, cs_l = [], [], [], []
    fp_l, fs_l = [], []
    for layer in range(depth):
        j = layer // 2
        g_mix = norm_mix[layer].reshape(1, d)
        if layer % 2 == 0:
            w_main = dn_w_in[j][:, :DN_MAIN_DIM].astype(BF16)
            w_bg, par = _dn_gate_layout(dn_w_in[j], dn_a_log[j], dn_dt_bias[j])
            w_out = dn_w_out[j].astype(BF16)
            nw = dn_norm[j].reshape(1, hd)
            proj, bg, bgt = _dn_inproj(yp, g_mix, w_main, w_bg, par, transposed=True)
            o, s_new = _dn_core(proj, bg, bgt, dn_conv_w[j], nw, bsz, seq)
            yp = _matmul_residual(o, w_out, yp)
            sp_l.append(s_new)
            cp_l.append(proj.reshape(bsz, seq, DN_MAIN_DIM)[:, seq - (DN_CONV - 1):, :DN_CONV_DIM])

            proj_s, bg_s = _dn_inproj(ys, g_mix, w_main, w_bg, par, transposed=False)
            o_s, s_s, c_s = _dn_sample(proj_s, bg_s, state_dn_conv[j], dn_conv_w[j], state_dn[j], nw)
            ys = _matmul_residual(o_s, w_out, ys)
            ss_l.append(s_s)
            cs_l.append(c_s)
        else:
            w_qkv = mb_w_qkv[j].astype(BF16)
            w_o = mb_w_o[j].astype(BF16)
            q, k, v, kb, vt, km = _mb_qkv(yp, g_mix, w_qkv, cos_p, sin_p, with_blocks=True)
            att = _mb_attn(q, kb, vt, km, bsz, seq)
            yp = _matmul_residual(att, w_o, yp)
            kp_l.append(k.reshape(bsz, seq, MB_HEADS, hd))
            vp_l.append(v.reshape(bsz, seq, MB_HEADS, hd))

            q_s, k_s, v_s = _mb_qkv(ys, g_mix, w_qkv, cos_s, sin_s, with_blocks=False)
            blk_sums = _page_block_sums(pool_k, page_table, j)
            q3 = q_s.reshape(bd, MB_HEADS, hd)
            k3 = k_s.reshape(bd, MB_HEADS, hd)
            sel = _mb_sample_select(q3, k3, blk_sums, own_blk)
            att_s = _mb_sample_attn(pool_k, pool_v, page_table, sel, q3, k3, v_s.reshape(bd, MB_HEADS, hd), j)
            ys = _matmul_residual(att_s.astype(BF16), w_o, ys)
            ks_l.append(k_s.reshape(bd, 1, MB_HEADS, hd))
            vs_l.append(v_s.reshape(bd, 1, MB_HEADS, hd))

        g_ffn = norm_ffn[layer].reshape(1, d)
        w_fi = ffn_w_in[layer].astype(BF16)
        w_fo = ffn_w_out[layer].astype(BF16)
        cb = ffn_conv_b[layer].reshape(1, -1)
        yp, fbp = _ffn_prompt(yp, g_ffn, w_fi, ffn_conv_w[layer], cb, w_fo, bsz, seq)
        ys, fbs = _ffn_sample(ys, g_ffn, w_fi, ffn_conv_w[layer], cb, w_fo, state_ffn_conv[layer])
        fp_l.append(fbp)
        fs_l.append(fbs)

    g_out = norm_out.reshape(1, d)
    y_prompt = _final_norm(yp, g_out).reshape(bsz, seq, d)
    y_sample = _final_norm(ys, g_out).reshape(bd, 1, d)
    return (y_prompt, y_sample, jnp.stack(kp_l), jnp.stack(vp_l), jnp.stack(ks_l), jnp.stack(vs_l),
            jnp.stack(sp_l), jnp.stack(ss_l), jnp.stack(cp_l), jnp.stack(cs_l), jnp.stack(fp_l), jnp.stack(fs_l))
```

```python
import functools

import jax
import jax.numpy as jnp
from jax import lax
from jax.experimental import pallas as pl
from jax.experimental.pallas import tpu as pltpu

F32 = jnp.float32
BF16 = jnp.bfloat16
I32 = jnp.int32

NORM_EPS = 1e-6
ROPE_THETA = 10000.0
HEAD_DIM = 128
DN_QK_HEADS = 8
DN_V_HEADS = 16
DN_CONV = 4
DN_KEY_DIM = DN_QK_HEADS * HEAD_DIM
DN_VAL_DIM = DN_V_HEADS * HEAD_DIM
DN_CONV_DIM = 2 * DN_KEY_DIM + DN_VAL_DIM
DN_MAIN_DIM = DN_CONV_DIM + DN_VAL_DIM
DN_CHUNK = 64
DN_TBLOCK = 128
DN_HEADS_PER_STEP = 8
MB_HEADS = 8
MB_BLOCK = 256
MB_TOP_K = 3
MB_V_PAD = 16
MB_HEADS_PER_STEP = 2
PAGE_SIZE = 128
FFN_CONV = 3
SUBLANES = 8
LANES = 128
NEG_BIG = -1e30
VMEM_LIMIT = 56 * 1024 * 1024


def _cparams(*sem):
    return pltpu.CompilerParams(dimension_semantics=sem, vmem_limit_bytes=VMEM_LIMIT)


def _sigmoid(x):
    return 1.0 / (1.0 + jnp.exp(-x))


def _silu(x):
    return x * _sigmoid(x)


def _softplus(x):
    return jnp.maximum(x, 0.0) + jnp.log1p(jnp.exp(-jnp.abs(x)))


def _rms_rows(x, g):
    ms = jnp.mean(x * x, axis=-1, keepdims=True)
    return x * lax.rsqrt(ms + NORM_EPS) * g


def _dot(a, b):
    return jnp.dot(a, b, preferred_element_type=F32)


def _dot_nt(a, b):
    return lax.dot_general(a, b, (((1,), (1,)), ((), ())), preferred_element_type=F32)


def _dot_tn(a, b):
    return lax.dot_general(a, b, (((0,), (0,)), ((), ())), preferred_element_type=F32)


def _split2(a):
    hi = a.astype(BF16)
    lo = (a - hi.astype(F32)).astype(BF16)
    return hi, lo


def _mm3_pair(a, b, left):
    (ah, al), (bh, bl) = a, b
    right = 1 - left
    zero = jnp.zeros_like(bh)
    bh_l, bh_r, bl_l, bl_r = bh * left, bh * right, bl * left, bl * right
    rhs = jnp.concatenate([jnp.concatenate([bh_l, bl_l], axis=1),
                           jnp.concatenate([bh_r, bl_r], axis=1),
                           jnp.concatenate([bh_l, zero], axis=1),
                           jnp.concatenate([bh_r, zero], axis=1)], axis=0)
    out = _dot(jnp.concatenate([ah, al], axis=1), rhs)
    n = bh.shape[1]
    return out[:, :n] + out[:, n:]


def _dot_exact01(m01, x):
    hi = x.astype(BF16)
    r1 = x - hi.astype(F32)
    mid = r1.astype(BF16)
    lo = (r1 - mid.astype(F32)).astype(BF16)
    return _dot(m01, hi) + (_dot(m01, mid) + _dot(m01, lo))


def _dot_exact01_r(x, m01):
    hi = x.astype(BF16)
    r1 = x - hi.astype(F32)
    mid = r1.astype(BF16)
    lo = (r1 - mid.astype(F32)).astype(BF16)
    return _dot(hi, m01) + (_dot(mid, m01) + _dot(lo, m01))


def _dn_inproj_kernel(x_ref, g_ref, w_ref, wbg_ref, par_ref, o_ref, bg_ref, *rest, transposed):
    if transposed:
        bgt_ref, h_scr = rest
    else:
        (h_scr,) = rest

    @pl.when(pl.program_id(1) == 0)
    def _():
        h = _rms_rows(x_ref[...], g_ref[...]).astype(BF16)
        h_scr[...] = h
        raw = _dot(h, wbg_ref[...])
        neg_a, dt_b, is_beta = par_ref[0:1, :], par_ref[1:2, :], par_ref[2:3, :]
        act = jnp.where(is_beta > 0.5, _sigmoid(raw), neg_a * _softplus(raw + dt_b))
        bg_ref[...] = act
        if transposed:
            bgt_ref[...] = act.T

    o_ref[...] = _dot(h_scr[...], w_ref[...])


def _dn_inproj(x, g, w_main, w_bg, par, *, transposed):
    m, d = x.shape
    n = w_main.shape[1]
    tm = min(512, m)
    tn = 2048
    out_shape = [jax.ShapeDtypeStruct((m, n), F32), jax.ShapeDtypeStruct((m, LANES), F32)]
    out_specs = [pl.BlockSpec((tm, tn), lambda i, j: (i, j)),
                 pl.BlockSpec((tm, LANES), lambda i, j: (i, 0))]
    if transposed:
        out_shape.append(jax.ShapeDtypeStruct((LANES, m), F32))
        out_specs.append(pl.BlockSpec((LANES, tm), lambda i, j: (0, i)))
    return pl.pallas_call(
        functools.partial(_dn_inproj_kernel, transposed=transposed),
        grid=(m // tm, n // tn),
        in_specs=[pl.BlockSpec((tm, d), lambda i, j: (i, 0)),
                  pl.BlockSpec((1, d), lambda i, j: (0, 0)),
                  pl.BlockSpec((d, tn), lambda i, j: (0, j)),
                  pl.BlockSpec((d, LANES), lambda i, j: (0, 0)),
                  pl.BlockSpec((SUBLANES, LANES), lambda i, j: (0, 0))],
        out_specs=out_specs,
        out_shape=out_shape,
        scratch_shapes=[pltpu.VMEM((tm, d), BF16)],
        compiler_params=_cparams("parallel", "arbitrary"),
        name="dn_inproj",
    )(x, g, w_main, w_bg, par)


def _unit_lower_inverse_pairs(lows, c):
    row = lax.broadcasted_iota(I32, (c, 2 * c), 0)
    lane = lax.broadcasted_iota(I32, (c, 2 * c), 1)
    col = lane & (c - 1)
    left = jnp.where(lane < c, 1.0, 0.0).astype(BF16)

    def same_block(shift):
        return (row >> shift) == (col >> shift)

    low_parts = [_split2(low) for low in lows]
    base = same_block(3)
    negs = [jnp.where(base, -low, 0.0) for low in lows]
    eye = jnp.where(row == col, 1.0, 0.0)
    invs = [eye + neg for neg in negs]
    powers = [_split2(neg) for neg in negs]
    for _ in range(2):
        powers = [_split2(_mm3_pair(p, p, left)) for p in powers]
        invs = [inv + _mm3_pair(_split2(inv), p, left) for inv, p in zip(invs, powers)]
    shift = 3
    while (1 << shift) < c:
        mask = same_block(shift + 1) & jnp.logical_not(same_block(shift))
        inv_parts = [_split2(inv) for inv in invs]
        tmps = [_split2(jnp.where(mask, _mm3_pair(lp, ip, left), 0.0))
                for lp, ip in zip(low_parts, inv_parts)]
        invs = [inv - _mm3_pair(ip, t, left) for inv, ip, t in zip(invs, inv_parts, tmps)]
        shift += 1
    return invs


def _dn_core_kernel(q_ref, k_ref, v_ref, z_ref, bg_ref, bgt_ref, cwq_ref, cwk_ref, cwv_ref, nw_ref,
                    o_ref, s_out_ref, xe_scr, s_scr, *, tb, chunk, nh):
    hg = pl.program_id(1)
    t = pl.program_id(2)
    hd = HEAD_DIM
    n_chunks = tb // chunk
    k_off, v_off = nh * hd, 2 * nh * hd

    @pl.when(t == 0)
    def _():
        s_scr[...] = jnp.zeros(s_scr.shape, F32)
        xe_scr[0:SUBLANES, :] = jnp.zeros((SUBLANES, xe_scr.shape[1]), F32)

    xe_scr[SUBLANES:SUBLANES + tb, 0:k_off] = q_ref[...]
    xe_scr[SUBLANES:SUBLANES + tb, k_off:v_off] = k_ref[...]
    xe_scr[SUBLANES:SUBLANES + tb, v_off:] = v_ref[...]

    def conv(lo, hi, cw_ref, w_lo):
        w = cw_ref[:, w_lo:w_lo + hi - lo]
        acc = w[DN_CONV - 1:DN_CONV, :] * xe_scr[SUBLANES:SUBLANES + tb, lo:hi]
        for d in range(1, DN_CONV):
            acc = acc + w[DN_CONV - 1 - d:DN_CONV - d, :] * xe_scr[SUBLANES - d:SUBLANES - d + tb, lo:hi]
        return _silu(acc)

    ri = lax.broadcasted_iota(I32, (tb, tb), 0)
    ci = lax.broadcasted_iota(I32, (tb, tb), 1)
    cshift = chunk.bit_length() - 1
    same_chunk = (ri >> cshift) == (ci >> cshift)
    upper = jnp.where(same_chunk & (ri <= ci), 1.0, 0.0).astype(BF16)
    lower = jnp.where(same_chunk & (ri >= ci), 1.0, 0.0).astype(BF16)

    rc = lax.broadcasted_iota(I32, (chunk, 2 * chunk), 0)
    lane = lax.broadcasted_iota(I32, (chunk, 2 * chunk), 1)
    cc = lane & (chunk - 1)
    is_left = lane < chunk
    nw = nw_ref[...]

    items = []
    for g in range(nh):
        qc = conv(g * hd, (g + 1) * hd, cwq_ref, g * hd)
        kc = conv(k_off + g * hd, k_off + (g + 1) * hd, cwk_ref, g * hd)
        vc = conv(v_off + 2 * g * hd, v_off + 2 * (g + 1) * hd, cwv_ref, 2 * g * hd)
        qn = qc * lax.rsqrt(jnp.sum(qc * qc, axis=-1, keepdims=True) + NORM_EPS) * (hd ** -0.5)
        kn = kc * lax.rsqrt(jnp.sum(kc * kc, axis=-1, keepdims=True) + NORM_EPS)
        hq = hg * nh + g
        bgr = pltpu.roll(bg_ref[...], (LANES - SUBLANES * hq) % LANES, axis=1)
        bgt = bgt_ref[g * SUBLANES:(g + 1) * SUBLANES, :]
        gc_rows = _dot_exact01_r(bgt, upper)
        gc_cols = _dot_exact01(lower, bgr)
        for c in range(n_chunks):
            r0, r1 = c * chunk, (c + 1) * chunk
            q_c, k_c = qn[r0:r1], kn[r0:r1]
            qb, kb = q_c.astype(BF16), k_c.astype(BF16)
            prods = _dot_nt(jnp.concatenate([qb, kb], axis=0), jnp.concatenate([kb, kb], axis=0))
            qk2, kk2 = prods[:chunk], prods[chunk:]
            g_cols = [gc_cols[r0:r1, 2 + x:3 + x] for x in range(2)]
            b_cols = [bgr[r0:r1, x:x + 1] for x in range(2)]
            g_row2 = jnp.concatenate([gc_rows[2 + x:3 + x, r0:r1] for x in range(2)], axis=1)
            g_col2 = jnp.where(is_left, g_cols[0], g_cols[1])
            b_col2 = jnp.where(is_left, b_cols[0], b_cols[1])
            decay2 = jnp.exp(jnp.where(rc >= cc, g_col2 - g_row2, -jnp.inf))
            heads = []
            for x in range(2):
                eg = jnp.exp(g_cols[x])
                g_last = g_cols[x][chunk - 1:chunk, :]
                v_c = vc[r0:r1, x * hd:(x + 1) * hd]
                heads.append(dict(
                    rhs=jnp.concatenate([b_cols[x] * v_c, (b_cols[x] * eg) * k_c], axis=1).astype(BF16),
                    qg=(q_c * eg).astype(BF16),
                    kd=(k_c * jnp.exp(g_last - g_cols[x])).astype(BF16),
                    gl=jnp.exp(g_last)))
            items.append(dict(g=g, c=c, low=jnp.where(rc > cc, b_col2 * decay2 * kk2, 0.0),
                              qkd=(qk2 * decay2).astype(BF16), heads=heads))
    xe_scr[0:SUBLANES, :] = xe_scr[tb:tb + SUBLANES, :]

    invs = _unit_lower_inverse_pairs([it["low"] for it in items], chunk)
    for inv, it in zip(invs, items):
        ra, rb = it["heads"][0]["rhs"], it["heads"][1]["rhs"]
        zero = jnp.zeros_like(ra)
        rhs = jnp.concatenate([jnp.concatenate([ra, zero], axis=1),
                               jnp.concatenate([zero, rb], axis=1)], axis=0)
        it["sol"] = _dot(inv.astype(BF16), rhs)

    states = [s_scr[h] for h in range(2 * nh)]
    for c in range(n_chunks):
        r0, r1 = c * chunk, (c + 1) * chunk
        row = [it for it in items if it["c"] == c]
        sbs = [states[2 * it["g"] + x].astype(BF16) for it in row for x in range(2)]
        ws_qs = [_dot(jnp.concatenate([it["sol"][:, (2 * x + 1) * hd:(2 * x + 2) * hd].astype(BF16),
                                       it["heads"][x]["qg"]], axis=0), sbs[2 * i + x])
                 for i, it in enumerate(row) for x in range(2)]
        ubs = [(it["sol"][:, 2 * x * hd:(2 * x + 1) * hd] - ws_qs[2 * i + x][:chunk]).astype(BF16)
               for i, it in enumerate(row) for x in range(2)]
        intras = []
        for i, it in enumerate(row):
            zero = jnp.zeros_like(ubs[0])
            u_diag = jnp.concatenate([jnp.concatenate([ubs[2 * i], zero], axis=1),
                                      jnp.concatenate([zero, ubs[2 * i + 1]], axis=1)], axis=0)
            intras.append(_dot(it["qkd"], u_diag))
        for i, it in enumerate(row):
            for x in range(2):
                h = 2 * it["g"] + x
                head = it["heads"][x]
                states[h] = head["gl"] * states[h] + _dot_tn(head["kd"], ubs[2 * i + x])
                o = ws_qs[2 * i + x][chunk:] + intras[i][:, x * hd:(x + 1) * hd]
                z_c = z_ref[r0:r1, h * hd:(h + 1) * hd]
                on = o * lax.rsqrt(jnp.mean(o * o, axis=-1, keepdims=True) + NORM_EPS) * nw * _silu(z_c)
                o_ref[r0:r1, h * hd:(h + 1) * hd] = on.astype(o_ref.dtype)
    for h in range(2 * nh):
        s_scr[h] = states[h]

    @pl.when(t == pl.num_programs(2) - 1)
    def _():
        s_out_ref[...] = s_scr[...]


def _dn_core(proj, bg, bgt, conv_w, norm_w, bsz, seq):
    tb, hd, nh = DN_TBLOCK, HEAD_DIM, DN_HEADS_PER_STEP
    nt = seq // tb
    ng = DN_QK_HEADS // nh
    kernel = functools.partial(_dn_core_kernel, tb=tb, chunk=DN_CHUNK, nh=nh)
    tok = lambda b, h, t: b * nt + t
    return pl.pallas_call(
        kernel,
        grid=(bsz, ng, nt),
        in_specs=[
            pl.BlockSpec((tb, nh * hd), lambda b, h, t: (tok(b, h, t), h)),
            pl.BlockSpec((tb, nh * hd), lambda b, h, t: (tok(b, h, t), ng + h)),
            pl.BlockSpec((tb, 2 * nh * hd), lambda b, h, t: (tok(b, h, t), ng + h)),
            pl.BlockSpec((tb, 2 * nh * hd), lambda b, h, t: (tok(b, h, t), 2 * ng + h)),
            pl.BlockSpec((tb, LANES), lambda b, h, t: (tok(b, h, t), 0)),
            pl.BlockSpec((nh * SUBLANES, tb), lambda b, h, t: (h, tok(b, h, t))),
            pl.BlockSpec((DN_CONV, nh * hd), lambda b, h, t: (0, h)),
            pl.BlockSpec((DN_CONV, nh * hd), lambda b, h, t: (0, ng + h)),
            pl.BlockSpec((DN_CONV, 2 * nh * hd), lambda b, h, t: (0, ng + h)),
            pl.BlockSpec((1, hd), lambda b, h, t: (0, 0)),
        ],
        out_specs=[
            pl.BlockSpec((tb, 2 * nh * hd), lambda b, h, t: (tok(b, h, t), h)),
            pl.BlockSpec((None, 2 * nh, hd, hd), lambda b, h, t: (b, h, 0, 0)),
        ],
        out_shape=[jax.ShapeDtypeStruct((bsz * seq, DN_VAL_DIM), BF16),
                   jax.ShapeDtypeStruct((bsz, DN_V_HEADS, hd, hd), F32)],
        scratch_shapes=[pltpu.VMEM((tb + SUBLANES, 4 * nh * hd), F32),
                        pltpu.VMEM((2 * nh, hd, hd), F32)],
        compiler_params=_cparams("parallel", "parallel", "arbitrary"),
        name="dn_core",
    )(proj, proj, proj, proj, bg, bgt, conv_w, conv_w, conv_w, norm_w)


def _dn_sample_kernel(p_ref, bg_ref, cst_ref, cw_ref, s_ref, nw_ref,
                      o_ref, s_out_ref, cst_out_ref, r_scr):
    hd = HEAD_DIM
    x = p_ref[:, 0:DN_CONV_DIM]
    buf = cst_ref[...]
    y = x * cw_ref[DN_CONV - 1:DN_CONV, :]
    for i in range(DN_CONV - 1):
        y = y + buf[i:i + 1, :] * cw_ref[i:i + 1, :]
    cst_out_ref[0:DN_CONV - 2, :] = buf[1:DN_CONV - 1, :]
    cst_out_ref[DN_CONV - 2:DN_CONV - 1, :] = x
    a = _silu(y)

    r_scr[...] = jnp.zeros(r_scr.shape, F32)
    qs, ks = [], []
    for h in range(DN_QK_HEADS):
        qh = a[:, h * hd:(h + 1) * hd]
        kh = a[:, DN_KEY_DIM + h * hd:DN_KEY_DIM + (h + 1) * hd]
        qh = qh * lax.rsqrt(jnp.sum(qh * qh, axis=-1, keepdims=True) + NORM_EPS) * (hd ** -0.5)
        kh = kh * lax.rsqrt(jnp.sum(kh * kh, axis=-1, keepdims=True) + NORM_EPS)
        qs.append(qh)
        ks.append(kh)
        r_scr[h:h + 1, :] = qh
        r_scr[DN_QK_HEADS + h:DN_QK_HEADS + h + 1, :] = kh
    rt = r_scr[...].T

    bg = bg_ref[...]
    nw = nw_ref[...]
    rep = DN_V_HEADS // DN_QK_HEADS
    for h in range(DN_V_HEADS):
        hq, x_in = h // rep, h % rep
        beta = bg[:, SUBLANES * hq + x_in:SUBLANES * hq + x_in + 1]
        g = bg[:, SUBLANES * hq + 2 + x_in:SUBLANES * hq + 3 + x_in]
        eg = jnp.exp(g)
        q_col = rt[:, hq:hq + 1]
        k_col = rt[:, DN_QK_HEADS + hq:DN_QK_HEADS + hq + 1]
        s_old = s_ref[h]
        k_s = jnp.sum(k_col * s_old, axis=0, keepdims=True)
        q_s = jnp.sum(q_col * s_old, axis=0, keepdims=True)
        v_h = a[:, 2 * DN_KEY_DIM + h * hd:2 * DN_KEY_DIM + (h + 1) * hd]
        u = beta * v_h - (beta * eg) * k_s
        qk = jnp.sum(qs[hq] * ks[hq], axis=-1, keepdims=True)
        o = eg * q_s + qk * u
        s_out_ref[h] = eg * s_old + k_col * u
        z_h = p_ref[:, DN_CONV_DIM + h * hd:DN_CONV_DIM + (h + 1) * hd]
        on = o * lax.rsqrt(jnp.mean(o * o, axis=-1, keepdims=True) + NORM_EPS) * nw * _silu(z_h)
        o_ref[:, h * hd:(h + 1) * hd] = on.astype(o_ref.dtype)


def _dn_sample(proj, bg, conv_state, conv_w, state, norm_w):
    bd = proj.shape[0]
    hd = HEAD_DIM
    o, s_new, c_new = pl.pallas_call(
        _dn_sample_kernel,
        grid=(bd,),
        in_specs=[
            pl.BlockSpec((None, 1, DN_MAIN_DIM), lambda b: (b, 0, 0)),
            pl.BlockSpec((None, 1, LANES), lambda b: (b, 0, 0)),
            pl.BlockSpec((None, DN_CONV - 1, DN_CONV_DIM), lambda b: (b, 0, 0)),
            pl.BlockSpec((DN_CONV, DN_CONV_DIM), lambda b: (0, 0)),
            pl.BlockSpec((None, DN_V_HEADS, hd, hd), lambda b: (b, 0, 0, 0)),
            pl.BlockSpec((1, hd), lambda b: (0, 0)),
        ],
        out_specs=[
            pl.BlockSpec((None, 1, DN_VAL_DIM), lambda b: (b, 0, 0)),
            pl.BlockSpec((None, DN_V_HEADS, hd, hd), lambda b: (b, 0, 0, 0)),
            pl.BlockSpec((None, DN_CONV - 1, DN_CONV_DIM), lambda b: (b, 0, 0)),
        ],
        out_shape=[jax.ShapeDtypeStruct((bd, 1, DN_VAL_DIM), BF16),
                   jax.ShapeDtypeStruct(state.shape, F32),
                   jax.ShapeDtypeStruct(conv_state.shape, F32)],
        scratch_shapes=[pltpu.VMEM((LANES, LANES), F32)],
        compiler_params=_cparams("parallel"),
        name="dn_sample",
    )(proj.reshape(bd, 1, DN_MAIN_DIM), bg.reshape(bd, 1, LANES), conv_state, conv_w, state, norm_w)
    return o.reshape(bd, DN_VAL_DIM), s_new, c_new


def _matmul_residual_kernel(a_ref, w_ref, y_ref, o_ref):
    o_ref[...] = y_ref[...] + _dot(a_ref[...], w_ref[...])


def _matmul_residual(a, w, y):
    m, k = a.shape
    n = w.shape[1]
    tm = min(512, m)
    return pl.pallas_call(
        _matmul_residual_kernel,
        grid=(m // tm,),
        in_specs=[pl.BlockSpec((tm, k), lambda i: (i, 0)),
                  pl.BlockSpec((k, n), lambda i: (0, 0)),
                  pl.BlockSpec((tm, n), lambda i: (i, 0))],
        out_specs=pl.BlockSpec((tm, n), lambda i: (i, 0)),
        out_shape=jax.ShapeDtypeStruct((m, n), F32),
        compiler_params=_cparams("parallel"),
        name="matmul_residual",
    )(a, w, y)


def _mb_qkv_kernel(x_ref, g_ref, w_ref, cos_ref, sin_ref, q_ref, k_ref, v_ref, *rest,
                   with_blocks, blocks_per_seq):
    if with_blocks:
        kb_ref, vt_ref, km_ref, h_scr = rest
    else:
        (h_scr,) = rest
    i, j = pl.program_id(0), pl.program_id(1)
    hd = HEAD_DIM

    @pl.when(j == 0)
    def _():
        h_scr[...] = _rms_rows(x_ref[...], g_ref[...]).astype(BF16)

    y = _dot(h_scr[...], w_ref[...])
    tm = y.shape[0]

    def rope(val):
        cos, sin = cos_ref[...], sin_ref[...]
        parts = []
        for h in range(MB_HEADS):
            yh = val[:, h * hd:(h + 1) * hd]
            parts.append(yh * cos + pltpu.roll(yh, hd // 2, axis=1) * sin)
        return jnp.concatenate(parts, axis=1)

    @pl.when(j == 0)
    def _():
        q_ref[...] = rope(y).astype(q_ref.dtype)

    @pl.when(j == 1)
    def _():
        k = rope(y)
        k_ref[...] = k
        if with_blocks:
            row = lax.broadcasted_iota(I32, (tm, LANES), 0)
            lane = lax.broadcasted_iota(I32, (tm, LANES), 1)
            first_blk = (i * (tm // MB_BLOCK)) % blocks_per_seq
            onehot = jnp.where(lane == first_blk + row // MB_BLOCK, 1.0, 0.0).astype(BF16)
            for h in range(MB_HEADS):
                kb_ref[:, 2 * h * hd:(2 * h + 1) * hd] = k[:, h * hd:(h + 1) * hd].astype(BF16)
                kb_ref[:, (2 * h + 1) * hd:(2 * h + 2) * hd] = onehot
            for r in range(tm // MB_BLOCK):
                blk = k[r * MB_BLOCK:(r + 1) * MB_BLOCK, :]
                km_ref[r] = jnp.sum(blk, axis=0, keepdims=True) * (1.0 / MB_BLOCK)

    @pl.when(j == 2)
    def _():
        v_ref[...] = y
        if with_blocks:
            ones = jnp.ones((MB_V_PAD, MB_BLOCK), BF16)
            rows = hd + MB_V_PAD
            for r in range(tm // MB_BLOCK):
                vt = y[r * MB_BLOCK:(r + 1) * MB_BLOCK, :].T.astype(BF16)
                for h in range(MB_HEADS):
                    vt_ref[r, h * rows:h * rows + hd, :] = vt[h * hd:(h + 1) * hd, :]
                    vt_ref[r, h * rows + hd:(h + 1) * rows, :] = ones


def _mb_qkv(x, g, w, cos, sin, *, with_blocks):
    m, d = x.shape
    tm = min(512, m)
    n_pos_tiles = cos.shape[0] // tm
    out_shape = [jax.ShapeDtypeStruct((m, d), BF16 if with_blocks else F32),
                 jax.ShapeDtypeStruct((m, d), F32), jax.ShapeDtypeStruct((m, d), F32)]
    row = pl.BlockSpec((tm, d), lambda i, j: (i, 0))
    out_specs = [row, row, row]
    if with_blocks:
        nb = tm // MB_BLOCK
        vt_rows = MB_HEADS * (HEAD_DIM + MB_V_PAD)
        out_shape += [jax.ShapeDtypeStruct((m, 2 * d), BF16),
                      jax.ShapeDtypeStruct((m // MB_BLOCK, vt_rows, MB_BLOCK), BF16),
                      jax.ShapeDtypeStruct((m // MB_BLOCK, 1, d), F32)]
        out_specs += [pl.BlockSpec((tm, 2 * d), lambda i, j: (i, 0)),
                      pl.BlockSpec((nb, vt_rows, MB_BLOCK), lambda i, j: (i, 0, 0)),
                      pl.BlockSpec((nb, 1, d), lambda i, j: (i, 0, 0))]
    return pl.pallas_call(
        functools.partial(_mb_qkv_kernel, with_blocks=with_blocks,
                          blocks_per_seq=cos.shape[0] // MB_BLOCK if with_blocks else 0),
        grid=(m // tm, 3),
        in_specs=[row,
                  pl.BlockSpec((1, d), lambda i, j: (0, 0)),
                  pl.BlockSpec((d, d), lambda i, j: (0, j)),
                  pl.BlockSpec((tm, HEAD_DIM), lambda i, j: (i % n_pos_tiles, 0)),
                  pl.BlockSpec((tm, HEAD_DIM), lambda i, j: (i % n_pos_tiles, 0))],
        out_specs=out_specs,
        out_shape=out_shape,
        scratch_shapes=[pltpu.VMEM((tm, d), BF16)],
        compiler_params=_cparams("parallel", "arbitrary"),
        name="mb_qkv",
    )(x, g, w, cos, sin)


def _mb_attn_kernel(q_ref, k_ref, vt_ref, km_ref, o_ref, *logit_bufs, heads):
    n = pl.program_id(2)
    bq, hd = MB_BLOCK, HEAD_DIM
    vrows = hd + MB_V_PAD
    c = (hd ** -0.5) * 1.4426950408889634
    nb = km_ref.shape[0]
    blk = lax.broadcasted_iota(I32, (nb, bq), 0)
    kpos = lax.broadcasted_iota(I32, (bq, bq), 0)
    qpos = lax.broadcasted_iota(I32, (bq, bq), 1)
    own_rows = pl.ds(pl.multiple_of(n * bq, bq), bq)

    qas, carry = [], []
    for h in range(heads):
        q = q_ref[:, h * hd:(h + 1) * hd]
        km = km_ref[:, h * hd:(h + 1) * hd].astype(BF16)
        s = jnp.where(blk < n, _dot_nt(km, q), -jnp.inf)
        bias = jnp.full((nb, bq), NEG_BIG, F32)
        for _ in range(MB_TOP_K):
            top = jnp.max(s, axis=0, keepdims=True)
            idx = jnp.min(jnp.where(s == top, blk, nb), axis=0, keepdims=True)
            hit = blk == idx
            bias = jnp.where(hit & (idx < n), 0.0, bias)
            s = jnp.where(hit, -jnp.inf, s)
        bias = jnp.concatenate([bias, jnp.full((LANES - nb, bq), NEG_BIG, F32)], axis=0)
        qas.append(jnp.concatenate([q, bias.T.astype(BF16)], axis=1))

        k_own = k_ref[own_rows, 2 * h * hd:(2 * h + 1) * hd]
        st = jnp.where(kpos <= qpos, _dot_nt(k_own, q), NEG_BIG)
        m = jnp.max(st, axis=0, keepdims=True)
        p = jnp.exp2((st - m) * c)
        carry += [m, _dot(vt_ref[n, h * vrows:(h + 1) * vrows, :], p.astype(BF16))]

    sets_x = [logit_bufs[8 * h:8 * h + 4] for h in range(heads)]
    sets_y = [logit_bufs[8 * h + 4:8 * h + 8] for h in range(heads)]

    def fill(sets, j0):
        for h, bufs in enumerate(sets):
            for i, buf in enumerate(bufs):
                j = jnp.minimum(j0 + i, nb - 1)
                rows = pl.ds(pl.multiple_of(j * bq, bq), bq)
                buf[...] = _dot_nt(k_ref[rows, 2 * h * hd:(2 * h + 2) * hd], qas[h])

    def consume(sets, j0, carry):
        out = []
        for h, bufs in enumerate(sets):
            m, acc = carry[2 * h], carry[2 * h + 1]
            m_new = m
            for buf in bufs:
                m_new = jnp.maximum(m_new, jnp.max(buf[...], axis=0, keepdims=True))
            p = jnp.concatenate([jnp.exp2((buf[...] - m_new) * c).astype(BF16) for buf in bufs], axis=0)
            vt = jnp.concatenate([vt_ref[j0 + i, h * vrows:(h + 1) * vrows, :]
                                  for i in range(len(bufs))], axis=1)
            out += [m_new, acc * jnp.exp2((m - m_new) * c) + _dot(vt, p)]
        return tuple(out)

    def body(t, carry):
        j = 8 * t
        fill(sets_y, j + 4)
        carry = consume(sets_x, j, carry)
        fill(sets_x, j + 8)
        return consume(sets_y, j + 4, carry)

    half_trips = (n + 3) // 4
    fill(sets_x, 0)
    carry = lax.fori_loop(0, half_trips // 2, body, tuple(carry))
    carry = lax.cond(half_trips % 2 == 1,
                     lambda: consume(sets_x, 8 * (half_trips // 2), carry),
                     lambda: carry)
    for h in range(heads):
        acc = carry[2 * h + 1]
        o_ref[:, h * hd:(h + 1) * hd] = (acc[0:hd, :] / acc[hd:hd + 1, :]).T.astype(o_ref.dtype)


def _mb_attn(q, kb, vt, km, bsz, seq):
    hd, bq, heads = HEAD_DIM, MB_BLOCK, MB_HEADS_PER_STEP
    nb = seq // bq
    d = MB_HEADS * hd
    assert nb % 4 == 0
    return pl.pallas_call(
        functools.partial(_mb_attn_kernel, heads=heads),
        grid=(bsz, MB_HEADS // heads, nb),
        in_specs=[
            pl.BlockSpec((bq, heads * hd), lambda b, h, n: (b * nb + n, h)),
            pl.BlockSpec((seq, 2 * heads * hd), lambda b, h, n: (b, h)),
            pl.BlockSpec((nb, heads * (hd + MB_V_PAD), bq), lambda b, h, n: (b, h, 0)),
            pl.BlockSpec((nb, None, heads * hd), lambda b, h, n: (b, 0, h)),
        ],
        out_specs=pl.BlockSpec((bq, heads * hd), lambda b, h, n: (b * nb + n, h)),
        out_shape=jax.ShapeDtypeStruct((bsz * seq, d), BF16),
        scratch_shapes=[pltpu.VMEM((bq, bq), F32)] * (8 * heads),
        compiler_params=_cparams("parallel", "parallel", "arbitrary"),
        name="mb_attn",
    )(q, kb, vt, km)


PAGES_PER_STEP = 16


def _page_sum_kernel(pt_ref, *refs):
    pages, o_ref = refs[:PAGES_PER_STEP], refs[PAGES_PER_STEP]
    per_blk = MB_BLOCK // PAGE_SIZE
    for r in range(PAGES_PER_STEP // per_blk):
        s = jnp.sum(pages[per_blk * r][...], axis=0)
        for i in range(1, per_blk):
            s = s + jnp.sum(pages[per_blk * r + i][...], axis=0)
        o_ref[r] = s


def _page_block_sums(pool, page_table, layer):
    bd, n_pages = page_table.shape
    heads, hd = pool.shape[-2:]
    steps = n_pages // PAGES_PER_STEP
    rows = PAGES_PER_STEP * PAGE_SIZE // MB_BLOCK

    def page_spec(i):
        return pl.BlockSpec(
            (None, None, PAGE_SIZE, heads, hd),
            lambda b, s, pt: (layer, pt[b * n_pages + s * PAGES_PER_STEP + i], 0, 0, 0))

    return pl.pallas_call(
        _page_sum_kernel,
        grid_spec=pltpu.PrefetchScalarGridSpec(
            num_scalar_prefetch=1,
            grid=(bd, steps),
            in_specs=[page_spec(i) for i in range(PAGES_PER_STEP)],
            out_specs=pl.BlockSpec((None, rows, heads, hd), lambda b, s, pt: (b, s, 0, 0)),
        ),
        out_shape=jax.ShapeDtypeStruct((bd, steps * rows, heads, hd), F32),
        compiler_params=_cparams("parallel", "arbitrary"),
        name="mb_page_sums",
    )(page_table.reshape(-1), *([pool] * PAGES_PER_STEP))


def _mb_sample_select_kernel(q_ref, kn_ref, bs_ref, sel_ref, *, own):
    nb = bs_ref.shape[0]
    n_rows = nb + SUBLANES
    inv_blk = 1.0 / MB_BLOCK
    q = q_ref[...]
    sc = jnp.sum(bs_ref[...] * inv_blk * q[None], axis=-1, keepdims=True)
    own_sc = jnp.sum(kn_ref[...] * inv_blk * q, axis=-1, keepdims=True)
    tail_row = lax.broadcasted_iota(I32, (SUBLANES, MB_HEADS, 1), 0)
    tail = jnp.where(tail_row == 0, own_sc[None], -jnp.inf)
    blk = lax.broadcasted_iota(I32, (n_rows, MB_HEADS, 1), 0)
    s = jnp.where(blk < own, jnp.concatenate([sc, tail], axis=0), -jnp.inf)
    out_lane = lax.broadcasted_iota(I32, (MB_HEADS, LANES), 1)
    out = jnp.zeros((MB_HEADS, LANES), I32)
    for r in range(MB_TOP_K):
        m = jnp.max(s, axis=0, keepdims=True)
        idx = jnp.min(jnp.where(s == m, blk, n_rows), axis=0, keepdims=True)
        s = jnp.where(blk == idx, -jnp.inf, s)
        idx = idx[0]
        out = jnp.where(out_lane == r, idx, out)
        out = jnp.where(out_lane == MB_TOP_K + 1 + r, (idx < own).astype(I32), out)
    sel_ref[...] = out


def _mb_sample_select(q, k_new, blk_sums, own):
    bd = q.shape[0]
    nb = blk_sums.shape[1]
    return pl.pallas_call(
        functools.partial(_mb_sample_select_kernel, own=own),
        grid=(bd,),
        in_specs=[pl.BlockSpec((None, MB_HEADS, HEAD_DIM), lambda b: (b, 0, 0)),
                  pl.BlockSpec((None, MB_HEADS, HEAD_DIM), lambda b: (b, 0, 0)),
                  pl.BlockSpec((None, nb, MB_HEADS, HEAD_DIM), lambda b: (b, 0, 0, 0))],
        out_specs=pl.BlockSpec((None, MB_HEADS, LANES), lambda b: (b, 0, 0)),
        out_shape=jax.ShapeDtypeStruct((bd, MB_HEADS, LANES), I32),
        compiler_params=_cparams("parallel"),
        name="mb_sample_select",
    )(q, k_new, blk_sums)


def _mb_sample_attn_kernel(pt_ref, sel_ref, q_ref, kn_ref, vn_ref, *refs):
    n_pg = MB_TOP_K * (MB_BLOCK // PAGE_SIZE)
    k_pages, v_pages, o_ref = refs[:n_pg], refs[n_pg:2 * n_pg], refs[2 * n_pg]
    b, h = pl.program_id(0), pl.program_id(1)
    scale = HEAD_DIM ** -0.5
    rows_pg = PAGE_SIZE * MB_HEADS
    q = q_ref[...]
    q8 = jnp.broadcast_to(q, (SUBLANES, HEAD_DIM)).astype(BF16)
    flat = lambda r: r[...].reshape(rows_pg, HEAD_DIM).astype(BF16)
    keys = jnp.concatenate([flat(r) for r in k_pages], axis=0)
    vals = jnp.concatenate([flat(r) for r in v_pages], axis=0)
    lg = _dot_nt(q8, keys) * scale
    lane = lax.broadcasted_iota(I32, lg.shape, 1)
    lg = jnp.where((lane & (MB_HEADS - 1)) == h, lg, NEG_BIG)
    base = (b * MB_HEADS + h) * SUBLANES
    blk_cols = rows_pg * (MB_BLOCK // PAGE_SIZE)
    for r in range(MB_TOP_K):
        penalty = jnp.where(sel_ref[base + MB_TOP_K + 1 + r] == 0, NEG_BIG, 0.0)
        in_blk = (lane >= r * blk_cols) & (lane < (r + 1) * blk_cols)
        lg = jnp.where(in_blk, lg + penalty, lg)
    qb = q.astype(BF16).astype(F32)
    lo = jnp.sum(qb * kn_ref[...].astype(BF16).astype(F32), axis=-1, keepdims=True) * scale
    m = jnp.maximum(jnp.max(lg, axis=-1, keepdims=True), lo)
    p = jnp.exp(lg - m)
    po = jnp.exp(lo - m)
    l = jnp.sum(p, axis=-1, keepdims=True) + po
    vn = vn_ref[...].astype(BF16).astype(F32)
    out = (_dot(p.astype(BF16), vals) + po.astype(BF16).astype(F32) * vn) / l
    o_ref[...] = out[0:1, :]


def _mb_sample_attn(pool_k, pool_v, page_table, sel, q, k_new, v_new, layer):
    bd, n_pages = page_table.shape
    hd = HEAD_DIM
    per_blk = MB_BLOCK // PAGE_SIZE

    def page_spec(r, half):
        def imap(b, h, pt, sl):
            blk = sl[(b * MB_HEADS + h) * SUBLANES + r]
            return (layer, pt[b * n_pages + blk * per_blk + half], 0, 0, 0)
        return pl.BlockSpec((None, None, PAGE_SIZE, MB_HEADS, hd), imap)

    pages = [page_spec(r, half) for r in range(MB_TOP_K) for half in range(per_blk)]
    vec = pl.BlockSpec((None, None, 1, hd), lambda b, h, pt, sl: (b, h, 0, 0))
    q4 = q.reshape(bd, MB_HEADS, 1, hd)
    out = pl.pallas_call(
        _mb_sample_attn_kernel,
        grid_spec=pltpu.PrefetchScalarGridSpec(
            num_scalar_prefetch=2,
            grid=(bd, MB_HEADS),
            in_specs=[vec, vec, vec] + pages + pages,
            out_specs=vec,
        ),
        out_shape=jax.ShapeDtypeStruct((bd, MB_HEADS, 1, hd), F32),
        compiler_params=_cparams("parallel", "arbitrary"),
        name="mb_sample_attn",
    )(page_table.reshape(-1), sel[:, :, :SUBLANES].reshape(-1), q4, k_new.reshape(q4.shape), v_new.reshape(q4.shape),
      *([pool_k] * len(pages)), *([pool_v] * len(pages)))
    return out.reshape(bd, MB_HEADS * hd)


def _ffn_kernel(y_ref, g_ref, wg_ref, wv_ref, cwg_ref, cwv_ref, cbg_ref, cbv_ref, wo_ref,
                out_ref, bufg_ref, bufv_ref, h_scr, ug_scr, uv_scr, cg_scr, cv_scr, acc_scr,
                *, tiles_per_seq):
    i, f = pl.program_id(0), pl.program_id(1)
    tm = y_ref.shape[0]
    first = (i % tiles_per_seq) == 0

    @pl.when(f == 0)
    def _():
        h_scr[...] = _rms_rows(y_ref[...], g_ref[...]).astype(BF16)
        acc_scr[...] = jnp.zeros(acc_scr.shape, F32)

    @pl.when(first)
    def _():
        cg_scr[f] = jnp.zeros(cg_scr.shape[1:], F32)
        cv_scr[f] = jnp.zeros(cv_scr.shape[1:], F32)

    def branch(w_ref, cw_ref, cb_ref, u_scr, carry, buf_ref):
        u = _dot(h_scr[...], w_ref[...])
        u_scr[0:SUBLANES, :] = carry[f]
        u_scr[SUBLANES:SUBLANES + tm, :] = u
        r = cw_ref[FFN_CONV - 1:FFN_CONV, :] * u + cb_ref[...]
        for d in range(1, FFN_CONV):
            r = r + cw_ref[FFN_CONV - 1 - d:FFN_CONV - d, :] * u_scr[SUBLANES - d:SUBLANES - d + tm, :]
        last = u_scr[tm:tm + SUBLANES, :]
        carry[f] = last
        buf_ref[...] = last
        return r

    gate = branch(wg_ref, cwg_ref, cbg_ref, ug_scr, cg_scr, bufg_ref)
    val = branch(wv_ref, cwv_ref, cbv_ref, uv_scr, cv_scr, bufv_ref)
    act = (_silu(gate) * val).astype(BF16)
    acc_scr[...] += _dot(act, wo_ref[...])

    @pl.when(f == pl.num_programs(1) - 1)
    def _():
        out_ref[...] = y_ref[...] + acc_scr[...]


def _ffn_tile(dff):
    for cand in (1408, 1024, 512, 256, 128):
        if dff % cand == 0:
            return cand
    raise ValueError(f"unsupported d_ff {dff}")


def _ffn_prompt(y, g, w_in, conv_w, conv_b, w_out, bsz, seq):
    m, d = y.shape
    dff = w_out.shape[0]
    tm = min(512, seq)
    tf = _ffn_tile(dff)
    nf = dff // tf
    tiles_per_seq = seq // tm
    gate = lambda i, f: (0, f)
    val = lambda i, f: (0, f + nf)
    buf_spec = pl.BlockSpec((None, SUBLANES, tf), lambda i, f: (i, 0, f))
    out, bufg, bufv = pl.pallas_call(
        functools.partial(_ffn_kernel, tiles_per_seq=tiles_per_seq),
        grid=(m // tm, nf),
        in_specs=[pl.BlockSpec((tm, d), lambda i, f: (i, 0)),
                  pl.BlockSpec((1, d), lambda i, f: (0, 0)),
                  pl.BlockSpec((d, tf), gate), pl.BlockSpec((d, tf), val),
                  pl.BlockSpec((FFN_CONV, tf), gate), pl.BlockSpec((FFN_CONV, tf), val),
                  pl.BlockSpec((1, tf), gate), pl.BlockSpec((1, tf), val),
                  pl.BlockSpec((tf, d), lambda i, f: (f, 0))],
        out_specs=[pl.BlockSpec((tm, d), lambda i, f: (i, 0)), buf_spec, buf_spec],
        out_shape=[jax.ShapeDtypeStruct((m, d), F32),
                   jax.ShapeDtypeStruct((m // tm, SUBLANES, dff), F32),
                   jax.ShapeDtypeStruct((m // tm, SUBLANES, dff), F32)],
        scratch_shapes=[pltpu.VMEM((tm, d), BF16),
                        pltpu.VMEM((tm + SUBLANES, tf), F32), pltpu.VMEM((tm + SUBLANES, tf), F32),
                        pltpu.VMEM((nf, SUBLANES, tf), F32), pltpu.VMEM((nf, SUBLANES, tf), F32),
                        pltpu.VMEM((tm, d), F32)],
        compiler_params=_cparams("arbitrary", "arbitrary"),
        name="ffn_prompt",
    )(y, g, w_in, w_in, conv_w, conv_w, conv_b, conv_b, w_out)
    keep = FFN_CONV - 1
    last = slice(tiles_per_seq - 1, None, tiles_per_seq)
    new_buf = jnp.concatenate([bufg[last, SUBLANES - keep:], bufv[last, SUBLANES - keep:]], axis=-1)
    return out, new_buf


def _ffn_sample_kernel(y_ref, g_ref, wg_ref, wv_ref, cwg_ref, cwv_ref, cbg_ref, cbv_ref,
                       sg_ref, sv_ref, wo_ref, out_ref, ng_ref, nv_ref, h_scr, acc_scr):
    f = pl.program_id(0)

    @pl.when(f == 0)
    def _():
        h_scr[...] = _rms_rows(y_ref[...], g_ref[...]).astype(BF16)
        acc_scr[...] = jnp.zeros(acc_scr.shape, F32)

    def branch(w_ref, cw_ref, cb_ref, st_ref, new_ref):
        u = _dot(h_scr[...], w_ref[...])
        r = cw_ref[FFN_CONV - 1:FFN_CONV, :] * u + cb_ref[...]
        for i in range(FFN_CONV - 1):
            r = r + cw_ref[i:i + 1, :] * st_ref[i]
        for i in range(FFN_CONV - 2):
            new_ref[i] = st_ref[i + 1]
        new_ref[FFN_CONV - 2] = u
        return r

    gate = branch(wg_ref, cwg_ref, cbg_ref, sg_ref, ng_ref)
    val = branch(wv_ref, cwv_ref, cbv_ref, sv_ref, nv_ref)
    acc_scr[...] += _dot((_silu(gate) * val).astype(BF16), wo_ref[...])

    @pl.when(f == pl.num_programs(0) - 1)
    def _():
        out_ref[...] = y_ref[...] + acc_scr[...]


def _ffn_sample(y, g, w_in, conv_w, conv_b, w_out, state):
    bd, d = y.shape
    dff = w_out.shape[0]
    tf = _ffn_tile(dff)
    nf = dff // tf
    keep = FFN_CONV - 1
    st = jnp.transpose(state, (1, 0, 2))
    gate = lambda f: (0, f)
    val = lambda f: (0, f + nf)
    st_gate = pl.BlockSpec((keep, bd, tf), lambda f: (0, 0, f))
    st_val = pl.BlockSpec((keep, bd, tf), lambda f: (0, 0, f + nf))
    new_spec = pl.BlockSpec((keep, bd, tf), lambda f: (0, 0, f))
    out, ng, nv = pl.pallas_call(
        _ffn_sample_kernel,
        grid=(nf,),
        in_specs=[pl.BlockSpec((bd, d), lambda f: (0, 0)),
                  pl.BlockSpec((1, d), lambda f: (0, 0)),
                  pl.BlockSpec((d, tf), gate), pl.BlockSpec((d, tf), val),
                  pl.BlockSpec((FFN_CONV, tf), gate), pl.BlockSpec((FFN_CONV, tf), val),
                  pl.BlockSpec((1, tf), gate), pl.BlockSpec((1, tf), val),
                  st_gate, st_val,
                  pl.BlockSpec((tf, d), lambda f: (f, 0))],
        out_specs=[pl.BlockSpec((bd, d), lambda f: (0, 0)), new_spec, new_spec],
        out_shape=[jax.ShapeDtypeStruct((bd, d), F32),
                   jax.ShapeDtypeStruct((keep, bd, dff), F32),
                   jax.ShapeDtypeStruct((keep, bd, dff), F32)],
        scratch_shapes=[pltpu.VMEM((bd, d), BF16), pltpu.VMEM((bd, d), F32)],
        compiler_params=_cparams("arbitrary"),
        name="ffn_sample",
    )(y, g, w_in, w_in, conv_w, conv_w, conv_b, conv_b, st, st, w_out)
    new_state = jnp.transpose(jnp.concatenate([ng, nv], axis=-1), (1, 0, 2))
    return out, new_state


def _final_norm_kernel(x_ref, g_ref, o_ref):
    o_ref[...] = _rms_rows(x_ref[...], g_ref[...])


def _final_norm(x, g):
    m, d = x.shape
    tm = min(1024, m)
    return pl.pallas_call(
        _final_norm_kernel,
        grid=(m // tm,),
        in_specs=[pl.BlockSpec((tm, d), lambda i: (i, 0)), pl.BlockSpec((1, d), lambda i: (0, 0))],
        out_specs=pl.BlockSpec((tm, d), lambda i: (i, 0)),
        out_shape=jax.ShapeDtypeStruct((m, d), F32),
        compiler_params=_cparams("parallel"),
        name="final_norm",
    )(x, g)


def _rope_tables(pos):
    half = HEAD_DIM // 2
    inv = ROPE_THETA ** (-jnp.arange(half, dtype=F32) / half)
    ang = pos.astype(F32)[:, None] * inv[None, :]
    cos, sin = jnp.cos(ang), jnp.sin(ang)
    return jnp.concatenate([cos, cos], axis=-1), jnp.concatenate([-sin, sin], axis=-1)


def _dn_gate_layout(w_in, a_log, dt_bias):
    d = w_in.shape[0]
    rep = DN_V_HEADS // DN_QK_HEADS
    w_beta = w_in[:, DN_MAIN_DIM:DN_MAIN_DIM + DN_V_HEADS].reshape(d, DN_QK_HEADS, rep)
    w_dec = w_in[:, DN_MAIN_DIM + DN_V_HEADS:].reshape(d, DN_QK_HEADS, rep)
    pad = jnp.zeros((d, DN_QK_HEADS, SUBLANES - 2 * rep), w_in.dtype)
    w_bg = jnp.concatenate([w_beta, w_dec, pad], axis=-1).reshape(d, DN_QK_HEADS * SUBLANES)
    w_bg = jnp.pad(w_bg, ((0, 0), (0, LANES - w_bg.shape[1])))

    def lanes(per_head, fill):
        v = per_head.reshape(DN_QK_HEADS, rep)
        z = jnp.full((DN_QK_HEADS, rep), fill, F32)
        p = jnp.full((DN_QK_HEADS, SUBLANES - 2 * rep), fill, F32)
        return v, z, p

    a, zero, padz = lanes(-jnp.exp(a_log.astype(F32)), 0.0)
    neg_a = jnp.concatenate([zero, a, padz], axis=-1).reshape(-1)
    dt, _, _ = lanes(dt_bias.astype(F32), 0.0)
    dt_b = jnp.concatenate([zero, dt, padz], axis=-1).reshape(-1)
    is_beta = jnp.concatenate([zero + 1.0, zero, padz], axis=-1).reshape(-1)
    par = jnp.stack([neg_a, dt_b, is_beta])
    par = jnp.pad(par, ((0, SUBLANES - par.shape[0]), (0, LANES - par.shape[1])))
    return w_bg.astype(BF16), par


def kernel(x_prompt, x_sample, cache_k, cache_v, state_dn, state_dn_conv, state_ffn_conv, page_table,
           norm_mix, norm_ffn, norm_out, dn_w_in, dn_conv_w, dn_a_log, dn_dt_bias, dn_norm, dn_w_out,
           mb_w_qkv, mb_w_o, ffn_w_in, ffn_conv_w, ffn_conv_b, ffn_w_out):
    bsz, seq, d = x_prompt.shape
    bd = x_sample.shape[0]
    depth = norm_mix.shape[0]
    n_pages = page_table.shape[1]
    past = n_pages * PAGE_SIZE
    own_blk = past // MB_BLOCK
    hd = HEAD_DIM

    yp = x_prompt.reshape(bsz * seq, d)
    ys = x_sample.reshape(bd, d)
    cos_p, sin_p = _rope_tables(jnp.arange(seq, dtype=I32))
    cos_s, sin_s = _rope_tables(jnp.full((bd,), past, dtype=I32))
    pool_k, pool_v = cache_k, cache_v

    kp_l, vp_l, ks_l, vs_l = [], [], [], []
    sp_l, ss_l, cp_l, cs_l = [], [], [], []
    fp_l, fs_l = [], []
    for layer in range(depth):
        j = layer // 2
        g_mix = norm_mix[layer].reshape(1, d)
        if layer % 2 == 0:
            w_main = dn_w_in[j][:, :DN_MAIN_DIM].astype(BF16)
            w_bg, par = _dn_gate_layout(dn_w_in[j], dn_a_log[j], dn_dt_bias[j])
            w_out = dn_w_out[j].astype(BF16)
            nw = dn_norm[j].reshape(1, hd)
            proj, bg, bgt = _dn_inproj(yp, g_mix, w_main, w_bg, par, transposed=True)
            o, s_new = _dn_core(proj, bg, bgt, dn_conv_w[j], nw, bsz, seq)
            yp = _matmul_residual(o, w_out, yp)
            sp_l.append(s_new)
            cp_l.append(proj.reshape(bsz, seq, DN_MAIN_DIM)[:, seq - (DN_CONV - 1):, :DN_CONV_DIM])

            proj_s, bg_s = _dn_inproj(ys, g_mix, w_main, w_bg, par, transposed=False)
            o_s, s_s, c_s = _dn_sample(proj_s, bg_s, state_dn_conv[j], dn_conv_w[j], state_dn[j], nw)
            ys = _matmul_residual(o_s, w_out, ys)
            ss_l.append(s_s)
            cs_l.append(c_s)
        else:
            w_qkv = mb_w_qkv[j].astype(BF16)
            w_o = mb_w_o[j].astype(BF16)
            q, k, v, kb, vt, km = _mb_qkv(yp, g_mix, w_qkv, cos_p, sin_p, with_blocks=True)
            att = _mb_attn(q, kb, vt, km, bsz, seq)
            yp = _matmul_residual(att, w_o, yp)
            kp_l.append(k.reshape(bsz, seq, MB_HEADS, hd))
            vp_l.append(v.reshape(bsz, seq, MB_HEADS, hd))

            q_s, k_s, v_s = _mb_qkv(ys, g_mix, w_qkv, cos_s, sin_s, with_blocks=False)
            blk_sums = _page_block_sums(pool_k, page_table, j)
            q3 = q_s.reshape(bd, MB_HEADS, hd)
            k3 = k_s.reshape(bd, MB_HEADS, hd)
            sel = _mb_sample_select(q3, k3, blk_sums, own_blk)
            att_s = _mb_sample_attn(pool_k, pool_v, page_table, sel, q3, k3, v_s.reshape(bd, MB_HEADS, hd), j)
            ys = _matmul_residual(att_s.astype(BF16), w_o, ys)
            ks_l.append(k_s.reshape(bd, 1, MB_HEADS, hd))
            vs_l.append(v_s.reshape(bd, 1, MB_HEADS, hd))

        g_ffn = norm_ffn[layer].reshape(1, d)
        w_fi = ffn_w_in[layer].astype(BF16)
        w_fo = ffn_w_out[layer].astype(BF16)
        cb = ffn_conv_b[layer].reshape(1, -1)
        yp, fbp = _ffn_prompt(yp, g_ffn, w_fi, ffn_conv_w[layer], cb, w_fo, bsz, seq)
        ys, fbs = _ffn_sample(ys, g_ffn, w_fi, ffn_conv_w[layer], cb, w_fo, state_ffn_conv[layer])
        fp_l.append(fbp)
        fs_l.append(fbs)

    g_out = norm_out.reshape(1, d)
    y_prompt = _final_norm(yp, g_out).reshape(bsz, seq, d)
    y_sample = _final_norm(ys, g_out).reshape(bd, 1, d)
    return (y_prompt, y_sample, jnp.stack(kp_l), jnp.stack(vp_l), jnp.stack(ks_l), jnp.stack(vs_l),
            jnp.stack(sp_l), jnp.stack(ss_l), jnp.stack(cp_l), jnp.stack(cs_l), jnp.stack(fp_l), jnp.stack(fs_l))
```

```python
import functools

import jax
import numpy as np
import jax.numpy as jnp
from jax import lax
from jax.experimental import pallas as pl
from jax.experimental.pallas import tpu as pltpu

F32 = jnp.float32
BF16 = jnp.bfloat16
I32 = jnp.int32

NORM_EPS = 1e-6
ROPE_THETA = 10000.0
HEAD_DIM = 128
DN_QK_HEADS = 8
DN_V_HEADS = 16
DN_CONV = 4
DN_KEY_DIM = DN_QK_HEADS * HEAD_DIM
DN_VAL_DIM = DN_V_HEADS * HEAD_DIM
DN_CONV_DIM = 2 * DN_KEY_DIM + DN_VAL_DIM
DN_MAIN_DIM = DN_CONV_DIM + DN_VAL_DIM
DN_CHUNK = 64
DN_TBLOCK = 128
DN_HEADS_PER_STEP = 8
MB_HEADS = 8
MB_BLOCK = 256
MB_TOP_K = 3
MB_V_PAD = 16
MB_HEADS_PER_STEP = 2
PAGE_SIZE = 128
FFN_CONV = 3
SUBLANES = 8
LANES = 128
NEG_BIG = -1e30
VMEM_LIMIT = 56 * 1024 * 1024


def _cparams(*sem):
    return pltpu.CompilerParams(dimension_semantics=sem, vmem_limit_bytes=VMEM_LIMIT)


def _sigmoid(x):
    return 1.0 / (1.0 + jnp.exp(-x))


def _silu(x):
    return x * _sigmoid(x)


def _softplus(x):
    return jnp.maximum(x, 0.0) + jnp.log1p(jnp.exp(-jnp.abs(x)))


def _rms_rows(x, g):
    ms = jnp.mean(x * x, axis=-1, keepdims=True)
    return x * lax.rsqrt(ms + NORM_EPS) * g


def _dot(a, b):
    return jnp.dot(a, b, preferred_element_type=F32)


def _dot_nt(a, b):
    return lax.dot_general(a, b, (((1,), (1,)), ((), ())), preferred_element_type=F32)


def _dot_tn(a, b):
    return lax.dot_general(a, b, (((0,), (0,)), ((), ())), preferred_element_type=F32)


def _split2(a):
    hi = a.astype(BF16)
    lo = (a - hi.astype(F32)).astype(BF16)
    return hi, lo


def _mm3_pair(a, b, left):
    (ah, al), (bh, bl) = a, b
    right = 1 - left
    zero = jnp.zeros_like(bh)
    bh_l, bh_r, bl_l, bl_r = bh * left, bh * right, bl * left, bl * right
    rhs = jnp.concatenate([jnp.concatenate([bh_l, bl_l], axis=1),
                           jnp.concatenate([bh_r, bl_r], axis=1),
                           jnp.concatenate([bh_l, zero], axis=1),
                           jnp.concatenate([bh_r, zero], axis=1)], axis=0)
    out = _dot(jnp.concatenate([ah, al], axis=1), rhs)
    n = bh.shape[1]
    return out[:, :n] + out[:, n:]


def _dot_exact01(m01, x):
    hi = x.astype(BF16)
    r1 = x - hi.astype(F32)
    mid = r1.astype(BF16)
    lo = (r1 - mid.astype(F32)).astype(BF16)
    return _dot(m01, hi) + (_dot(m01, mid) + _dot(m01, lo))


def _dot_exact01_r(x, m01):
    hi = x.astype(BF16)
    r1 = x - hi.astype(F32)
    mid = r1.astype(BF16)
    lo = (r1 - mid.astype(F32)).astype(BF16)
    return _dot(hi, m01) + (_dot(mid, m01) + _dot(lo, m01))


def _dn_inproj_kernel(x_ref, g_ref, w_ref, wbg_ref, par_ref, o_ref, bg_ref, *rest, transposed):
    if transposed:
        bgt_ref, h_scr = rest
    else:
        (h_scr,) = rest

    @pl.when(pl.program_id(1) == 0)
    def _():
        h = _rms_rows(x_ref[...], g_ref[...]).astype(BF16)
        h_scr[...] = h
        raw = _dot(h, wbg_ref[...])
        neg_a, dt_b, is_beta = par_ref[0:1, :], par_ref[1:2, :], par_ref[2:3, :]
        act = jnp.where(is_beta > 0.5, _sigmoid(raw), neg_a * _softplus(raw + dt_b))
        bg_ref[...] = act
        if transposed:
            bgt_ref[...] = act.T

    o_ref[...] = _dot(h_scr[...], w_ref[...])


def _dn_inproj(x, g, w_main, w_bg, par, *, transposed):
    m, d = x.shape
    n = w_main.shape[1]
    tm = min(1024, m)
    tn = 2048
    out_shape = [jax.ShapeDtypeStruct((m, n), F32), jax.ShapeDtypeStruct((m, LANES), F32)]
    out_specs = [pl.BlockSpec((tm, tn), lambda i, j: (i, j)),
                 pl.BlockSpec((tm, LANES), lambda i, j: (i, 0))]
    if transposed:
        out_shape.append(jax.ShapeDtypeStruct((LANES, m), F32))
        out_specs.append(pl.BlockSpec((LANES, tm), lambda i, j: (0, i)))
    return pl.pallas_call(
        functools.partial(_dn_inproj_kernel, transposed=transposed),
        grid=(m // tm, n // tn),
        in_specs=[pl.BlockSpec((tm, d), lambda i, j: (i, 0)),
                  pl.BlockSpec((1, d), lambda i, j: (0, 0)),
                  pl.BlockSpec((d, tn), lambda i, j: (0, j)),
                  pl.BlockSpec((d, LANES), lambda i, j: (0, 0)),
                  pl.BlockSpec((SUBLANES, LANES), lambda i, j: (0, 0))],
        out_specs=out_specs,
        out_shape=out_shape,
        scratch_shapes=[pltpu.VMEM((tm, d), BF16)],
        compiler_params=_cparams("parallel", "arbitrary"),
        name="dn_inproj",
    )(x, g, w_main, w_bg, par)


def _unit_lower_inverse_pairs(lows, c):
    row = lax.broadcasted_iota(I32, (c, 2 * c), 0)
    lane = lax.broadcasted_iota(I32, (c, 2 * c), 1)
    col = lane & (c - 1)
    left = jnp.where(lane < c, 1.0, 0.0).astype(BF16)

    def same_block(shift):
        return (row >> shift) == (col >> shift)

    low_parts = [_split2(low) for low in lows]
    base = same_block(3)
    negs = [jnp.where(base, -low, 0.0) for low in lows]
    eye = jnp.where(row == col, 1.0, 0.0)
    invs = [eye + neg for neg in negs]
    powers = [_split2(neg) for neg in negs]
    for _ in range(2):
        powers = [_split2(_mm3_pair(p, p, left)) for p in powers]
        invs = [inv + _mm3_pair(_split2(inv), p, left) for inv, p in zip(invs, powers)]
    shift = 3
    while (1 << shift) < c:
        mask = same_block(shift + 1) & jnp.logical_not(same_block(shift))
        inv_parts = [_split2(inv) for inv in invs]
        tmps = [_split2(jnp.where(mask, _mm3_pair(lp, ip, left), 0.0))
                for lp, ip in zip(low_parts, inv_parts)]
        invs = [inv - _mm3_pair(ip, t, left) for inv, ip, t in zip(invs, inv_parts, tmps)]
        shift += 1
    return invs


def _dn_core_kernel(q_ref, k_ref, v_ref, z_ref, bg_ref, bgt_ref, cwq_ref, cwk_ref, cwv_ref, nw_ref,
                    o_ref, s_out_ref, xe_scr, s_scr, *, tb, chunk, nh):
    hg = pl.program_id(1)
    t = pl.program_id(2)
    hd = HEAD_DIM
    n_chunks = tb // chunk
    k_off, v_off = nh * hd, 2 * nh * hd

    @pl.when(t == 0)
    def _():
        s_scr[...] = jnp.zeros(s_scr.shape, F32)
        xe_scr[0:SUBLANES, :] = jnp.zeros((SUBLANES, xe_scr.shape[1]), F32)

    xe_scr[SUBLANES:SUBLANES + tb, 0:k_off] = q_ref[...]
    xe_scr[SUBLANES:SUBLANES + tb, k_off:v_off] = k_ref[...]
    xe_scr[SUBLANES:SUBLANES + tb, v_off:] = v_ref[...]

    def conv(lo, hi, cw_ref, w_lo):
        w = cw_ref[:, w_lo:w_lo + hi - lo]
        acc = w[DN_CONV - 1:DN_CONV, :] * xe_scr[SUBLANES:SUBLANES + tb, lo:hi]
        for d in range(1, DN_CONV):
            acc = acc + w[DN_CONV - 1 - d:DN_CONV - d, :] * xe_scr[SUBLANES - d:SUBLANES - d + tb, lo:hi]
        return _silu(acc)

    ri = lax.broadcasted_iota(I32, (tb, tb), 0)
    ci = lax.broadcasted_iota(I32, (tb, tb), 1)
    cshift = chunk.bit_length() - 1
    same_chunk = (ri >> cshift) == (ci >> cshift)
    upper = jnp.where(same_chunk & (ri <= ci), 1.0, 0.0).astype(BF16)
    lower = jnp.where(same_chunk & (ri >= ci), 1.0, 0.0).astype(BF16)

    rc = lax.broadcasted_iota(I32, (chunk, 2 * chunk), 0)
    lane = lax.broadcasted_iota(I32, (chunk, 2 * chunk), 1)
    cc = lane & (chunk - 1)
    is_left = lane < chunk
    nw = nw_ref[...]

    items = []
    for g in range(nh):
        qc = conv(g * hd, (g + 1) * hd, cwq_ref, g * hd)
        kc = conv(k_off + g * hd, k_off + (g + 1) * hd, cwk_ref, g * hd)
        vc = conv(v_off + 2 * g * hd, v_off + 2 * (g + 1) * hd, cwv_ref, 2 * g * hd)
        qn = qc * lax.rsqrt(jnp.sum(qc * qc, axis=-1, keepdims=True) + NORM_EPS) * (hd ** -0.5)
        kn = kc * lax.rsqrt(jnp.sum(kc * kc, axis=-1, keepdims=True) + NORM_EPS)
        hq = hg * nh + g
        bgr = pltpu.roll(bg_ref[...], (LANES - SUBLANES * hq) % LANES, axis=1)
        bgt = bgt_ref[g * SUBLANES:(g + 1) * SUBLANES, :]
        gc_rows = _dot_exact01_r(bgt, upper)
        gc_cols = _dot_exact01(lower, bgr)
        for c in range(n_chunks):
            r0, r1 = c * chunk, (c + 1) * chunk
            q_c, k_c = qn[r0:r1], kn[r0:r1]
            qb, kb = q_c.astype(BF16), k_c.astype(BF16)
            prods = _dot_nt(jnp.concatenate([qb, kb], axis=0), jnp.concatenate([kb, kb], axis=0))
            qk2, kk2 = prods[:chunk], prods[chunk:]
            g_cols = [gc_cols[r0:r1, 2 + x:3 + x] for x in range(2)]
            b_cols = [bgr[r0:r1, x:x + 1] for x in range(2)]
            g_row2 = jnp.concatenate([gc_rows[2 + x:3 + x, r0:r1] for x in range(2)], axis=1)
            g_col2 = jnp.where(is_left, g_cols[0], g_cols[1])
            b_col2 = jnp.where(is_left, b_cols[0], b_cols[1])
            decay2 = jnp.exp(jnp.where(rc >= cc, g_col2 - g_row2, -jnp.inf))
            heads = []
            for x in range(2):
                eg = jnp.exp(g_cols[x])
                g_last = g_cols[x][chunk - 1:chunk, :]
                v_c = vc[r0:r1, x * hd:(x + 1) * hd]
                heads.append(dict(
                    rhs=jnp.concatenate([b_cols[x] * v_c, (b_cols[x] * eg) * k_c], axis=1).astype(BF16),
                    qg=(q_c * eg).astype(BF16),
                    kd=(k_c * jnp.exp(g_last - g_cols[x])).astype(BF16),
                    gl=jnp.exp(g_last)))
            items.append(dict(g=g, c=c, low=jnp.where(rc > cc, b_col2 * decay2 * kk2, 0.0),
                              qkd=(qk2 * decay2).astype(BF16), heads=heads))
    xe_scr[0:SUBLANES, :] = xe_scr[tb:tb + SUBLANES, :]

    invs = _unit_lower_inverse_pairs([it["low"] for it in items], chunk)
    for inv, it in zip(invs, items):
        ra, rb = it["heads"][0]["rhs"], it["heads"][1]["rhs"]
        zero = jnp.zeros_like(ra)
        rhs = jnp.concatenate([jnp.concatenate([ra, zero], axis=1),
                               jnp.concatenate([zero, rb], axis=1)], axis=0)
        it["sol"] = _dot(inv.astype(BF16), rhs)

    states = [s_scr[h] for h in range(2 * nh)]
    for c in range(n_chunks):
        r0, r1 = c * chunk, (c + 1) * chunk
        row = [it for it in items if it["c"] == c]
        sbs = [states[2 * it["g"] + x].astype(BF16) for it in row for x in range(2)]
        ws_qs = [_dot(jnp.concatenate([it["sol"][:, (2 * x + 1) * hd:(2 * x + 2) * hd].astype(BF16),
                                       it["heads"][x]["qg"]], axis=0), sbs[2 * i + x])
                 for i, it in enumerate(row) for x in range(2)]
        ubs = [(it["sol"][:, 2 * x * hd:(2 * x + 1) * hd] - ws_qs[2 * i + x][:chunk]).astype(BF16)
               for i, it in enumerate(row) for x in range(2)]
        intras = []
        for i, it in enumerate(row):
            zero = jnp.zeros_like(ubs[0])
            u_diag = jnp.concatenate([jnp.concatenate([ubs[2 * i], zero], axis=1),
                                      jnp.concatenate([zero, ubs[2 * i + 1]], axis=1)], axis=0)
            intras.append(_dot(it["qkd"], u_diag))
        for i, it in enumerate(row):
            for x in range(2):
                h = 2 * it["g"] + x
                head = it["heads"][x]
                states[h] = head["gl"] * states[h] + _dot_tn(head["kd"], ubs[2 * i + x])
                o = ws_qs[2 * i + x][chunk:] + intras[i][:, x * hd:(x + 1) * hd]
                z_c = z_ref[r0:r1, h * hd:(h + 1) * hd]
                on = o * lax.rsqrt(jnp.mean(o * o, axis=-1, keepdims=True) + NORM_EPS) * nw * _silu(z_c)
                o_ref[r0:r1, h * hd:(h + 1) * hd] = on.astype(o_ref.dtype)
    for h in range(2 * nh):
        s_scr[h] = states[h]

    @pl.when(t == pl.num_programs(2) - 1)
    def _():
        s_out_ref[...] = s_scr[...]


def _dn_core(proj, bg, bgt, conv_w, norm_w, bsz, seq):
    tb, hd, nh = DN_TBLOCK, HEAD_DIM, DN_HEADS_PER_STEP
    nt = seq // tb
    ng = DN_QK_HEADS // nh
    kernel = functools.partial(_dn_core_kernel, tb=tb, chunk=DN_CHUNK, nh=nh)
    tok = lambda b, h, t: b * nt + t
    return pl.pallas_call(
        kernel,
        grid=(bsz, ng, nt),
        in_specs=[
            pl.BlockSpec((tb, nh * hd), lambda b, h, t: (tok(b, h, t), h)),
            pl.BlockSpec((tb, nh * hd), lambda b, h, t: (tok(b, h, t), ng + h)),
            pl.BlockSpec((tb, 2 * nh * hd), lambda b, h, t: (tok(b, h, t), ng + h)),
            pl.BlockSpec((tb, 2 * nh * hd), lambda b, h, t: (tok(b, h, t), 2 * ng + h)),
            pl.BlockSpec((tb, LANES), lambda b, h, t: (tok(b, h, t), 0)),
            pl.BlockSpec((nh * SUBLANES, tb), lambda b, h, t: (h, tok(b, h, t))),
            pl.BlockSpec((DN_CONV, nh * hd), lambda b, h, t: (0, h)),
            pl.BlockSpec((DN_CONV, nh * hd), lambda b, h, t: (0, ng + h)),
            pl.BlockSpec((DN_CONV, 2 * nh * hd), lambda b, h, t: (0, ng + h)),
            pl.BlockSpec((1, hd), lambda b, h, t: (0, 0)),
        ],
        out_specs=[
            pl.BlockSpec((tb, 2 * nh * hd), lambda b, h, t: (tok(b, h, t), h)),
            pl.BlockSpec((None, 2 * nh, hd, hd), lambda b, h, t: (b, h, 0, 0)),
        ],
        out_shape=[jax.ShapeDtypeStruct((bsz * seq, DN_VAL_DIM), BF16),
                   jax.ShapeDtypeStruct((bsz, DN_V_HEADS, hd, hd), F32)],
        scratch_shapes=[pltpu.VMEM((tb + SUBLANES, 4 * nh * hd), F32),
                        pltpu.VMEM((2 * nh, hd, hd), F32)],
        compiler_params=_cparams("parallel", "parallel", "arbitrary"),
        name="dn_core",
    )(proj, proj, proj, proj, bg, bgt, conv_w, conv_w, conv_w, norm_w)


def _dn_sample_kernel(p_ref, bg_ref, cst_ref, cw_ref, s_ref, nw_ref,
                      o_ref, s_out_ref, cst_out_ref, r_scr):
    hd = HEAD_DIM
    x = p_ref[:, 0:DN_CONV_DIM]
    buf = cst_ref[...]
    y = x * cw_ref[DN_CONV - 1:DN_CONV, :]
    for i in range(DN_CONV - 1):
        y = y + buf[i:i + 1, :] * cw_ref[i:i + 1, :]
    cst_out_ref[0:DN_CONV - 2, :] = buf[1:DN_CONV - 1, :]
    cst_out_ref[DN_CONV - 2:DN_CONV - 1, :] = x
    a = _silu(y)

    r_scr[...] = jnp.zeros(r_scr.shape, F32)
    qs, ks = [], []
    for h in range(DN_QK_HEADS):
        qh = a[:, h * hd:(h + 1) * hd]
        kh = a[:, DN_KEY_DIM + h * hd:DN_KEY_DIM + (h + 1) * hd]
        qh = qh * lax.rsqrt(jnp.sum(qh * qh, axis=-1, keepdims=True) + NORM_EPS) * (hd ** -0.5)
        kh = kh * lax.rsqrt(jnp.sum(kh * kh, axis=-1, keepdims=True) + NORM_EPS)
        qs.append(qh)
        ks.append(kh)
        r_scr[h:h + 1, :] = qh
        r_scr[DN_QK_HEADS + h:DN_QK_HEADS + h + 1, :] = kh
    rt = r_scr[...].T

    bg = bg_ref[...]
    nw = nw_ref[...]
    rep = DN_V_HEADS // DN_QK_HEADS
    for h in range(DN_V_HEADS):
        hq, x_in = h // rep, h % rep
        beta = bg[:, SUBLANES * hq + x_in:SUBLANES * hq + x_in + 1]
        g = bg[:, SUBLANES * hq + 2 + x_in:SUBLANES * hq + 3 + x_in]
        eg = jnp.exp(g)
        q_col = rt[:, hq:hq + 1]
        k_col = rt[:, DN_QK_HEADS + hq:DN_QK_HEADS + hq + 1]
        s_old = s_ref[h]
        k_s = jnp.sum(k_col * s_old, axis=0, keepdims=True)
        q_s = jnp.sum(q_col * s_old, axis=0, keepdims=True)
        v_h = a[:, 2 * DN_KEY_DIM + h * hd:2 * DN_KEY_DIM + (h + 1) * hd]
        u = beta * v_h - (beta * eg) * k_s
        qk = jnp.sum(qs[hq] * ks[hq], axis=-1, keepdims=True)
        o = eg * q_s + qk * u
        s_out_ref[h] = eg * s_old + k_col * u
        z_h = p_ref[:, DN_CONV_DIM + h * hd:DN_CONV_DIM + (h + 1) * hd]
        on = o * lax.rsqrt(jnp.mean(o * o, axis=-1, keepdims=True) + NORM_EPS) * nw * _silu(z_h)
        o_ref[:, h * hd:(h + 1) * hd] = on.astype(o_ref.dtype)


def _dn_sample(proj, bg, conv_state, conv_w, state, norm_w):
    bd = proj.shape[0]
    hd = HEAD_DIM
    o, s_new, c_new = pl.pallas_call(
        _dn_sample_kernel,
        grid=(bd,),
        in_specs=[
            pl.BlockSpec((None, 1, DN_MAIN_DIM), lambda b: (b, 0, 0)),
            pl.BlockSpec((None, 1, LANES), lambda b: (b, 0, 0)),
            pl.BlockSpec((None, DN_CONV - 1, DN_CONV_DIM), lambda b: (b, 0, 0)),
            pl.BlockSpec((DN_CONV, DN_CONV_DIM), lambda b: (0, 0)),
            pl.BlockSpec((None, DN_V_HEADS, hd, hd), lambda b: (b, 0, 0, 0)),
            pl.BlockSpec((1, hd), lambda b: (0, 0)),
        ],
        out_specs=[
            pl.BlockSpec((None, 1, DN_VAL_DIM), lambda b: (b, 0, 0)),
            pl.BlockSpec((None, DN_V_HEADS, hd, hd), lambda b: (b, 0, 0, 0)),
            pl.BlockSpec((None, DN_CONV - 1, DN_CONV_DIM), lambda b: (b, 0, 0)),
        ],
        out_shape=[jax.ShapeDtypeStruct((bd, 1, DN_VAL_DIM), BF16),
                   jax.ShapeDtypeStruct(state.shape, F32),
                   jax.ShapeDtypeStruct(conv_state.shape, F32)],
        scratch_shapes=[pltpu.VMEM((LANES, LANES), F32)],
        compiler_params=_cparams("parallel"),
        name="dn_sample",
    )(proj.reshape(bd, 1, DN_MAIN_DIM), bg.reshape(bd, 1, LANES), conv_state, conv_w, state, norm_w)
    return o.reshape(bd, DN_VAL_DIM), s_new, c_new


def _matmul_residual_kernel(a_ref, w_ref, y_ref, o_ref):
    o_ref[...] = y_ref[...] + _dot(a_ref[...], w_ref[...])


def _matmul_residual(a, w, y):
    m, k = a.shape
    n = w.shape[1]
    tm = min(512, m)
    return pl.pallas_call(
        _matmul_residual_kernel,
        grid=(m // tm,),
        in_specs=[pl.BlockSpec((tm, k), lambda i: (i, 0)),
                  pl.BlockSpec((k, n), lambda i: (0, 0)),
                  pl.BlockSpec((tm, n), lambda i: (i, 0))],
        out_specs=pl.BlockSpec((tm, n), lambda i: (i, 0)),
        out_shape=jax.ShapeDtypeStruct((m, n), F32),
        compiler_params=_cparams("parallel"),
        name="matmul_residual",
    )(a, w, y)


def _mb_qkv_kernel(x_ref, g_ref, w_ref, cos_ref, sin_ref, q_ref, k_ref, v_ref, *rest,
                   with_blocks, blocks_per_seq):
    if with_blocks:
        kb_ref, vt_ref, km_ref, h_scr = rest
    else:
        (h_scr,) = rest
    i, j = pl.program_id(0), pl.program_id(1)
    hd = HEAD_DIM

    @pl.when(j == 0)
    def _():
        h_scr[...] = _rms_rows(x_ref[...], g_ref[...]).astype(BF16)

    y = _dot(h_scr[...], w_ref[...])
    tm = y.shape[0]

    def rope(val):
        cos, sin = cos_ref[...], sin_ref[...]
        parts = []
        for h in range(MB_HEADS):
            yh = val[:, h * hd:(h + 1) * hd]
            parts.append(yh * cos + pltpu.roll(yh, hd // 2, axis=1) * sin)
        return jnp.concatenate(parts, axis=1)

    @pl.when(j == 0)
    def _():
        q_ref[...] = rope(y).astype(q_ref.dtype)

    @pl.when(j == 1)
    def _():
        k = rope(y)
        k_ref[...] = k
        if with_blocks:
            row = lax.broadcasted_iota(I32, (tm, LANES), 0)
            lane = lax.broadcasted_iota(I32, (tm, LANES), 1)
            first_blk = (i * (tm // MB_BLOCK)) % blocks_per_seq
            onehot = jnp.where(lane == first_blk + row // MB_BLOCK, 1.0, 0.0).astype(BF16)
            for h in range(MB_HEADS):
                kb_ref[:, 2 * h * hd:(2 * h + 1) * hd] = k[:, h * hd:(h + 1) * hd].astype(BF16)
                kb_ref[:, (2 * h + 1) * hd:(2 * h + 2) * hd] = onehot
            for r in range(tm // MB_BLOCK):
                blk = k[r * MB_BLOCK:(r + 1) * MB_BLOCK, :]
                km_ref[r] = jnp.sum(blk, axis=0, keepdims=True) * (1.0 / MB_BLOCK)

    @pl.when(j == 2)
    def _():
        v_ref[...] = y
        if with_blocks:
            ones = jnp.ones((MB_V_PAD, MB_BLOCK), BF16)
            rows = hd + MB_V_PAD
            for r in range(tm // MB_BLOCK):
                vt = y[r * MB_BLOCK:(r + 1) * MB_BLOCK, :].T.astype(BF16)
                for h in range(MB_HEADS):
                    vt_ref[r, h * rows:h * rows + hd, :] = vt[h * hd:(h + 1) * hd, :]
                    vt_ref[r, h * rows + hd:(h + 1) * rows, :] = ones


def _mb_qkv(x, g, w, cos, sin, *, with_blocks):
    m, d = x.shape
    tm = min(512, m)
    n_pos_tiles = cos.shape[0] // tm
    out_shape = [jax.ShapeDtypeStruct((m, d), BF16 if with_blocks else F32),
                 jax.ShapeDtypeStruct((m, d), F32), jax.ShapeDtypeStruct((m, d), F32)]
    row = pl.BlockSpec((tm, d), lambda i, j: (i, 0))
    out_specs = [row, row, row]
    if with_blocks:
        nb = tm // MB_BLOCK
        vt_rows = MB_HEADS * (HEAD_DIM + MB_V_PAD)
        out_shape += [jax.ShapeDtypeStruct((m, 2 * d), BF16),
                      jax.ShapeDtypeStruct((m // MB_BLOCK, vt_rows, MB_BLOCK), BF16),
                      jax.ShapeDtypeStruct((m // MB_BLOCK, 1, d), F32)]
        out_specs += [pl.BlockSpec((tm, 2 * d), lambda i, j: (i, 0)),
                      pl.BlockSpec((nb, vt_rows, MB_BLOCK), lambda i, j: (i, 0, 0)),
                      pl.BlockSpec((nb, 1, d), lambda i, j: (i, 0, 0))]
    return pl.pallas_call(
        functools.partial(_mb_qkv_kernel, with_blocks=with_blocks,
                          blocks_per_seq=cos.shape[0] // MB_BLOCK if with_blocks else 0),
        grid=(m // tm, 3),
        in_specs=[row,
                  pl.BlockSpec((1, d), lambda i, j: (0, 0)),
                  pl.BlockSpec((d, d), lambda i, j: (0, j)),
                  pl.BlockSpec((tm, HEAD_DIM), lambda i, j: (i % n_pos_tiles, 0)),
                  pl.BlockSpec((tm, HEAD_DIM), lambda i, j: (i % n_pos_tiles, 0))],
        out_specs=out_specs,
        out_shape=out_shape,
        scratch_shapes=[pltpu.VMEM((tm, d), BF16)],
        compiler_params=_cparams("parallel", "arbitrary"),
        name="mb_qkv",
    )(x, g, w, cos, sin)


def _mb_attn_kernel(q_ref, k_ref, vt_ref, km_ref, o_ref, *logit_bufs, heads):
    n = pl.program_id(2)
    bq, hd = MB_BLOCK, HEAD_DIM
    vrows = hd + MB_V_PAD
    c = (hd ** -0.5) * 1.4426950408889634
    nb = km_ref.shape[0]
    blk = lax.broadcasted_iota(I32, (nb, bq), 0)
    kpos = lax.broadcasted_iota(I32, (bq, bq), 0)
    qpos = lax.broadcasted_iota(I32, (bq, bq), 1)
    own_rows = pl.ds(pl.multiple_of(n * bq, bq), bq)

    qas, carry = [], []
    for h in range(heads):
        q = q_ref[:, h * hd:(h + 1) * hd]
        km = km_ref[:, h * hd:(h + 1) * hd].astype(BF16)
        s = jnp.where(blk < n, _dot_nt(km, q), -jnp.inf)
        bias = jnp.full((nb, bq), NEG_BIG, F32)
        for _ in range(MB_TOP_K):
            top = jnp.max(s, axis=0, keepdims=True)
            idx = jnp.min(jnp.where(s == top, blk, nb), axis=0, keepdims=True)
            hit = blk == idx
            bias = jnp.where(hit & (idx < n), 0.0, bias)
            s = jnp.where(hit, -jnp.inf, s)
        bias = jnp.concatenate([bias, jnp.full((LANES - nb, bq), NEG_BIG, F32)], axis=0)
        qas.append(jnp.concatenate([q, bias.T.astype(BF16)], axis=1))

        k_own = k_ref[own_rows, 2 * h * hd:(2 * h + 1) * hd]
        st = jnp.where(kpos <= qpos, _dot_nt(k_own, q), NEG_BIG)
        m = jnp.max(st, axis=0, keepdims=True)
        p = jnp.exp2((st - m) * c)
        carry += [m, _dot(vt_ref[n, h * vrows:(h + 1) * vrows, :], p.astype(BF16))]

    sets_x = [logit_bufs[8 * h:8 * h + 4] for h in range(heads)]
    sets_y = [logit_bufs[8 * h + 4:8 * h + 8] for h in range(heads)]

    def fill(sets, j0):
        for h, bufs in enumerate(sets):
            for i, buf in enumerate(bufs):
                j = jnp.minimum(j0 + i, nb - 1)
                rows = pl.ds(pl.multiple_of(j * bq, bq), bq)
                buf[...] = _dot_nt(k_ref[rows, 2 * h * hd:(2 * h + 2) * hd], qas[h])

    def consume(sets, j0, carry):
        out = []
        for h, bufs in enumerate(sets):
            m, acc = carry[2 * h], carry[2 * h + 1]
            m_new = m
            for buf in bufs:
                m_new = jnp.maximum(m_new, jnp.max(buf[...], axis=0, keepdims=True))
            p = jnp.concatenate([jnp.exp2((buf[...] - m_new) * c).astype(BF16) for buf in bufs], axis=0)
            vt = jnp.concatenate([vt_ref[j0 + i, h * vrows:(h + 1) * vrows, :]
                                  for i in range(len(bufs))], axis=1)
            out += [m_new, acc * jnp.exp2((m - m_new) * c) + _dot(vt, p)]
        return tuple(out)

    def body(t, carry):
        j = 8 * t
        fill(sets_y, j + 4)
        carry = consume(sets_x, j, carry)
        fill(sets_x, j + 8)
        return consume(sets_y, j + 4, carry)

    half_trips = (n + 3) // 4
    fill(sets_x, 0)
    carry = lax.fori_loop(0, half_trips // 2, body, tuple(carry))
    carry = lax.cond(half_trips % 2 == 1,
                     lambda: consume(sets_x, 8 * (half_trips // 2), carry),
                     lambda: carry)
    for h in range(heads):
        acc = carry[2 * h + 1]
        o_ref[:, h * hd:(h + 1) * hd] = (acc[0:hd, :] / acc[hd:hd + 1, :]).T.astype(o_ref.dtype)


def _mb_attn(q, kb, vt, km, bsz, seq):
    hd, bq, heads = HEAD_DIM, MB_BLOCK, MB_HEADS_PER_STEP
    nb = seq // bq
    d = MB_HEADS * hd
    assert nb % 4 == 0
    return pl.pallas_call(
        functools.partial(_mb_attn_kernel, heads=heads),
        grid=(bsz, MB_HEADS // heads, nb),
        in_specs=[
            pl.BlockSpec((bq, heads * hd), lambda b, h, n: (b * nb + n, h)),
            pl.BlockSpec((seq, 2 * heads * hd), lambda b, h, n: (b, h)),
            pl.BlockSpec((nb, heads * (hd + MB_V_PAD), bq), lambda b, h, n: (b, h, 0)),
            pl.BlockSpec((nb, None, heads * hd), lambda b, h, n: (b, 0, h)),
        ],
        out_specs=pl.BlockSpec((bq, heads * hd), lambda b, h, n: (b * nb + n, h)),
        out_shape=jax.ShapeDtypeStruct((bsz * seq, d), BF16),
        scratch_shapes=[pltpu.VMEM((bq, bq), F32)] * (8 * heads),
        compiler_params=_cparams("parallel", "parallel", "arbitrary"),
        name="mb_attn",
    )(q, kb, vt, km)


PAGES_PER_STEP = 16


def _page_sum_kernel(pt_ref, *refs):
    pages, o_ref = refs[:PAGES_PER_STEP], refs[PAGES_PER_STEP]
    per_blk = MB_BLOCK // PAGE_SIZE
    for r in range(PAGES_PER_STEP // per_blk):
        s = jnp.sum(pages[per_blk * r][...], axis=0)
        for i in range(1, per_blk):
            s = s + jnp.sum(pages[per_blk * r + i][...], axis=0)
        o_ref[r] = s


def _page_block_sums(pool, page_table, layer):
    bd, n_pages = page_table.shape
    heads, hd = pool.shape[-2:]
    steps = n_pages // PAGES_PER_STEP
    rows = PAGES_PER_STEP * PAGE_SIZE // MB_BLOCK

    def page_spec(i):
        return pl.BlockSpec(
            (None, None, PAGE_SIZE, heads, hd),
            lambda b, s, pt: (layer, pt[b * n_pages + s * PAGES_PER_STEP + i], 0, 0, 0))

    return pl.pallas_call(
        _page_sum_kernel,
        grid_spec=pltpu.PrefetchScalarGridSpec(
            num_scalar_prefetch=1,
            grid=(bd, steps),
            in_specs=[page_spec(i) for i in range(PAGES_PER_STEP)],
            out_specs=pl.BlockSpec((None, rows, heads, hd), lambda b, s, pt: (b, s, 0, 0)),
        ),
        out_shape=jax.ShapeDtypeStruct((bd, steps * rows, heads, hd), F32),
        compiler_params=_cparams("parallel", "arbitrary"),
        name="mb_page_sums",
    )(page_table.reshape(-1), *([pool] * PAGES_PER_STEP))


def _mb_sample_select_kernel(q_ref, kn_ref, bs_ref, sel_ref, *, own):
    nb = bs_ref.shape[0]
    n_rows = nb + SUBLANES
    inv_blk = 1.0 / MB_BLOCK
    q = q_ref[...]
    sc = jnp.sum(bs_ref[...] * inv_blk * q[None], axis=-1, keepdims=True)
    own_sc = jnp.sum(kn_ref[...] * inv_blk * q, axis=-1, keepdims=True)
    tail_row = lax.broadcasted_iota(I32, (SUBLANES, MB_HEADS, 1), 0)
    tail = jnp.where(tail_row == 0, own_sc[None], -jnp.inf)
    blk = lax.broadcasted_iota(I32, (n_rows, MB_HEADS, 1), 0)
    s = jnp.where(blk < own, jnp.concatenate([sc, tail], axis=0), -jnp.inf)
    out_lane = lax.broadcasted_iota(I32, (MB_HEADS, LANES), 1)
    out = jnp.zeros((MB_HEADS, LANES), I32)
    for r in range(MB_TOP_K):
        m = jnp.max(s, axis=0, keepdims=True)
        idx = jnp.min(jnp.where(s == m, blk, n_rows), axis=0, keepdims=True)
        s = jnp.where(blk == idx, -jnp.inf, s)
        idx = idx[0]
        out = jnp.where(out_lane == r, idx, out)
        out = jnp.where(out_lane == MB_TOP_K + 1 + r, (idx < own).astype(I32), out)
    sel_ref[...] = out


def _mb_sample_select(q, k_new, blk_sums, own):
    bd = q.shape[0]
    nb = blk_sums.shape[1]
    return pl.pallas_call(
        functools.partial(_mb_sample_select_kernel, own=own),
        grid=(bd,),
        in_specs=[pl.BlockSpec((None, MB_HEADS, HEAD_DIM), lambda b: (b, 0, 0)),
                  pl.BlockSpec((None, MB_HEADS, HEAD_DIM), lambda b: (b, 0, 0)),
                  pl.BlockSpec((None, nb, MB_HEADS, HEAD_DIM), lambda b: (b, 0, 0, 0))],
        out_specs=pl.BlockSpec((None, MB_HEADS, LANES), lambda b: (b, 0, 0)),
        out_shape=jax.ShapeDtypeStruct((bd, MB_HEADS, LANES), I32),
        compiler_params=_cparams("parallel"),
        name="mb_sample_select",
    )(q, k_new, blk_sums)


def _mb_sample_attn_kernel(pt_ref, sel_ref, q_ref, kn_ref, vn_ref, pool_k, pool_v, o_ref,
                           kbuf, vbuf, sem, *, layer, n_pages):
    per_blk = MB_BLOCK // PAGE_SIZE
    n_pg = MB_TOP_K * per_blk
    b, h = pl.program_id(0), pl.program_id(1)
    n_heads = pl.num_programs(1)
    step = b * n_heads + h
    slot = step % 2
    scale = HEAD_DIM ** -0.5

    def copies(bb, hh, sl):
        base = (bb * MB_HEADS + hh) * SUBLANES
        out = []
        for r in range(MB_TOP_K):
            blk = jnp.minimum(sel_ref[base + r], n_pages // per_blk - 1)
            for half in range(per_blk):
                i = r * per_blk + half
                page = pt_ref[bb * n_pages + blk * per_blk + half]
                out.append(pltpu.make_async_copy(pool_k.at[layer, page, :, hh, :], kbuf.at[sl, i],
                                                 sem.at[sl, 0, i]))
                out.append(pltpu.make_async_copy(pool_v.at[layer, page, :, hh, :], vbuf.at[sl, i],
                                                 sem.at[sl, 1, i]))
        return out

    @pl.when(step == 0)
    def _():
        for cp in copies(b, h, slot):
            cp.start()

    @pl.when(step + 1 < pl.num_programs(0) * n_heads)
    def _():
        wrap = h + 1 == n_heads
        for cp in copies(jnp.where(wrap, b + 1, b), jnp.where(wrap, 0, h + 1), 1 - slot):
            cp.start()

    for cp in copies(b, h, slot):
        cp.wait()

    q = q_ref[...]
    q8 = jnp.broadcast_to(q, (SUBLANES, HEAD_DIM)).astype(BF16)
    keys = jnp.concatenate([kbuf[slot, i] for i in range(n_pg)], axis=0).astype(BF16)
    vals = jnp.concatenate([vbuf[slot, i] for i in range(n_pg)], axis=0).astype(BF16)
    lg = _dot_nt(q8, keys) * scale
    lane = lax.broadcasted_iota(I32, lg.shape, 1)
    base = (b * MB_HEADS + h) * SUBLANES
    for r in range(MB_TOP_K):
        penalty = jnp.where(sel_ref[base + MB_TOP_K + 1 + r] == 0, NEG_BIG, 0.0)
        in_blk = (lane >= r * MB_BLOCK) & (lane < (r + 1) * MB_BLOCK)
        lg = jnp.where(in_blk, lg + penalty, lg)
    qb = q.astype(BF16).astype(F32)
    lo = jnp.sum(qb * kn_ref[...].astype(BF16).astype(F32), axis=-1, keepdims=True) * scale
    m = jnp.maximum(jnp.max(lg, axis=-1, keepdims=True), lo)
    p = jnp.exp(lg - m)
    po = jnp.exp(lo - m)
    l = jnp.sum(p, axis=-1, keepdims=True) + po
    vn = vn_ref[...].astype(BF16).astype(F32)
    out = (_dot(p.astype(BF16), vals) + po.astype(BF16).astype(F32) * vn) / l
    o_ref[...] = out[0:1, :]


def _mb_sample_attn(pool_k, pool_v, page_table, sel, q, k_new, v_new, layer):
    bd, n_pages = page_table.shape
    hd = HEAD_DIM
    n_pg = MB_TOP_K * (MB_BLOCK // PAGE_SIZE)
    vec = pl.BlockSpec((None, None, 1, hd), lambda b, h, pt, sl: (b, h, 0, 0))
    hbm = pl.BlockSpec(memory_space=pl.ANY)
    q4 = q.reshape(bd, MB_HEADS, 1, hd)
    out = pl.pallas_call(
        functools.partial(_mb_sample_attn_kernel, layer=layer, n_pages=n_pages),
        grid_spec=pltpu.PrefetchScalarGridSpec(
            num_scalar_prefetch=2,
            grid=(bd, MB_HEADS),
            in_specs=[vec, vec, vec, hbm, hbm],
            out_specs=vec,
            scratch_shapes=[pltpu.VMEM((2, n_pg, PAGE_SIZE, hd), F32),
                            pltpu.VMEM((2, n_pg, PAGE_SIZE, hd), F32),
                            pltpu.SemaphoreType.DMA((2, 2, n_pg))],
        ),
        out_shape=jax.ShapeDtypeStruct((bd, MB_HEADS, 1, hd), F32),
        compiler_params=_cparams("arbitrary", "arbitrary"),
        name="mb_sample_attn",
    )(page_table.reshape(-1), sel[:, :, :SUBLANES].reshape(-1), q4, k_new.reshape(q4.shape), v_new.reshape(q4.shape),
      pool_k, pool_v)
    return out.reshape(bd, MB_HEADS * hd)


def _ffn_kernel(y_ref, g_ref, wg_ref, wv_ref, cwg_ref, cwv_ref, cbg_ref, cbv_ref, wo_ref,
                out_ref, bufg_ref, bufv_ref, h_scr, ug_scr, uv_scr, cg_scr, cv_scr, acc_scr,
                *, tiles_per_seq):
    i, f = pl.program_id(0), pl.program_id(1)
    tm = y_ref.shape[0]
    first = (i % tiles_per_seq) == 0

    @pl.when(f == 0)
    def _():
        h_scr[...] = _rms_rows(y_ref[...], g_ref[...]).astype(BF16)
        acc_scr[...] = jnp.zeros(acc_scr.shape, F32)

    @pl.when(first)
    def _():
        cg_scr[f] = jnp.zeros(cg_scr.shape[1:], F32)
        cv_scr[f] = jnp.zeros(cv_scr.shape[1:], F32)

    def branch(w_ref, cw_ref, cb_ref, u_scr, carry, buf_ref):
        u = _dot(h_scr[...], w_ref[...])
        u_scr[0:SUBLANES, :] = carry[f]
        u_scr[SUBLANES:SUBLANES + tm, :] = u
        r = cw_ref[FFN_CONV - 1:FFN_CONV, :] * u + cb_ref[...]
        for d in range(1, FFN_CONV):
            r = r + cw_ref[FFN_CONV - 1 - d:FFN_CONV - d, :] * u_scr[SUBLANES - d:SUBLANES - d + tm, :]
        last = u_scr[tm:tm + SUBLANES, :]
        carry[f] = last
        buf_ref[...] = last
        return r

    gate = branch(wg_ref, cwg_ref, cbg_ref, ug_scr, cg_scr, bufg_ref)
    val = branch(wv_ref, cwv_ref, cbv_ref, uv_scr, cv_scr, bufv_ref)
    act = (_silu(gate) * val).astype(BF16)
    acc_scr[...] += _dot(act, wo_ref[...])

    @pl.when(f == pl.num_programs(1) - 1)
    def _():
        out_ref[...] = y_ref[...] + acc_scr[...]


def _ffn_tile(dff):
    for cand in (1408, 1024, 512, 256, 128):
        if dff % cand == 0:
            return cand
    raise ValueError(f"unsupported d_ff {dff}")


def _ffn_prompt(y, g, w_in, conv_w, conv_b, w_out, bsz, seq):
    m, d = y.shape
    dff = w_out.shape[0]
    tm = min(512, seq)
    tf = _ffn_tile(dff)
    nf = dff // tf
    tiles_per_seq = seq // tm
    gate = lambda i, f: (0, f)
    val = lambda i, f: (0, f + nf)
    buf_spec = pl.BlockSpec((None, SUBLANES, tf), lambda i, f: (i, 0, f))
    out, bufg, bufv = pl.pallas_call(
        functools.partial(_ffn_kernel, tiles_per_seq=tiles_per_seq),
        grid=(m // tm, nf),
        in_specs=[pl.BlockSpec((tm, d), lambda i, f: (i, 0)),
                  pl.BlockSpec((1, d), lambda i, f: (0, 0)),
                  pl.BlockSpec((d, tf), gate), pl.BlockSpec((d, tf), val),
                  pl.BlockSpec((FFN_CONV, tf), gate), pl.BlockSpec((FFN_CONV, tf), val),
                  pl.BlockSpec((1, tf), gate), pl.BlockSpec((1, tf), val),
                  pl.BlockSpec((tf, d), lambda i, f: (f, 0))],
        out_specs=[pl.BlockSpec((tm, d), lambda i, f: (i, 0)), buf_spec, buf_spec],
        out_shape=[jax.ShapeDtypeStruct((m, d), F32),
                   jax.ShapeDtypeStruct((m // tm, SUBLANES, dff), F32),
                   jax.ShapeDtypeStruct((m // tm, SUBLANES, dff), F32)],
        scratch_shapes=[pltpu.VMEM((tm, d), BF16),
                        pltpu.VMEM((tm + SUBLANES, tf), F32), pltpu.VMEM((tm + SUBLANES, tf), F32),
                        pltpu.VMEM((nf, SUBLANES, tf), F32), pltpu.VMEM((nf, SUBLANES, tf), F32),
                        pltpu.VMEM((tm, d), F32)],
        compiler_params=_cparams("arbitrary", "arbitrary"),
        name="ffn_prompt",
    )(y, g, w_in, w_in, conv_w, conv_w, conv_b, conv_b, w_out)
    keep = FFN_CONV - 1
    last = slice(tiles_per_seq - 1, None, tiles_per_seq)
    new_buf = jnp.concatenate([bufg[last, SUBLANES - keep:], bufv[last, SUBLANES - keep:]], axis=-1)
    return out, new_buf


def _ffn_sample_kernel(y_ref, g_ref, wg_ref, wv_ref, cwg_ref, cwv_ref, cbg_ref, cbv_ref,
                       sg_ref, sv_ref, wo_ref, out_ref, ng_ref, nv_ref, h_scr, acc_scr):
    f = pl.program_id(0)

    @pl.when(f == 0)
    def _():
        h_scr[...] = _rms_rows(y_ref[...], g_ref[...]).astype(BF16)
        acc_scr[...] = jnp.zeros(acc_scr.shape, F32)

    def branch(w_ref, cw_ref, cb_ref, st_ref, new_ref):
        u = _dot(h_scr[...], w_ref[...])
        r = cw_ref[FFN_CONV - 1:FFN_CONV, :] * u + cb_ref[...]
        for i in range(FFN_CONV - 1):
            r = r + cw_ref[i:i + 1, :] * st_ref[i]
        for i in range(FFN_CONV - 2):
            new_ref[i] = st_ref[i + 1]
        new_ref[FFN_CONV - 2] = u
        return r

    gate = branch(wg_ref, cwg_ref, cbg_ref, sg_ref, ng_ref)
    val = branch(wv_ref, cwv_ref, cbv_ref, sv_ref, nv_ref)
    acc_scr[...] += _dot((_silu(gate) * val).astype(BF16), wo_ref[...])

    @pl.when(f == pl.num_programs(0) - 1)
    def _():
        out_ref[...] = y_ref[...] + acc_scr[...]


def _ffn_sample(y, g, w_in, conv_w, conv_b, w_out, state):
    bd, d = y.shape
    dff = w_out.shape[0]
    tf = _ffn_tile(dff)
    nf = dff // tf
    keep = FFN_CONV - 1
    st = jnp.transpose(state, (1, 0, 2))
    gate = lambda f: (0, f)
    val = lambda f: (0, f + nf)
    st_gate = pl.BlockSpec((keep, bd, tf), lambda f: (0, 0, f))
    st_val = pl.BlockSpec((keep, bd, tf), lambda f: (0, 0, f + nf))
    new_spec = pl.BlockSpec((keep, bd, tf), lambda f: (0, 0, f))
    out, ng, nv = pl.pallas_call(
        _ffn_sample_kernel,
        grid=(nf,),
        in_specs=[pl.BlockSpec((bd, d), lambda f: (0, 0)),
                  pl.BlockSpec((1, d), lambda f: (0, 0)),
                  pl.BlockSpec((d, tf), gate), pl.BlockSpec((d, tf), val),
                  pl.BlockSpec((FFN_CONV, tf), gate), pl.BlockSpec((FFN_CONV, tf), val),
                  pl.BlockSpec((1, tf), gate), pl.BlockSpec((1, tf), val),
                  st_gate, st_val,
                  pl.BlockSpec((tf, d), lambda f: (f, 0))],
        out_specs=[pl.BlockSpec((bd, d), lambda f: (0, 0)), new_spec, new_spec],
        out_shape=[jax.ShapeDtypeStruct((bd, d), F32),
                   jax.ShapeDtypeStruct((keep, bd, dff), F32),
                   jax.ShapeDtypeStruct((keep, bd, dff), F32)],
        scratch_shapes=[pltpu.VMEM((bd, d), BF16), pltpu.VMEM((bd, d), F32)],
        compiler_params=_cparams("arbitrary"),
        name="ffn_sample",
    )(y, g, w_in, w_in, conv_w, conv_w, conv_b, conv_b, st, st, w_out)
    new_state = jnp.transpose(jnp.concatenate([ng, nv], axis=-1), (1, 0, 2))
    return out, new_state


def _final_norm_kernel(x_ref, g_ref, o_ref):
    o_ref[...] = _rms_rows(x_ref[...], g_ref[...])


def _final_norm(x, g):
    m, d = x.shape
    tm = min(1024, m)
    return pl.pallas_call(
        _final_norm_kernel,
        grid=(m // tm,),
        in_specs=[pl.BlockSpec((tm, d), lambda i: (i, 0)), pl.BlockSpec((1, d), lambda i: (0, 0))],
        out_specs=pl.BlockSpec((tm, d), lambda i: (i, 0)),
        out_shape=jax.ShapeDtypeStruct((m, d), F32),
        compiler_params=_cparams("parallel"),
        name="final_norm",
    )(x, g)


def _rope_tables(pos):
    half = HEAD_DIM // 2
    inv = ROPE_THETA ** (-jnp.arange(half, dtype=F32) / half)
    ang = pos.astype(F32)[:, None] * inv[None, :]
    cos, sin = jnp.cos(ang), jnp.sin(ang)
    return jnp.concatenate([cos, cos], axis=-1), jnp.concatenate([-sin, sin], axis=-1)


def _dn_gate_layout(w_in, a_log, dt_bias):
    rep = DN_V_HEADS // DN_QK_HEADS
    n_gate = 2 * DN_V_HEADS
    src = np.full((LANES,), n_gate, np.int32)
    for hq in range(DN_QK_HEADS):
        for x in range(rep):
            src[SUBLANES * hq + x] = rep * hq + x
            src[SUBLANES * hq + rep + x] = DN_V_HEADS + rep * hq + x
    is_decay = (src >= DN_V_HEADS) & (src < n_gate)
    gates_t = w_in[:, DN_MAIN_DIM:].T
    w_bg = jnp.take(gates_t, src, axis=0, mode="fill", fill_value=0).T.astype(BF16)
    dec_src = np.where(is_decay, src - DN_V_HEADS, DN_V_HEADS)
    neg_a = jnp.take(-jnp.exp(a_log.astype(F32)), dec_src, mode="fill", fill_value=0)
    dt_b = jnp.take(dt_bias.astype(F32), dec_src, mode="fill", fill_value=0)
    is_beta = jnp.asarray((src < DN_V_HEADS).astype(np.float32))
    par = jnp.stack([neg_a, dt_b, is_beta])
    par = jnp.pad(par, ((0, SUBLANES - par.shape[0]), (0, 0)))
    return w_bg, par


def kernel(x_prompt, x_sample, cache_k, cache_v, state_dn, state_dn_conv, state_ffn_conv, page_table,
           norm_mix, norm_ffn, norm_out, dn_w_in, dn_conv_w, dn_a_log, dn_dt_bias, dn_norm, dn_w_out,
           mb_w_qkv, mb_w_o, ffn_w_in, ffn_conv_w, ffn_conv_b, ffn_w_out):
    bsz, seq, d = x_prompt.shape
    bd = x_sample.shape[0]
    depth = norm_mix.shape[0]
    n_pages = page_table.shape[1]
    past = n_pages * PAGE_SIZE
    own_blk = past // MB_BLOCK
    hd = HEAD_DIM

    yp = x_prompt.reshape(bsz * seq, d)
    ys = x_sample.reshape(bd, d)
    cos_p, sin_p = _rope_tables(jnp.arange(seq, dtype=I32))
    cos_s, sin_s = _rope_tables(jnp.full((bd,), past, dtype=I32))
    pool_k, pool_v = cache_k, cache_v

    kp_l, vp_l, ks_l, vs_l = [], [], [], []
    sp_l, ss_l, cp_l, cs_l = [], [], [], []
    fp_l, fs_l = [], []
    for layer in range(depth):
        j = layer // 2
        g_mix = norm_mix[layer].reshape(1, d)
        if layer % 2 == 0:
            w_main = dn_w_in[j][:, :DN_MAIN_DIM].astype(BF16)
            w_bg, par = _dn_gate_layout(dn_w_in[j], dn_a_log[j], dn_dt_bias[j])
            w_out = dn_w_out[j].astype(BF16)
            nw = dn_norm[j].reshape(1, hd)
            proj, bg, bgt = _dn_inproj(yp, g_mix, w_main, w_bg, par, transposed=True)
            o, s_new = _dn_core(proj, bg, bgt, dn_conv_w[j], nw, bsz, seq)
            yp = _matmul_residual(o, w_out, yp)
            sp_l.append(s_new)
            cp_l.append(proj.reshape(bsz, seq, DN_MAIN_DIM)[:, seq - (DN_CONV - 1):, :DN_CONV_DIM])

            proj_s, bg_s = _dn_inproj(ys, g_mix, w_main, w_bg, par, transposed=False)
            o_s, s_s, c_s = _dn_sample(proj_s, bg_s, state_dn_conv[j], dn_conv_w[j], state_dn[j], nw)
            ys = _matmul_residual(o_s, w_out, ys)
            ss_l.append(s_s)
            cs_l.append(c_s)
        else:
            w_qkv = mb_w_qkv[j].astype(BF16)
            w_o = mb_w_o[j].astype(BF16)
            q, k, v, kb, vt, km = _mb_qkv(yp, g_mix, w_qkv, cos_p, sin_p, with_blocks=True)
            att = _mb_attn(q, kb, vt, km, bsz, seq)
            yp = _matmul_residual(att, w_o, yp)
            kp_l.append(k.reshape(bsz, seq, MB_HEADS, hd))
            vp_l.append(v.reshape(bsz, seq, MB_HEADS, hd))

            q_s, k_s, v_s = _mb_qkv(ys, g_mix, w_qkv, cos_s, sin_s, with_blocks=False)
            blk_sums = _page_block_sums(pool_k, page_table, j)
            q3 = q_s.reshape(bd, MB_HEADS, hd)
            k3 = k_s.reshape(bd, MB_HEADS, hd)
            sel = _mb_sample_select(q3, k3, blk_sums, own_blk)
            att_s = _mb_sample_attn(pool_k, pool_v, page_table, sel, q3, k3, v_s.reshape(bd, MB_HEADS, hd), j)
            ys = _matmul_residual(att_s.astype(BF16), w_o, ys)
            ks_l.append(k_s.reshape(bd, 1, MB_HEADS, hd))
            vs_l.append(v_s.reshape(bd, 1, MB_HEADS, hd))

        g_ffn = norm_ffn[layer].reshape(1, d)
        w_fi = ffn_w_in[layer].astype(BF16)
        w_fo = ffn_w_out[layer].astype(BF16)
        cb = ffn_conv_b[layer].reshape(1, -1)
        yp, fbp = _ffn_prompt(yp, g_ffn, w_fi, ffn_conv_w[layer], cb, w_fo, bsz, seq)
        ys, fbs = _ffn_sample(ys, g_ffn, w_fi, ffn_conv_w[layer], cb, w_fo, state_ffn_conv[layer])
        fp_l.append(fbp)
        fs_l.append(fbs)

    g_out = norm_out.reshape(1, d)
    y_prompt = _final_norm(yp, g_out).reshape(bsz, seq, d)
    y_sample = _final_norm(ys, g_out).reshape(bd, 1, d)
    return (y_prompt, y_sample, jnp.stack(kp_l), jnp.stack(vp_l), jnp.stack(ks_l), jnp.stack(vs_l),
            jnp.stack(sp_l), jnp.stack(ss_l), jnp.stack(cp_l), jnp.stack(cs_l), jnp.stack(fp_l), jnp.stack(fs_l))
```

```python
import functools

import jax
import numpy as np
import jax.numpy as jnp
from jax import lax
from jax.experimental import pallas as pl
from jax.experimental.pallas import tpu as pltpu

F32 = jnp.float32
BF16 = jnp.bfloat16
I32 = jnp.int32

NORM_EPS = 1e-6
ROPE_THETA = 10000.0
HEAD_DIM = 128
DN_QK_HEADS = 8
DN_V_HEADS = 16
DN_CONV = 4
DN_KEY_DIM = DN_QK_HEADS * HEAD_DIM
DN_VAL_DIM = DN_V_HEADS * HEAD_DIM
DN_CONV_DIM = 2 * DN_KEY_DIM + DN_VAL_DIM
DN_MAIN_DIM = DN_CONV_DIM + DN_VAL_DIM
DN_CHUNK = 64
DN_TBLOCK = 128
DN_HEADS_PER_STEP = 8
MB_HEADS = 8
MB_BLOCK = 256
MB_TOP_K = 3
MB_V_PAD = 16
MB_HEADS_PER_STEP = 2
PAGE_SIZE = 128
FFN_CONV = 3
SUBLANES = 8
LANES = 128
NEG_BIG = -1e30
VMEM_LIMIT = 56 * 1024 * 1024


def _cparams(*sem):
    return pltpu.CompilerParams(dimension_semantics=sem, vmem_limit_bytes=VMEM_LIMIT)


def _sigmoid(x):
    return 1.0 / (1.0 + jnp.exp(-x))


def _silu(x):
    return x * _sigmoid(x)


def _softplus(x):
    return jnp.maximum(x, 0.0) + jnp.log1p(jnp.exp(-jnp.abs(x)))


def _rms_rows(x, g):
    ms = jnp.mean(x * x, axis=-1, keepdims=True)
    return x * lax.rsqrt(ms + NORM_EPS) * g


def _dot(a, b):
    return jnp.dot(a, b, preferred_element_type=F32)


def _dot_nt(a, b):
    return lax.dot_general(a, b, (((1,), (1,)), ((), ())), preferred_element_type=F32)


def _dot_tn(a, b):
    return lax.dot_general(a, b, (((0,), (0,)), ((), ())), preferred_element_type=F32)


def _split2(a):
    hi = a.astype(BF16)
    lo = (a - hi.astype(F32)).astype(BF16)
    return hi, lo


def _mm2_pair(a, bh, left):
    ah, al = a
    b_l, b_r = bh * left, bh * (1 - left)
    return _dot(jnp.concatenate([ah, al], axis=1), jnp.concatenate([b_l, b_r, b_l, b_r], axis=0))


def _dot_exact01(m01, x):
    hi = x.astype(BF16)
    r1 = x - hi.astype(F32)
    mid = r1.astype(BF16)
    lo = (r1 - mid.astype(F32)).astype(BF16)
    return _dot(m01, hi) + (_dot(m01, mid) + _dot(m01, lo))


def _dot_exact01_r(x, m01):
    hi = x.astype(BF16)
    r1 = x - hi.astype(F32)
    mid = r1.astype(BF16)
    lo = (r1 - mid.astype(F32)).astype(BF16)
    return _dot(hi, m01) + (_dot(mid, m01) + _dot(lo, m01))


def _dn_inproj_kernel(x_ref, g_ref, w_ref, wbg_ref, par_ref, o_ref, bg_ref, *rest, transposed):
    if transposed:
        bgt_ref, h_scr = rest
    else:
        (h_scr,) = rest

    @pl.when(pl.program_id(1) == 0)
    def _():
        h = _rms_rows(x_ref[...], g_ref[...]).astype(BF16)
        h_scr[...] = h
        raw = _dot(h, wbg_ref[...])
        neg_a, dt_b, is_beta = par_ref[0:1, :], par_ref[1:2, :], par_ref[2:3, :]
        act = jnp.where(is_beta > 0.5, _sigmoid(raw), neg_a * _softplus(raw + dt_b))
        bg_ref[...] = act
        if transposed:
            bgt_ref[...] = act.T

    o_ref[...] = _dot(h_scr[...], w_ref[...])


def _dn_inproj(x, g, w_main, w_bg, par, *, transposed):
    m, d = x.shape
    n = w_main.shape[1]
    tm = min(1024, m)
    tn = 2048
    out_shape = [jax.ShapeDtypeStruct((m, n), F32), jax.ShapeDtypeStruct((m, LANES), F32)]
    out_specs = [pl.BlockSpec((tm, tn), lambda i, j: (i, j)),
                 pl.BlockSpec((tm, LANES), lambda i, j: (i, 0))]
    if transposed:
        out_shape.append(jax.ShapeDtypeStruct((LANES, m), F32))
        out_specs.append(pl.BlockSpec((LANES, tm), lambda i, j: (0, i)))
    return pl.pallas_call(
        functools.partial(_dn_inproj_kernel, transposed=transposed),
        grid=(m // tm, n // tn),
        in_specs=[pl.BlockSpec((tm, d), lambda i, j: (i, 0)),
                  pl.BlockSpec((1, d), lambda i, j: (0, 0)),
                  pl.BlockSpec((d, tn), lambda i, j: (0, j)),
                  pl.BlockSpec((d, LANES), lambda i, j: (0, 0)),
                  pl.BlockSpec((SUBLANES, LANES), lambda i, j: (0, 0))],
        out_specs=out_specs,
        out_shape=out_shape,
        scratch_shapes=[pltpu.VMEM((tm, d), BF16)],
        compiler_params=_cparams("parallel", "arbitrary"),
        name="dn_inproj",
    )(x, g, w_main, w_bg, par)


def _unit_lower_inverse_pairs(lows, c):
    row = lax.broadcasted_iota(I32, (c, 2 * c), 0)
    lane = lax.broadcasted_iota(I32, (c, 2 * c), 1)
    col = lane & (c - 1)
    left = jnp.where(lane < c, 1.0, 0.0).astype(BF16)

    def same_block(shift):
        return (row >> shift) == (col >> shift)

    low_parts = [_split2(low) for low in lows]
    base = same_block(3)
    negs = [jnp.where(base, -low, 0.0) for low in lows]
    eye = jnp.where(row == col, 1.0, 0.0)
    invs = [eye + neg for neg in negs]
    powers = [_split2(neg) for neg in negs]
    for _ in range(2):
        powers = [_split2(_mm2_pair(p, p[0], left)) for p in powers]
        invs = [inv + _mm2_pair(_split2(inv), p[0], left) for inv, p in zip(invs, powers)]
    shift = 3
    while (1 << shift) < c:
        mask = same_block(shift + 1) & jnp.logical_not(same_block(shift))
        inv_parts = [_split2(inv) for inv in invs]
        tmps = [jnp.where(mask, _mm2_pair(lp, ip[0], left), 0.0).astype(BF16)
                for lp, ip in zip(low_parts, inv_parts)]
        invs = [inv - _mm2_pair(ip, t, left) for inv, ip, t in zip(invs, inv_parts, tmps)]
        shift += 1
    return invs


def _dn_core_kernel(q_ref, k_ref, v_ref, z_ref, bg_ref, bgt_ref, cwq_ref, cwk_ref, cwv_ref, nw_ref,
                    o_ref, s_out_ref, xe_scr, s_scr, *, tb, chunk, nh):
    hg = pl.program_id(1)
    t = pl.program_id(2)
    hd = HEAD_DIM
    n_chunks = tb // chunk
    k_off, v_off = nh * hd, 2 * nh * hd

    @pl.when(t == 0)
    def _():
        s_scr[...] = jnp.zeros(s_scr.shape, F32)
        xe_scr[0:SUBLANES, :] = jnp.zeros((SUBLANES, xe_scr.shape[1]), F32)

    xe_scr[SUBLANES:SUBLANES + tb, 0:k_off] = q_ref[...]
    xe_scr[SUBLANES:SUBLANES + tb, k_off:v_off] = k_ref[...]
    xe_scr[SUBLANES:SUBLANES + tb, v_off:] = v_ref[...]

    def conv(lo, hi, cw_ref, w_lo):
        w = cw_ref[:, w_lo:w_lo + hi - lo]
        acc = w[DN_CONV - 1:DN_CONV, :] * xe_scr[SUBLANES:SUBLANES + tb, lo:hi]
        for d in range(1, DN_CONV):
            acc = acc + w[DN_CONV - 1 - d:DN_CONV - d, :] * xe_scr[SUBLANES - d:SUBLANES - d + tb, lo:hi]
        return _silu(acc)

    ri = lax.broadcasted_iota(I32, (tb, tb), 0)
    ci = lax.broadcasted_iota(I32, (tb, tb), 1)
    cshift = chunk.bit_length() - 1
    same_chunk = (ri >> cshift) == (ci >> cshift)
    upper = jnp.where(same_chunk & (ri <= ci), 1.0, 0.0).astype(BF16)
    lower = jnp.where(same_chunk & (ri >= ci), 1.0, 0.0).astype(BF16)

    rc = lax.broadcasted_iota(I32, (chunk, 2 * chunk), 0)
    lane = lax.broadcasted_iota(I32, (chunk, 2 * chunk), 1)
    cc = lane & (chunk - 1)
    is_left = lane < chunk
    nw = nw_ref[...]

    items = []
    for g in range(nh):
        qc = conv(g * hd, (g + 1) * hd, cwq_ref, g * hd)
        kc = conv(k_off + g * hd, k_off + (g + 1) * hd, cwk_ref, g * hd)
        vc = conv(v_off + 2 * g * hd, v_off + 2 * (g + 1) * hd, cwv_ref, 2 * g * hd)
        qn = qc * lax.rsqrt(jnp.sum(qc * qc, axis=-1, keepdims=True) + NORM_EPS) * (hd ** -0.5)
        kn = kc * lax.rsqrt(jnp.sum(kc * kc, axis=-1, keepdims=True) + NORM_EPS)
        hq = hg * nh + g
        bgr = pltpu.roll(bg_ref[...], (LANES - SUBLANES * hq) % LANES, axis=1)
        bgt = bgt_ref[g * SUBLANES:(g + 1) * SUBLANES, :]
        gc_rows = _dot_exact01_r(bgt, upper)
        gc_cols = _dot_exact01(lower, bgr)
        for c in range(n_chunks):
            r0, r1 = c * chunk, (c + 1) * chunk
            q_c, k_c = qn[r0:r1], kn[r0:r1]
            qb, kb = q_c.astype(BF16), k_c.astype(BF16)
            prods = _dot_nt(jnp.concatenate([qb, kb], axis=0), jnp.concatenate([kb, kb], axis=0))
            qk2, kk2 = prods[:chunk], prods[chunk:]
            g_cols = [gc_cols[r0:r1, 2 + x:3 + x] for x in range(2)]
            b_cols = [bgr[r0:r1, x:x + 1] for x in range(2)]
            g_row2 = jnp.concatenate([gc_rows[2 + x:3 + x, r0:r1] for x in range(2)], axis=1)
            g_col2 = jnp.where(is_left, g_cols[0], g_cols[1])
            b_col2 = jnp.where(is_left, b_cols[0], b_cols[1])
            decay2 = jnp.exp(jnp.where(rc >= cc, g_col2 - g_row2, -jnp.inf))
            heads = []
            for x in range(2):
                eg = jnp.exp(g_cols[x])
                g_last = g_cols[x][chunk - 1:chunk, :]
                v_c = vc[r0:r1, x * hd:(x + 1) * hd]
                heads.append(dict(
                    rhs=jnp.concatenate([b_cols[x] * v_c, (b_cols[x] * eg) * k_c], axis=1).astype(BF16),
                    qg=(q_c * eg).astype(BF16),
                    kd=(k_c * jnp.exp(g_last - g_cols[x])).astype(BF16),
                    gl=jnp.exp(g_last)))
            items.append(dict(g=g, c=c, low=jnp.where(rc > cc, b_col2 * decay2 * kk2, 0.0),
                              qkd=(qk2 * decay2).astype(BF16), heads=heads))
    xe_scr[0:SUBLANES, :] = xe_scr[tb:tb + SUBLANES, :]

    invs = _unit_lower_inverse_pairs([it["low"] for it in items], chunk)
    for inv, it in zip(invs, items):
        ra, rb = it["heads"][0]["rhs"], it["heads"][1]["rhs"]
        zero = jnp.zeros_like(ra)
        rhs = jnp.concatenate([jnp.concatenate([ra, zero], axis=1),
                               jnp.concatenate([zero, rb], axis=1)], axis=0)
        it["sol"] = _dot(inv.astype(BF16), rhs)

    states = [s_scr[h] for h in range(2 * nh)]
    for c in range(n_chunks):
        r0, r1 = c * chunk, (c + 1) * chunk
        row = [it for it in items if it["c"] == c]
        sbs = [states[2 * it["g"] + x].astype(BF16) for it in row for x in range(2)]
        ws_qs = [_dot(jnp.concatenate([it["sol"][:, (2 * x + 1) * hd:(2 * x + 2) * hd].astype(BF16),
                                       it["heads"][x]["qg"]], axis=0), sbs[2 * i + x])
                 for i, it in enumerate(row) for x in range(2)]
        ubs = [(it["sol"][:, 2 * x * hd:(2 * x + 1) * hd] - ws_qs[2 * i + x][:chunk]).astype(BF16)
               for i, it in enumerate(row) for x in range(2)]
        intras = []
        for i, it in enumerate(row):
            zero = jnp.zeros_like(ubs[0])
            u_diag = jnp.concatenate([jnp.concatenate([ubs[2 * i], zero], axis=1),
                                      jnp.concatenate([zero, ubs[2 * i + 1]], axis=1)], axis=0)
            intras.append(_dot(it["qkd"], u_diag))
        for i, it in enumerate(row):
            for x in range(2):
                h = 2 * it["g"] + x
                head = it["heads"][x]
                states[h] = head["gl"] * states[h] + _dot_tn(head["kd"], ubs[2 * i + x])
                o = ws_qs[2 * i + x][chunk:] + intras[i][:, x * hd:(x + 1) * hd]
                z_c = z_ref[r0:r1, h * hd:(h + 1) * hd]
                on = o * lax.rsqrt(jnp.mean(o * o, axis=-1, keepdims=True) + NORM_EPS) * nw * _silu(z_c)
                o_ref[r0:r1, h * hd:(h + 1) * hd] = on.astype(o_ref.dtype)
    for h in range(2 * nh):
        s_scr[h] = states[h]

    @pl.when(t == pl.num_programs(2) - 1)
    def _():
        s_out_ref[...] = s_scr[...]


def _dn_core(proj, bg, bgt, conv_w, norm_w, bsz, seq):
    tb, hd, nh = DN_TBLOCK, HEAD_DIM, DN_HEADS_PER_STEP
    nt = seq // tb
    ng = DN_QK_HEADS // nh
    kernel = functools.partial(_dn_core_kernel, tb=tb, chunk=DN_CHUNK, nh=nh)
    tok = lambda b, h, t: b * nt + t
    return pl.pallas_call(
        kernel,
        grid=(bsz, ng, nt),
        in_specs=[
            pl.BlockSpec((tb, nh * hd), lambda b, h, t: (tok(b, h, t), h)),
            pl.BlockSpec((tb, nh * hd), lambda b, h, t: (tok(b, h, t), ng + h)),
            pl.BlockSpec((tb, 2 * nh * hd), lambda b, h, t: (tok(b, h, t), ng + h)),
            pl.BlockSpec((tb, 2 * nh * hd), lambda b, h, t: (tok(b, h, t), 2 * ng + h)),
            pl.BlockSpec((tb, LANES), lambda b, h, t: (tok(b, h, t), 0)),
            pl.BlockSpec((nh * SUBLANES, tb), lambda b, h, t: (h, tok(b, h, t))),
            pl.BlockSpec((DN_CONV, nh * hd), lambda b, h, t: (0, h)),
            pl.BlockSpec((DN_CONV, nh * hd), lambda b, h, t: (0, ng + h)),
            pl.BlockSpec((DN_CONV, 2 * nh * hd), lambda b, h, t: (0, ng + h)),
            pl.BlockSpec((1, hd), lambda b, h, t: (0, 0)),
        ],
        out_specs=[
            pl.BlockSpec((tb, 2 * nh * hd), lambda b, h, t: (tok(b, h, t), h)),
            pl.BlockSpec((None, 2 * nh, hd, hd), lambda b, h, t: (b, h, 0, 0)),
        ],
        out_shape=[jax.ShapeDtypeStruct((bsz * seq, DN_VAL_DIM), BF16),
                   jax.ShapeDtypeStruct((bsz, DN_V_HEADS, hd, hd), F32)],
        scratch_shapes=[pltpu.VMEM((tb + SUBLANES, 4 * nh * hd), F32),
                        pltpu.VMEM((2 * nh, hd, hd), F32)],
        compiler_params=_cparams("parallel", "parallel", "arbitrary"),
        name="dn_core",
    )(proj, proj, proj, proj, bg, bgt, conv_w, conv_w, conv_w, norm_w)


def _dn_sample_kernel(p_ref, bg_ref, cst_ref, cw_ref, s_ref, nw_ref,
                      o_ref, s_out_ref, cst_out_ref, r_scr):
    hd = HEAD_DIM
    x = p_ref[:, 0:DN_CONV_DIM]
    buf = cst_ref[...]
    y = x * cw_ref[DN_CONV - 1:DN_CONV, :]
    for i in range(DN_CONV - 1):
        y = y + buf[i:i + 1, :] * cw_ref[i:i + 1, :]
    cst_out_ref[0:DN_CONV - 2, :] = buf[1:DN_CONV - 1, :]
    cst_out_ref[DN_CONV - 2:DN_CONV - 1, :] = x
    a = _silu(y)

    r_scr[...] = jnp.zeros(r_scr.shape, F32)
    qs, ks = [], []
    for h in range(DN_QK_HEADS):
        qh = a[:, h * hd:(h + 1) * hd]
        kh = a[:, DN_KEY_DIM + h * hd:DN_KEY_DIM + (h + 1) * hd]
        qh = qh * lax.rsqrt(jnp.sum(qh * qh, axis=-1, keepdims=True) + NORM_EPS) * (hd ** -0.5)
        kh = kh * lax.rsqrt(jnp.sum(kh * kh, axis=-1, keepdims=True) + NORM_EPS)
        qs.append(qh)
        ks.append(kh)
        r_scr[h:h + 1, :] = qh
        r_scr[DN_QK_HEADS + h:DN_QK_HEADS + h + 1, :] = kh
    rt = r_scr[...].T

    bg = bg_ref[...]
    nw = nw_ref[...]
    rep = DN_V_HEADS // DN_QK_HEADS
    for h in range(DN_V_HEADS):
        hq, x_in = h // rep, h % rep
        beta = bg[:, SUBLANES * hq + x_in:SUBLANES * hq + x_in + 1]
        g = bg[:, SUBLANES * hq + 2 + x_in:SUBLANES * hq + 3 + x_in]
        eg = jnp.exp(g)
        q_col = rt[:, hq:hq + 1]
        k_col = rt[:, DN_QK_HEADS + hq:DN_QK_HEADS + hq + 1]
        s_old = s_ref[h]
        k_s = jnp.sum(k_col * s_old, axis=0, keepdims=True)
        q_s = jnp.sum(q_col * s_old, axis=0, keepdims=True)
        v_h = a[:, 2 * DN_KEY_DIM + h * hd:2 * DN_KEY_DIM + (h + 1) * hd]
        u = beta * v_h - (beta * eg) * k_s
        qk = jnp.sum(qs[hq] * ks[hq], axis=-1, keepdims=True)
        o = eg * q_s + qk * u
        s_out_ref[h] = eg * s_old + k_col * u
        z_h = p_ref[:, DN_CONV_DIM + h * hd:DN_CONV_DIM + (h + 1) * hd]
        on = o * lax.rsqrt(jnp.mean(o * o, axis=-1, keepdims=True) + NORM_EPS) * nw * _silu(z_h)
        o_ref[:, h * hd:(h + 1) * hd] = on.astype(o_ref.dtype)


def _dn_sample(proj, bg, conv_state, conv_w, state, norm_w):
    bd = proj.shape[0]
    hd = HEAD_DIM
    o, s_new, c_new = pl.pallas_call(
        _dn_sample_kernel,
        grid=(bd,),
        in_specs=[
            pl.BlockSpec((None, 1, DN_MAIN_DIM), lambda b: (b, 0, 0)),
            pl.BlockSpec((None, 1, LANES), lambda b: (b, 0, 0)),
            pl.BlockSpec((None, DN_CONV - 1, DN_CONV_DIM), lambda b: (b, 0, 0)),
            pl.BlockSpec((DN_CONV, DN_CONV_DIM), lambda b: (0, 0)),
            pl.BlockSpec((None, DN_V_HEADS, hd, hd), lambda b: (b, 0, 0, 0)),
            pl.BlockSpec((1, hd), lambda b: (0, 0)),
        ],
        out_specs=[
            pl.BlockSpec((None, 1, DN_VAL_DIM), lambda b: (b, 0, 0)),
            pl.BlockSpec((None, DN_V_HEADS, hd, hd), lambda b: (b, 0, 0, 0)),
            pl.BlockSpec((None, DN_CONV - 1, DN_CONV_DIM), lambda b: (b, 0, 0)),
        ],
        out_shape=[jax.ShapeDtypeStruct((bd, 1, DN_VAL_DIM), BF16),
                   jax.ShapeDtypeStruct(state.shape, F32),
                   jax.ShapeDtypeStruct(conv_state.shape, F32)],
        scratch_shapes=[pltpu.VMEM((LANES, LANES), F32)],
        compiler_params=_cparams("parallel"),
        name="dn_sample",
    )(proj.reshape(bd, 1, DN_MAIN_DIM), bg.reshape(bd, 1, LANES), conv_state, conv_w, state, norm_w)
    return o.reshape(bd, DN_VAL_DIM), s_new, c_new


def _matmul_residual_kernel(a_ref, w_ref, y_ref, o_ref):
    o_ref[...] = y_ref[...] + _dot(a_ref[...], w_ref[...])


def _matmul_residual(a, w, y):
    m, k = a.shape
    n = w.shape[1]
    tm = min(512, m)
    return pl.pallas_call(
        _matmul_residual_kernel,
        grid=(m // tm,),
        in_specs=[pl.BlockSpec((tm, k), lambda i: (i, 0)),
                  pl.BlockSpec((k, n), lambda i: (0, 0)),
                  pl.BlockSpec((tm, n), lambda i: (i, 0))],
        out_specs=pl.BlockSpec((tm, n), lambda i: (i, 0)),
        out_shape=jax.ShapeDtypeStruct((m, n), F32),
        compiler_params=_cparams("parallel"),
        name="matmul_residual",
    )(a, w, y)


def _mb_qkv_kernel(x_ref, g_ref, w_ref, cos_ref, sin_ref, q_ref, k_ref, v_ref, *rest,
                   with_blocks, blocks_per_seq):
    if with_blocks:
        kb_ref, vt_ref, km_ref, h_scr = rest
    else:
        (h_scr,) = rest
    i, j = pl.program_id(0), pl.program_id(1)
    hd = HEAD_DIM

    @pl.when(j == 0)
    def _():
        h_scr[...] = _rms_rows(x_ref[...], g_ref[...]).astype(BF16)

    y = _dot(h_scr[...], w_ref[...])
    tm = y.shape[0]

    def rope(val):
        cos, sin = cos_ref[...], sin_ref[...]
        parts = []
        for h in range(MB_HEADS):
            yh = val[:, h * hd:(h + 1) * hd]
            parts.append(yh * cos + pltpu.roll(yh, hd // 2, axis=1) * sin)
        return jnp.concatenate(parts, axis=1)

    @pl.when(j == 0)
    def _():
        q_ref[...] = rope(y).astype(q_ref.dtype)

    @pl.when(j == 1)
    def _():
        k = rope(y)
        k_ref[...] = k
        if with_blocks:
            row = lax.broadcasted_iota(I32, (tm, LANES), 0)
            lane = lax.broadcasted_iota(I32, (tm, LANES), 1)
            first_blk = (i * (tm // MB_BLOCK)) % blocks_per_seq
            onehot = jnp.where(lane == first_blk + row // MB_BLOCK, 1.0, 0.0).astype(BF16)
            for h in range(MB_HEADS):
                kb_ref[:, 2 * h * hd:(2 * h + 1) * hd] = k[:, h * hd:(h + 1) * hd].astype(BF16)
                kb_ref[:, (2 * h + 1) * hd:(2 * h + 2) * hd] = onehot
            for r in range(tm // MB_BLOCK):
                blk = k[r * MB_BLOCK:(r + 1) * MB_BLOCK, :]
                km_ref[r] = jnp.sum(blk, axis=0, keepdims=True) * (1.0 / MB_BLOCK)

    @pl.when(j == 2)
    def _():
        v_ref[...] = y
        if with_blocks:
            ones = jnp.ones((MB_V_PAD, MB_BLOCK), BF16)
            rows = hd + MB_V_PAD
            for r in range(tm // MB_BLOCK):
                vt = y[r * MB_BLOCK:(r + 1) * MB_BLOCK, :].T.astype(BF16)
                for h in range(MB_HEADS):
                    vt_ref[r, h * rows:h * rows + hd, :] = vt[h * hd:(h + 1) * hd, :]
                    vt_ref[r, h * rows + hd:(h + 1) * rows, :] = ones


def _mb_qkv(x, g, w, cos, sin, *, with_blocks):
    m, d = x.shape
    tm = min(512, m)
    n_pos_tiles = cos.shape[0] // tm
    out_shape = [jax.ShapeDtypeStruct((m, d), BF16 if with_blocks else F32),
                 jax.ShapeDtypeStruct((m, d), F32), jax.ShapeDtypeStruct((m, d), F32)]
    row = pl.BlockSpec((tm, d), lambda i, j: (i, 0))
    out_specs = [row, row, row]
    if with_blocks:
        nb = tm // MB_BLOCK
        vt_rows = MB_HEADS * (HEAD_DIM + MB_V_PAD)
        out_shape += [jax.ShapeDtypeStruct((m, 2 * d), BF16),
                      jax.ShapeDtypeStruct((m // MB_BLOCK, vt_rows, MB_BLOCK), BF16),
                      jax.ShapeDtypeStruct((m // MB_BLOCK, 1, d), F32)]
        out_specs += [pl.BlockSpec((tm, 2 * d), lambda i, j: (i, 0)),
                      pl.BlockSpec((nb, vt_rows, MB_BLOCK), lambda i, j: (i, 0, 0)),
                      pl.BlockSpec((nb, 1, d), lambda i, j: (i, 0, 0))]
    return pl.pallas_call(
        functools.partial(_mb_qkv_kernel, with_blocks=with_blocks,
                          blocks_per_seq=cos.shape[0] // MB_BLOCK if with_blocks else 0),
        grid=(m // tm, 3),
        in_specs=[row,
                  pl.BlockSpec((1, d), lambda i, j: (0, 0)),
                  pl.BlockSpec((d, d), lambda i, j: (0, j)),
                  pl.BlockSpec((tm, HEAD_DIM), lambda i, j: (i % n_pos_tiles, 0)),
                  pl.BlockSpec((tm, HEAD_DIM), lambda i, j: (i % n_pos_tiles, 0))],
        out_specs=out_specs,
        out_shape=out_shape,
        scratch_shapes=[pltpu.VMEM((tm, d), BF16)],
        compiler_params=_cparams("parallel", "arbitrary"),
        name="mb_qkv",
    )(x, g, w, cos, sin)


def _mb_attn_kernel(q_ref, k_ref, vt_ref, km_ref, o_ref, *logit_bufs, heads):
    n = pl.program_id(2)
    bq, hd = MB_BLOCK, HEAD_DIM
    vrows = hd + MB_V_PAD
    c = (hd ** -0.5) * 1.4426950408889634
    nb = km_ref.shape[0]
    blk = lax.broadcasted_iota(I32, (nb, bq), 0)
    kpos = lax.broadcasted_iota(I32, (bq, bq), 0)
    qpos = lax.broadcasted_iota(I32, (bq, bq), 1)
    own_rows = pl.ds(pl.multiple_of(n * bq, bq), bq)

    qs = [q_ref[:, h * hd:(h + 1) * hd] for h in range(heads)]
    sts = [jnp.where(kpos <= qpos, _dot_nt(k_ref[own_rows, 2 * h * hd:(2 * h + 1) * hd], qs[h]), NEG_BIG)
           for h in range(heads)]
    scores = [_dot_nt(km_ref[:, h * hd:(h + 1) * hd].astype(BF16), qs[h]) for h in range(heads)]
    carry = []
    for h in range(heads):
        m = jnp.max(sts[h], axis=0, keepdims=True)
        p = jnp.exp2((sts[h] - m) * c)
        carry += [m, _dot(vt_ref[n, h * vrows:(h + 1) * vrows, :], p.astype(BF16))]

    qas = []
    for h in range(heads):
        s = jnp.where(blk < n, scores[h], -jnp.inf)
        bias = jnp.full((nb, bq), NEG_BIG, F32)
        for _ in range(MB_TOP_K):
            top = jnp.max(s, axis=0, keepdims=True)
            idx = jnp.min(jnp.where(s == top, blk, nb), axis=0, keepdims=True)
            hit = blk == idx
            bias = jnp.where(hit & (idx < n), 0.0, bias)
            s = jnp.where(hit, -jnp.inf, s)
        bias = jnp.concatenate([bias, jnp.full((LANES - nb, bq), NEG_BIG, F32)], axis=0)
        qas.append(jnp.concatenate([qs[h], bias.T.astype(BF16)], axis=1))

    sets_x = [logit_bufs[8 * h:8 * h + 4] for h in range(heads)]
    sets_y = [logit_bufs[8 * h + 4:8 * h + 8] for h in range(heads)]

    def fill(sets, j0):
        for h, bufs in enumerate(sets):
            for i, buf in enumerate(bufs):
                j = jnp.minimum(j0 + i, nb - 1)
                rows = pl.ds(pl.multiple_of(j * bq, bq), bq)
                buf[...] = _dot_nt(k_ref[rows, 2 * h * hd:(2 * h + 2) * hd], qas[h])

    def consume(sets, j0, carry):
        out = []
        for h, bufs in enumerate(sets):
            m, acc = carry[2 * h], carry[2 * h + 1]
            m_new = m
            for buf in bufs:
                m_new = jnp.maximum(m_new, jnp.max(buf[...], axis=0, keepdims=True))
            p = jnp.concatenate([jnp.exp2((buf[...] - m_new) * c).astype(BF16) for buf in bufs], axis=0)
            vt = jnp.concatenate([vt_ref[j0 + i, h * vrows:(h + 1) * vrows, :]
                                  for i in range(len(bufs))], axis=1)
            out += [m_new, acc * jnp.exp2((m - m_new) * c) + _dot(vt, p)]
        return tuple(out)

    def body(t, carry):
        j = 8 * t
        fill(sets_y, j + 4)
        carry = consume(sets_x, j, carry)
        fill(sets_x, j + 8)
        return consume(sets_y, j + 4, carry)

    half_trips = (n + 3) // 4
    fill(sets_x, 0)
    carry = lax.fori_loop(0, half_trips // 2, body, tuple(carry))
    carry = lax.cond(half_trips % 2 == 1,
                     lambda: consume(sets_x, 8 * (half_trips // 2), carry),
                     lambda: carry)
    for h in range(heads):
        acc = carry[2 * h + 1]
        o_ref[:, h * hd:(h + 1) * hd] = (acc[0:hd, :] / acc[hd:hd + 1, :]).T.astype(o_ref.dtype)


def _mb_attn(q, kb, vt, km, bsz, seq):
    hd, bq, heads = HEAD_DIM, MB_BLOCK, MB_HEADS_PER_STEP
    nb = seq // bq
    d = MB_HEADS * hd
    assert nb % 4 == 0
    return pl.pallas_call(
        functools.partial(_mb_attn_kernel, heads=heads),
        grid=(bsz, MB_HEADS // heads, nb),
        in_specs=[
            pl.BlockSpec((bq, heads * hd), lambda b, h, n: (b * nb + n, h)),
            pl.BlockSpec((seq, 2 * heads * hd), lambda b, h, n: (b, h)),
            pl.BlockSpec((nb, heads * (hd + MB_V_PAD), bq), lambda b, h, n: (b, h, 0)),
            pl.BlockSpec((nb, None, heads * hd), lambda b, h, n: (b, 0, h)),
        ],
        out_specs=pl.BlockSpec((bq, heads * hd), lambda b, h, n: (b * nb + n, h)),
        out_shape=jax.ShapeDtypeStruct((bsz * seq, d), BF16),
        scratch_shapes=[pltpu.VMEM((bq, bq), F32)] * (8 * heads),
        compiler_params=_cparams("parallel", "parallel", "arbitrary"),
        name="mb_attn",
    )(q, kb, vt, km)


PAGES_PER_STEP = 16


def _page_sum_kernel(pt_ref, *refs):
    pages, o_ref = refs[:PAGES_PER_STEP], refs[PAGES_PER_STEP]
    per_blk = MB_BLOCK // PAGE_SIZE
    for r in range(PAGES_PER_STEP // per_blk):
        s = jnp.sum(pages[per_blk * r][...], axis=0)
        for i in range(1, per_blk):
            s = s + jnp.sum(pages[per_blk * r + i][...], axis=0)
        o_ref[r] = s


def _page_block_sums(pool, page_table, layer):
    bd, n_pages = page_table.shape
    heads, hd = pool.shape[-2:]
    steps = n_pages // PAGES_PER_STEP
    rows = PAGES_PER_STEP * PAGE_SIZE // MB_BLOCK

    def page_spec(i):
        return pl.BlockSpec(
            (None, None, PAGE_SIZE, heads, hd),
            lambda b, s, pt: (layer, pt[b * n_pages + s * PAGES_PER_STEP + i], 0, 0, 0))

    return pl.pallas_call(
        _page_sum_kernel,
        grid_spec=pltpu.PrefetchScalarGridSpec(
            num_scalar_prefetch=1,
            grid=(bd, steps),
            in_specs=[page_spec(i) for i in range(PAGES_PER_STEP)],
            out_specs=pl.BlockSpec((None, rows, heads, hd), lambda b, s, pt: (b, s, 0, 0)),
        ),
        out_shape=jax.ShapeDtypeStruct((bd, steps * rows, heads, hd), F32),
        compiler_params=_cparams("parallel", "arbitrary"),
        name="mb_page_sums",
    )(page_table.reshape(-1), *([pool] * PAGES_PER_STEP))


def _mb_sample_select_kernel(q_ref, kn_ref, bs_ref, sel_ref, *, own):
    nb = bs_ref.shape[0]
    n_rows = nb + SUBLANES
    inv_blk = 1.0 / MB_BLOCK
    q = q_ref[...]
    sc = jnp.sum(bs_ref[...] * inv_blk * q[None], axis=-1, keepdims=True)
    own_sc = jnp.sum(kn_ref[...] * inv_blk * q, axis=-1, keepdims=True)
    tail_row = lax.broadcasted_iota(I32, (SUBLANES, MB_HEADS, 1), 0)
    tail = jnp.where(tail_row == 0, own_sc[None], -jnp.inf)
    blk = lax.broadcasted_iota(I32, (n_rows, MB_HEADS, 1), 0)
    s = jnp.where(blk < own, jnp.concatenate([sc, tail], axis=0), -jnp.inf)
    out_lane = lax.broadcasted_iota(I32, (MB_HEADS, LANES), 1)
    out = jnp.zeros((MB_HEADS, LANES), I32)
    for r in range(MB_TOP_K):
        m = jnp.max(s, axis=0, keepdims=True)
        idx = jnp.min(jnp.where(s == m, blk, n_rows), axis=0, keepdims=True)
        s = jnp.where(blk == idx, -jnp.inf, s)
        idx = idx[0]
        out = jnp.where(out_lane == r, idx, out)
        out = jnp.where(out_lane == MB_TOP_K + 1 + r, (idx < own).astype(I32), out)
    sel_ref[...] = out


def _mb_sample_select(q, k_new, blk_sums, own):
    bd = q.shape[0]
    nb = blk_sums.shape[1]
    return pl.pallas_call(
        functools.partial(_mb_sample_select_kernel, own=own),
        grid=(bd,),
        in_specs=[pl.BlockSpec((None, MB_HEADS, HEAD_DIM), lambda b: (b, 0, 0)),
                  pl.BlockSpec((None, MB_HEADS, HEAD_DIM), lambda b: (b, 0, 0)),
                  pl.BlockSpec((None, nb, MB_HEADS, HEAD_DIM), lambda b: (b, 0, 0, 0))],
        out_specs=pl.BlockSpec((None, MB_HEADS, LANES), lambda b: (b, 0, 0)),
        out_shape=jax.ShapeDtypeStruct((bd, MB_HEADS, LANES), I32),
        compiler_params=_cparams("parallel"),
        name="mb_sample_select",
    )(q, k_new, blk_sums)


def _mb_sample_attn_kernel(pt_ref, sel_ref, q_ref, kn_ref, vn_ref, pool_k, pool_v, o_ref,
                           kbuf, vbuf, sem, *, layer, n_pages):
    per_blk = MB_BLOCK // PAGE_SIZE
    n_pg = MB_TOP_K * per_blk
    b, h = pl.program_id(0), pl.program_id(1)
    n_heads = pl.num_programs(1)
    step = b * n_heads + h
    slot = step % 2
    scale = HEAD_DIM ** -0.5

    def copies(bb, hh, sl):
        base = (bb * MB_HEADS + hh) * SUBLANES
        out = []
        for r in range(MB_TOP_K):
            blk = jnp.minimum(sel_ref[base + r], n_pages // per_blk - 1)
            for half in range(per_blk):
                i = r * per_blk + half
                page = pt_ref[bb * n_pages + blk * per_blk + half]
                out.append(pltpu.make_async_copy(pool_k.at[layer, page, :, hh, :], kbuf.at[sl, i],
                                                 sem.at[sl, 0, i]))
                out.append(pltpu.make_async_copy(pool_v.at[layer, page, :, hh, :], vbuf.at[sl, i],
                                                 sem.at[sl, 1, i]))
        return out

    @pl.when(step == 0)
    def _():
        for cp in copies(b, h, slot):
            cp.start()

    @pl.when(step + 1 < pl.num_programs(0) * n_heads)
    def _():
        wrap = h + 1 == n_heads
        for cp in copies(jnp.where(wrap, b + 1, b), jnp.where(wrap, 0, h + 1), 1 - slot):
            cp.start()

    for cp in copies(b, h, slot):
        cp.wait()

    q = q_ref[...]
    q8 = jnp.broadcast_to(q, (SUBLANES, HEAD_DIM)).astype(BF16)
    keys = jnp.concatenate([kbuf[slot, i] for i in range(n_pg)], axis=0).astype(BF16)
    vals = jnp.concatenate([vbuf[slot, i] for i in range(n_pg)], axis=0).astype(BF16)
    lg = _dot_nt(q8, keys) * scale
    lane = lax.broadcasted_iota(I32, lg.shape, 1)
    base = (b * MB_HEADS + h) * SUBLANES
    for r in range(MB_TOP_K):
        penalty = jnp.where(sel_ref[base + MB_TOP_K + 1 + r] == 0, NEG_BIG, 0.0)
        in_blk = (lane >= r * MB_BLOCK) & (lane < (r + 1) * MB_BLOCK)
        lg = jnp.where(in_blk, lg + penalty, lg)
    qb = q.astype(BF16).astype(F32)
    lo = jnp.sum(qb * kn_ref[...].astype(BF16).astype(F32), axis=-1, keepdims=True) * scale
    m = jnp.maximum(jnp.max(lg, axis=-1, keepdims=True), lo)
    p = jnp.exp(lg - m)
    po = jnp.exp(lo - m)
    l = jnp.sum(p, axis=-1, keepdims=True) + po
    vn = vn_ref[...].astype(BF16).astype(F32)
    out = (_dot(p.astype(BF16), vals) + po.astype(BF16).astype(F32) * vn) / l
    o_ref[...] = out[0:1, :]


def _mb_sample_attn(pool_k, pool_v, page_table, sel, q, k_new, v_new, layer):
    bd, n_pages = page_table.shape
    hd = HEAD_DIM
    n_pg = MB_TOP_K * (MB_BLOCK // PAGE_SIZE)
    vec = pl.BlockSpec((None, None, 1, hd), lambda b, h, pt, sl: (b, h, 0, 0))
    hbm = pl.BlockSpec(memory_space=pl.ANY)
    q4 = q.reshape(bd, MB_HEADS, 1, hd)
    out = pl.pallas_call(
        functools.partial(_mb_sample_attn_kernel, layer=layer, n_pages=n_pages),
        grid_spec=pltpu.PrefetchScalarGridSpec(
            num_scalar_prefetch=2,
            grid=(bd, MB_HEADS),
            in_specs=[vec, vec, vec, hbm, hbm],
            out_specs=vec,
            scratch_shapes=[pltpu.VMEM((2, n_pg, PAGE_SIZE, hd), F32),
                            pltpu.VMEM((2, n_pg, PAGE_SIZE, hd), F32),
                            pltpu.SemaphoreType.DMA((2, 2, n_pg))],
        ),
        out_shape=jax.ShapeDtypeStruct((bd, MB_HEADS, 1, hd), F32),
        compiler_params=_cparams("arbitrary", "arbitrary"),
        name="mb_sample_attn",
    )(page_table.reshape(-1), sel[:, :, :SUBLANES].reshape(-1), q4, k_new.reshape(q4.shape), v_new.reshape(q4.shape),
      pool_k, pool_v)
    return out.reshape(bd, MB_HEADS * hd)


def _ffn_kernel(y_ref, g_ref, wg_ref, wv_ref, cwg_ref, cwv_ref, cbg_ref, cbv_ref, wo_ref,
                out_ref, bufg_ref, bufv_ref, h_scr, ug_scr, uv_scr, cg_scr, cv_scr, acc_scr,
                *, tiles_per_seq):
    i, f = pl.program_id(0), pl.program_id(1)
    tm = y_ref.shape[0]
    first = (i % tiles_per_seq) == 0

    @pl.when(f == 0)
    def _():
        h_scr[...] = _rms_rows(y_ref[...], g_ref[...]).astype(BF16)
        acc_scr[...] = jnp.zeros(acc_scr.shape, F32)

    @pl.when(first)
    def _():
        cg_scr[f] = jnp.zeros(cg_scr.shape[1:], F32)
        cv_scr[f] = jnp.zeros(cv_scr.shape[1:], F32)

    def branch(w_ref, cw_ref, cb_ref, u_scr, carry, buf_ref):
        u = _dot(h_scr[...], w_ref[...])
        u_scr[0:SUBLANES, :] = carry[f]
        u_scr[SUBLANES:SUBLANES + tm, :] = u
        r = cw_ref[FFN_CONV - 1:FFN_CONV, :] * u + cb_ref[...]
        for d in range(1, FFN_CONV):
            r = r + cw_ref[FFN_CONV - 1 - d:FFN_CONV - d, :] * u_scr[SUBLANES - d:SUBLANES - d + tm, :]
        last = u_scr[tm:tm + SUBLANES, :]
        carry[f] = last
        buf_ref[...] = last
        return r

    gate = branch(wg_ref, cwg_ref, cbg_ref, ug_scr, cg_scr, bufg_ref)
    val = branch(wv_ref, cwv_ref, cbv_ref, uv_scr, cv_scr, bufv_ref)
    act = (_silu(gate) * val).astype(BF16)
    acc_scr[...] += _dot(act, wo_ref[...])

    @pl.when(f == pl.num_programs(1) - 1)
    def _():
        out_ref[...] = y_ref[...] + acc_scr[...]


def _ffn_tile(dff):
    for cand in (1408, 1024, 512, 256, 128):
        if dff % cand == 0:
            return cand
    raise ValueError(f"unsupported d_ff {dff}")


def _ffn_prompt(y, g, w_in, conv_w, conv_b, w_out, bsz, seq):
    m, d = y.shape
    dff = w_out.shape[0]
    tm = min(512, seq)
    tf = _ffn_tile(dff)
    nf = dff // tf
    tiles_per_seq = seq // tm
    gate = lambda i, f: (0, f)
    val = lambda i, f: (0, f + nf)
    buf_spec = pl.BlockSpec((None, SUBLANES, tf), lambda i, f: (i, 0, f))
    out, bufg, bufv = pl.pallas_call(
        functools.partial(_ffn_kernel, tiles_per_seq=tiles_per_seq),
        grid=(m // tm, nf),
        in_specs=[pl.BlockSpec((tm, d), lambda i, f: (i, 0)),
                  pl.BlockSpec((1, d), lambda i, f: (0, 0)),
                  pl.BlockSpec((d, tf), gate), pl.BlockSpec((d, tf), val),
                  pl.BlockSpec((FFN_CONV, tf), gate), pl.BlockSpec((FFN_CONV, tf), val),
                  pl.BlockSpec((1, tf), gate), pl.BlockSpec((1, tf), val),
                  pl.BlockSpec((tf, d), lambda i, f: (f, 0))],
        out_specs=[pl.BlockSpec((tm, d), lambda i, f: (i, 0)), buf_spec, buf_spec],
        out_shape=[jax.ShapeDtypeStruct((m, d), F32),
                   jax.ShapeDtypeStruct((m // tm, SUBLANES, dff), F32),
                   jax.ShapeDtypeStruct((m // tm, SUBLANES, dff), F32)],
        scratch_shapes=[pltpu.VMEM((tm, d), BF16),
                        pltpu.VMEM((tm + SUBLANES, tf), F32), pltpu.VMEM((tm + SUBLANES, tf), F32),
                        pltpu.VMEM((nf, SUBLANES, tf), F32), pltpu.VMEM((nf, SUBLANES, tf), F32),
                        pltpu.VMEM((tm, d), F32)],
        compiler_params=_cparams("arbitrary", "arbitrary"),
        name="ffn_prompt",
    )(y, g, w_in, w_in, conv_w, conv_w, conv_b, conv_b, w_out)
    keep = FFN_CONV - 1
    last = slice(tiles_per_seq - 1, None, tiles_per_seq)
    new_buf = jnp.concatenate([bufg[last, SUBLANES - keep:], bufv[last, SUBLANES - keep:]], axis=-1)
    return out, new_buf


def _ffn_sample_kernel(y_ref, g_ref, wg_ref, wv_ref, cwg_ref, cwv_ref, cbg_ref, cbv_ref,
                       sg_ref, sv_ref, wo_ref, out_ref, ng_ref, nv_ref, h_scr, acc_scr):
    f = pl.program_id(0)

    @pl.when(f == 0)
    def _():
        h_scr[...] = _rms_rows(y_ref[...], g_ref[...]).astype(BF16)
        acc_scr[...] = jnp.zeros(acc_scr.shape, F32)

    def branch(w_ref, cw_ref, cb_ref, st_ref, new_ref):
        u = _dot(h_scr[...], w_ref[...])
        r = cw_ref[FFN_CONV - 1:FFN_CONV, :] * u + cb_ref[...]
        for i in range(FFN_CONV - 1):
            r = r + cw_ref[i:i + 1, :] * st_ref[i]
        for i in range(FFN_CONV - 2):
            new_ref[i] = st_ref[i + 1]
        new_ref[FFN_CONV - 2] = u
        return r

    gate = branch(wg_ref, cwg_ref, cbg_ref, sg_ref, ng_ref)
    val = branch(wv_ref, cwv_ref, cbv_ref, sv_ref, nv_ref)
    acc_scr[...] += _dot((_silu(gate) * val).astype(BF16), wo_ref[...])

    @pl.when(f == pl.num_programs(0) - 1)
    def _():
        out_ref[...] = y_ref[...] + acc_scr[...]


def _ffn_sample(y, g, w_in, conv_w, conv_b, w_out, state):
    bd, d = y.shape
    dff = w_out.shape[0]
    tf = _ffn_tile(dff)
    nf = dff // tf
    keep = FFN_CONV - 1
    st = jnp.transpose(state, (1, 0, 2))
    gate = lambda f: (0, f)
    val = lambda f: (0, f + nf)
    st_gate = pl.BlockSpec((keep, bd, tf), lambda f: (0, 0, f))
    st_val = pl.BlockSpec((keep, bd, tf), lambda f: (0, 0, f + nf))
    new_spec = pl.BlockSpec((keep, bd, tf), lambda f: (0, 0, f))
    out, ng, nv = pl.pallas_call(
        _ffn_sample_kernel,
        grid=(nf,),
        in_specs=[pl.BlockSpec((bd, d), lambda f: (0, 0)),
                  pl.BlockSpec((1, d), lambda f: (0, 0)),
                  pl.BlockSpec((d, tf), gate), pl.BlockSpec((d, tf), val),
                  pl.BlockSpec((FFN_CONV, tf), gate), pl.BlockSpec((FFN_CONV, tf), val),
                  pl.BlockSpec((1, tf), gate), pl.BlockSpec((1, tf), val),
                  st_gate, st_val,
                  pl.BlockSpec((tf, d), lambda f: (f, 0))],
        out_specs=[pl.BlockSpec((bd, d), lambda f: (0, 0)), new_spec, new_spec],
        out_shape=[jax.ShapeDtypeStruct((bd, d), F32),
                   jax.ShapeDtypeStruct((keep, bd, dff), F32),
                   jax.ShapeDtypeStruct((keep, bd, dff), F32)],
        scratch_shapes=[pltpu.VMEM((bd, d), BF16), pltpu.VMEM((bd, d), F32)],
        compiler_params=_cparams("arbitrary"),
        name="ffn_sample",
    )(y, g, w_in, w_in, conv_w, conv_w, conv_b, conv_b, st, st, w_out)
    new_state = jnp.transpose(jnp.concatenate([ng, nv], axis=-1), (1, 0, 2))
    return out, new_state


def _final_norm_kernel(x_ref, g_ref, o_ref):
    o_ref[...] = _rms_rows(x_ref[...], g_ref[...])


def _final_norm(x, g):
    m, d = x.shape
    tm = min(1024, m)
    return pl.pallas_call(
        _final_norm_kernel,
        grid=(m // tm,),
        in_specs=[pl.BlockSpec((tm, d), lambda i: (i, 0)), pl.BlockSpec((1, d), lambda i: (0, 0))],
        out_specs=pl.BlockSpec((tm, d), lambda i: (i, 0)),
        out_shape=jax.ShapeDtypeStruct((m, d), F32),
        compiler_params=_cparams("parallel"),
        name="final_norm",
    )(x, g)


def _rope_tables(pos):
    half = HEAD_DIM // 2
    inv = ROPE_THETA ** (-jnp.arange(half, dtype=F32) / half)
    ang = pos.astype(F32)[:, None] * inv[None, :]
    cos, sin = jnp.cos(ang), jnp.sin(ang)
    return jnp.concatenate([cos, cos], axis=-1), jnp.concatenate([-sin, sin], axis=-1)


def _dn_gate_layout(w_in, a_log, dt_bias):
    rep = DN_V_HEADS // DN_QK_HEADS
    n_gate = 2 * DN_V_HEADS
    src = np.full((LANES,), n_gate, np.int32)
    for hq in range(DN_QK_HEADS):
        for x in range(rep):
            src[SUBLANES * hq + x] = rep * hq + x
            src[SUBLANES * hq + rep + x] = DN_V_HEADS + rep * hq + x
    is_decay = (src >= DN_V_HEADS) & (src < n_gate)
    gates_t = w_in[:, DN_MAIN_DIM:].T
    w_bg = jnp.take(gates_t, src, axis=0, mode="fill", fill_value=0).T.astype(BF16)
    dec_src = np.where(is_decay, src - DN_V_HEADS, DN_V_HEADS)
    neg_a = jnp.take(-jnp.exp(a_log.astype(F32)), dec_src, mode="fill", fill_value=0)
    dt_b = jnp.take(dt_bias.astype(F32), dec_src, mode="fill", fill_value=0)
    is_beta = jnp.asarray((src < DN_V_HEADS).astype(np.float32))
    par = jnp.stack([neg_a, dt_b, is_beta])
    par = jnp.pad(par, ((0, SUBLANES - par.shape[0]), (0, 0)))
    return w_bg, par


def kernel(x_prompt, x_sample, cache_k, cache_v, state_dn, state_dn_conv, state_ffn_conv, page_table,
           norm_mix, norm_ffn, norm_out, dn_w_in, dn_conv_w, dn_a_log, dn_dt_bias, dn_norm, dn_w_out,
           mb_w_qkv, mb_w_o, ffn_w_in, ffn_conv_w, ffn_conv_b, ffn_w_out):
    bsz, seq, d = x_prompt.shape
    bd = x_sample.shape[0]
    depth = norm_mix.shape[0]
    n_pages = page_table.shape[1]
    past = n_pages * PAGE_SIZE
    own_blk = past // MB_BLOCK
    hd = HEAD_DIM

    yp = x_prompt.reshape(bsz * seq, d)
    ys = x_sample.reshape(bd, d)
    cos_p, sin_p = _rope_tables(jnp.arange(seq, dtype=I32))
    cos_s, sin_s = _rope_tables(jnp.full((bd,), past, dtype=I32))
    pool_k, pool_v = cache_k, cache_v

    kp_l, vp_l, ks_l, vs_l = [], [], [], []
    sp_l, ss_l, cp_l, cs_l = [], [], [], []
    fp_l, fs_l = [], []
    for layer in range(depth):
        j = layer // 2
        g_mix = norm_mix[layer].reshape(1, d)
        if layer % 2 == 0:
            w_main = dn_w_in[j][:, :DN_MAIN_DIM].astype(BF16)
            w_bg, par = _dn_gate_layout(dn_w_in[j], dn_a_log[j], dn_dt_bias[j])
            w_out = dn_w_out[j].astype(BF16)
            nw = dn_norm[j].reshape(1, hd)
            proj, bg, bgt = _dn_inproj(yp, g_mix, w_main, w_bg, par, transposed=True)
            o, s_new = _dn_core(proj, bg, bgt, dn_conv_w[j], nw, bsz, seq)
            yp = _matmul_residual(o, w_out, yp)
            sp_l.append(s_new)
            cp_l.append(proj.reshape(bsz, seq, DN_MAIN_DIM)[:, seq - (DN_CONV - 1):, :DN_CONV_DIM])

            proj_s, bg_s = _dn_inproj(ys, g_mix, w_main, w_bg, par, transposed=False)
            o_s, s_s, c_s = _dn_sample(proj_s, bg_s, state_dn_conv[j], dn_conv_w[j], state_dn[j], nw)
            ys = _matmul_residual(o_s, w_out, ys)
            ss_l.append(s_s)
            cs_l.append(c_s)
        else:
            w_qkv = mb_w_qkv[j].astype(BF16)
            w_o = mb_w_o[j].astype(BF16)
            q, k, v, kb, vt, km = _mb_qkv(yp, g_mix, w_qkv, cos_p, sin_p, with_blocks=True)
            att = _mb_attn(q, kb, vt, km, bsz, seq)
            yp = _matmul_residual(att, w_o, yp)
            kp_l.append(k.reshape(bsz, seq, MB_HEADS, hd))
            vp_l.append(v.reshape(bsz, seq, MB_HEADS, hd))

            q_s, k_s, v_s = _mb_qkv(ys, g_mix, w_qkv, cos_s, sin_s, with_blocks=False)
            blk_sums = _page_block_sums(pool_k, page_table, j)
            q3 = q_s.reshape(bd, MB_HEADS, hd)
            k3 = k_s.reshape(bd, MB_HEADS, hd)
            sel = _mb_sample_select(q3, k3, blk_sums, own_blk)
            att_s = _mb_sample_attn(pool_k, pool_v, page_table, sel, q3, k3, v_s.reshape(bd, MB_HEADS, hd), j)
            ys = _matmul_residual(att_s.astype(BF16), w_o, ys)
            ks_l.append(k_s.reshape(bd, 1, MB_HEADS, hd))
            vs_l.append(v_s.reshape(bd, 1, MB_HEADS, hd))

        g_ffn = norm_ffn[layer].reshape(1, d)
        w_fi = ffn_w_in[layer].astype(BF16)
        w_fo = ffn_w_out[layer].astype(BF16)
        cb = ffn_conv_b[layer].reshape(1, -1)
        yp, fbp = _ffn_prompt(yp, g_ffn, w_fi, ffn_conv_w[layer], cb, w_fo, bsz, seq)
        ys, fbs = _ffn_sample(ys, g_ffn, w_fi, ffn_conv_w[layer], cb, w_fo, state_ffn_conv[layer])
        fp_l.append(fbp)
        fs_l.append(fbs)

    g_out = norm_out.reshape(1, d)
    y_prompt = _final_norm(yp, g_out).reshape(bsz, seq, d)
    y_sample = _final_norm(ys, g_out).reshape(bd, 1, d)
    return (y_prompt, y_sample, jnp.stack(kp_l), jnp.stack(vp_l), jnp.stack(ks_l), jnp.stack(vs_l),
            jnp.stack(sp_l), jnp.stack(ss_l), jnp.stack(cp_l), jnp.stack(cs_l), jnp.stack(fp_l), jnp.stack(fs_l))
```

```python
import functools

import jax
import numpy as np
import jax.numpy as jnp
from jax import lax
from jax.experimental import pallas as pl
from jax.experimental.pallas import tpu as pltpu

F32 = jnp.float32
BF16 = jnp.bfloat16
I32 = jnp.int32

NORM_EPS = 1e-6
ROPE_THETA = 10000.0
HEAD_DIM = 128
DN_QK_HEADS = 8
DN_V_HEADS = 16
DN_CONV = 4
DN_KEY_DIM = DN_QK_HEADS * HEAD_DIM
DN_VAL_DIM = DN_V_HEADS * HEAD_DIM
DN_CONV_DIM = 2 * DN_KEY_DIM + DN_VAL_DIM
DN_MAIN_DIM = DN_CONV_DIM + DN_VAL_DIM
DN_CHUNK = 64
DN_TBLOCK = 128
DN_HEADS_PER_STEP = 8
MB_HEADS = 8
MB_BLOCK = 256
MB_TOP_K = 3
PAGES_PER_STEP = 16
MB_V_PAD = 16
MB_HEADS_PER_STEP = 2
PAGE_SIZE = 128
FFN_CONV = 3
SUBLANES = 8
LANES = 128
NEG_BIG = -1e30
VMEM_LIMIT = 56 * 1024 * 1024


def _cparams(*sem):
    return pltpu.CompilerParams(dimension_semantics=sem, vmem_limit_bytes=VMEM_LIMIT)


def _sigmoid(x):
    return 1.0 / (1.0 + jnp.exp(-x))


def _silu(x):
    return x * _sigmoid(x)


def _softplus(x):
    return jnp.maximum(x, 0.0) + jnp.log1p(jnp.exp(-jnp.abs(x)))


def _rms_rows(x, g):
    ms = jnp.mean(x * x, axis=-1, keepdims=True)
    return x * lax.rsqrt(ms + NORM_EPS) * g


def _dot(a, b):
    return jnp.dot(a, b, preferred_element_type=F32)


def _dot_nt(a, b):
    return lax.dot_general(a, b, (((1,), (1,)), ((), ())), preferred_element_type=F32)


def _dot_tn(a, b):
    return lax.dot_general(a, b, (((0,), (0,)), ((), ())), preferred_element_type=F32)


def _split2(a):
    hi = a.astype(BF16)
    lo = (a - hi.astype(F32)).astype(BF16)
    return hi, lo


def _mm2_pair(a, bh, left):
    ah, al = a
    b_l, b_r = bh * left, bh * (1 - left)
    return _dot(jnp.concatenate([ah, al], axis=1), jnp.concatenate([b_l, b_r, b_l, b_r], axis=0))


def _dot_exact01(m01, x):
    hi = x.astype(BF16)
    r1 = x - hi.astype(F32)
    mid = r1.astype(BF16)
    lo = (r1 - mid.astype(F32)).astype(BF16)
    return _dot(m01, hi) + (_dot(m01, mid) + _dot(m01, lo))


def _dot_exact01_r(x, m01):
    hi = x.astype(BF16)
    r1 = x - hi.astype(F32)
    mid = r1.astype(BF16)
    lo = (r1 - mid.astype(F32)).astype(BF16)
    return _dot(hi, m01) + (_dot(mid, m01) + _dot(lo, m01))


def _dn_inproj_kernel(x_ref, g_ref, w_ref, wbg_ref, par_ref, o_ref, bg_ref, *rest, transposed):
    if transposed:
        bgt_ref, h_scr = rest
    else:
        (h_scr,) = rest

    @pl.when(pl.program_id(1) == 0)
    def _():
        h = _rms_rows(x_ref[...], g_ref[...]).astype(BF16)
        h_scr[...] = h
        raw = _dot(h, wbg_ref[...])
        neg_a, dt_b, is_beta = par_ref[0:1, :], par_ref[1:2, :], par_ref[2:3, :]
        act = jnp.where(is_beta > 0.5, _sigmoid(raw), neg_a * _softplus(raw + dt_b))
        bg_ref[...] = act
        if transposed:
            bgt_ref[...] = act.T

    o_ref[...] = _dot(h_scr[...], w_ref[...])


def _dn_inproj(x, g, w_main, w_bg, par, *, transposed):
    m, d = x.shape
    n = w_main.shape[1]
    tm = min(1024, m)
    tn = 2048
    out_shape = [jax.ShapeDtypeStruct((m, n), F32), jax.ShapeDtypeStruct((m, LANES), F32)]
    out_specs = [pl.BlockSpec((tm, tn), lambda i, j: (i, j)),
                 pl.BlockSpec((tm, LANES), lambda i, j: (i, 0))]
    if transposed:
        out_shape.append(jax.ShapeDtypeStruct((LANES, m), F32))
        out_specs.append(pl.BlockSpec((LANES, tm), lambda i, j: (0, i)))
    return pl.pallas_call(
        functools.partial(_dn_inproj_kernel, transposed=transposed),
        grid=(m // tm, n // tn),
        in_specs=[pl.BlockSpec((tm, d), lambda i, j: (i, 0)),
                  pl.BlockSpec((1, d), lambda i, j: (0, 0)),
                  pl.BlockSpec((d, tn), lambda i, j: (0, j)),
                  pl.BlockSpec((d, LANES), lambda i, j: (0, 0)),
                  pl.BlockSpec((SUBLANES, LANES), lambda i, j: (0, 0))],
        out_specs=out_specs,
        out_shape=out_shape,
        scratch_shapes=[pltpu.VMEM((tm, d), BF16)],
        compiler_params=_cparams("parallel", "arbitrary"),
        name="dn_inproj",
    )(x, g, w_main, w_bg, par)


def _page_sum_kernel(pt_ref, *refs):
    pages, o_ref = refs[:PAGES_PER_STEP], refs[PAGES_PER_STEP]
    per_blk = MB_BLOCK // PAGE_SIZE
    for r in range(PAGES_PER_STEP // per_blk):
        s = jnp.sum(pages[per_blk * r][...], axis=0)
        for i in range(1, per_blk):
            s = s + jnp.sum(pages[per_blk * r + i][...], axis=0)
        o_ref[r] = s


def _unit_lower_inverse_pairs(lows, c):
    row = lax.broadcasted_iota(I32, (c, 2 * c), 0)
    lane = lax.broadcasted_iota(I32, (c, 2 * c), 1)
    col = lane & (c - 1)
    left = jnp.where(lane < c, 1.0, 0.0).astype(BF16)

    def same_block(shift):
        return (row >> shift) == (col >> shift)

    low_parts = [_split2(low) for low in lows]
    base = same_block(3)
    negs = [jnp.where(base, -low, 0.0) for low in lows]
    eye = jnp.where(row == col, 1.0, 0.0)
    invs = [eye + neg for neg in negs]
    powers = [_split2(neg) for neg in negs]
    for _ in range(2):
        powers = [_split2(_mm2_pair(p, p[0], left)) for p in powers]
        invs = [inv + _mm2_pair(_split2(inv), p[0], left) for inv, p in zip(invs, powers)]
    shift = 3
    while (1 << shift) < c:
        mask = same_block(shift + 1) & jnp.logical_not(same_block(shift))
        inv_parts = [_split2(inv) for inv in invs]
        tmps = [jnp.where(mask, _mm2_pair(lp, ip[0], left), 0.0).astype(BF16)
                for lp, ip in zip(low_parts, inv_parts)]
        invs = [inv - _mm2_pair(ip, t, left) for inv, ip, t in zip(invs, inv_parts, tmps)]
        shift += 1
    return invs


def _dn_core_kernel(pt_ref, q_ref, k_ref, v_ref, z_ref, bg_ref, bgt_ref, cwq_ref, cwk_ref, cwv_ref, nw_ref,
                    *rest, tb, chunk, nh):
    pages = rest[:PAGES_PER_STEP]
    o_ref, s_out_ref, psum_ref, xe_scr, s_scr = rest[PAGES_PER_STEP:]
    _page_sum_kernel(pt_ref, *pages, psum_ref)
    hg = pl.program_id(1)
    t = pl.program_id(2)
    hd = HEAD_DIM
    n_chunks = tb // chunk
    k_off, v_off = nh * hd, 2 * nh * hd

    @pl.when(t == 0)
    def _():
        s_scr[...] = jnp.zeros(s_scr.shape, F32)
        xe_scr[0:SUBLANES, :] = jnp.zeros((SUBLANES, xe_scr.shape[1]), F32)

    xe_scr[SUBLANES:SUBLANES + tb, 0:k_off] = q_ref[...]
    xe_scr[SUBLANES:SUBLANES + tb, k_off:v_off] = k_ref[...]
    xe_scr[SUBLANES:SUBLANES + tb, v_off:] = v_ref[...]

    def conv(lo, hi, cw_ref, w_lo):
        w = cw_ref[:, w_lo:w_lo + hi - lo]
        acc = w[DN_CONV - 1:DN_CONV, :] * xe_scr[SUBLANES:SUBLANES + tb, lo:hi]
        for d in range(1, DN_CONV):
            acc = acc + w[DN_CONV - 1 - d:DN_CONV - d, :] * xe_scr[SUBLANES - d:SUBLANES - d + tb, lo:hi]
        return _silu(acc)

    ri = lax.broadcasted_iota(I32, (tb, tb), 0)
    ci = lax.broadcasted_iota(I32, (tb, tb), 1)
    cshift = chunk.bit_length() - 1
    same_chunk = (ri >> cshift) == (ci >> cshift)
    upper = jnp.where(same_chunk & (ri <= ci), 1.0, 0.0).astype(BF16)
    lower = jnp.where(same_chunk & (ri >= ci), 1.0, 0.0).astype(BF16)

    rc = lax.broadcasted_iota(I32, (chunk, 2 * chunk), 0)
    lane = lax.broadcasted_iota(I32, (chunk, 2 * chunk), 1)
    cc = lane & (chunk - 1)
    is_left = lane < chunk
    nw = nw_ref[...]

    items = []
    for g in range(nh):
        qc = conv(g * hd, (g + 1) * hd, cwq_ref, g * hd)
        kc = conv(k_off + g * hd, k_off + (g + 1) * hd, cwk_ref, g * hd)
        vc = conv(v_off + 2 * g * hd, v_off + 2 * (g + 1) * hd, cwv_ref, 2 * g * hd)
        qn = qc * lax.rsqrt(jnp.sum(qc * qc, axis=-1, keepdims=True) + NORM_EPS) * (hd ** -0.5)
        kn = kc * lax.rsqrt(jnp.sum(kc * kc, axis=-1, keepdims=True) + NORM_EPS)
        hq = hg * nh + g
        bgr = pltpu.roll(bg_ref[...], (LANES - SUBLANES * hq) % LANES, axis=1)
        bgt = bgt_ref[g * SUBLANES:(g + 1) * SUBLANES, :]
        gc_rows = _dot_exact01_r(bgt, upper)
        gc_cols = _dot_exact01(lower, bgr)
        for c in range(n_chunks):
            r0, r1 = c * chunk, (c + 1) * chunk
            q_c, k_c = qn[r0:r1], kn[r0:r1]
            qb, kb = q_c.astype(BF16), k_c.astype(BF16)
            prods = _dot_nt(jnp.concatenate([qb, kb], axis=0), jnp.concatenate([kb, kb], axis=0))
            qk2, kk2 = prods[:chunk], prods[chunk:]
            g_cols = [gc_cols[r0:r1, 2 + x:3 + x] for x in range(2)]
            b_cols = [bgr[r0:r1, x:x + 1] for x in range(2)]
            g_row2 = jnp.concatenate([gc_rows[2 + x:3 + x, r0:r1] for x in range(2)], axis=1)
            g_col2 = jnp.where(is_left, g_cols[0], g_cols[1])
            b_col2 = jnp.where(is_left, b_cols[0], b_cols[1])
            decay2 = jnp.exp(jnp.where(rc >= cc, g_col2 - g_row2, -jnp.inf))
            heads = []
            for x in range(2):
                eg = jnp.exp(g_cols[x])
                g_last = g_cols[x][chunk - 1:chunk, :]
                v_c = vc[r0:r1, x * hd:(x + 1) * hd]
                heads.append(dict(
                    rhs=jnp.concatenate([b_cols[x] * v_c, (b_cols[x] * eg) * k_c], axis=1).astype(BF16),
                    qg=(q_c * eg).astype(BF16),
                    kd=(k_c * jnp.exp(g_last - g_cols[x])).astype(BF16),
                    gl=jnp.exp(g_last)))
            items.append(dict(g=g, c=c, low=jnp.where(rc > cc, b_col2 * decay2 * kk2, 0.0),
                              qkd=(qk2 * decay2).astype(BF16), heads=heads))
    xe_scr[0:SUBLANES, :] = xe_scr[tb:tb + SUBLANES, :]

    invs = _unit_lower_inverse_pairs([it["low"] for it in items], chunk)
    for inv, it in zip(invs, items):
        ra, rb = it["heads"][0]["rhs"], it["heads"][1]["rhs"]
        zero = jnp.zeros_like(ra)
        rhs = jnp.concatenate([jnp.concatenate([ra, zero], axis=1),
                               jnp.concatenate([zero, rb], axis=1)], axis=0)
        it["sol"] = _dot(inv.astype(BF16), rhs)

    states = [s_scr[h] for h in range(2 * nh)]
    for c in range(n_chunks):
        r0, r1 = c * chunk, (c + 1) * chunk
        row = [it for it in items if it["c"] == c]
        sbs = [states[2 * it["g"] + x].astype(BF16) for it in row for x in range(2)]
        ws_qs = [_dot(jnp.concatenate([it["sol"][:, (2 * x + 1) * hd:(2 * x + 2) * hd].astype(BF16),
                                       it["heads"][x]["qg"]], axis=0), sbs[2 * i + x])
                 for i, it in enumerate(row) for x in range(2)]
        ubs = [(it["sol"][:, 2 * x * hd:(2 * x + 1) * hd] - ws_qs[2 * i + x][:chunk]).astype(BF16)
               for i, it in enumerate(row) for x in range(2)]
        intras = []
        for i, it in enumerate(row):
            zero = jnp.zeros_like(ubs[0])
            u_diag = jnp.concatenate([jnp.concatenate([ubs[2 * i], zero], axis=1),
                                      jnp.concatenate([zero, ubs[2 * i + 1]], axis=1)], axis=0)
            intras.append(_dot(it["qkd"], u_diag))
        for i, it in enumerate(row):
            for x in range(2):
                h = 2 * it["g"] + x
                head = it["heads"][x]
                states[h] = head["gl"] * states[h] + _dot_tn(head["kd"], ubs[2 * i + x])
                o = ws_qs[2 * i + x][chunk:] + intras[i][:, x * hd:(x + 1) * hd]
                z_c = z_ref[r0:r1, h * hd:(h + 1) * hd]
                on = o * lax.rsqrt(jnp.mean(o * o, axis=-1, keepdims=True) + NORM_EPS) * nw * _silu(z_c)
                o_ref[r0:r1, h * hd:(h + 1) * hd] = on.astype(o_ref.dtype)
    for h in range(2 * nh):
        s_scr[h] = states[h]

    @pl.when(t == pl.num_programs(2) - 1)
    def _():
        s_out_ref[...] = s_scr[...]


def _dn_core(proj, bg, bgt, conv_w, norm_w, bsz, seq, pool, page_table, layer):
    tb, hd, nh = DN_TBLOCK, HEAD_DIM, DN_HEADS_PER_STEP
    nt = seq // tb
    ng = DN_QK_HEADS // nh
    bd, n_pages = page_table.shape
    heads = pool.shape[-2]
    groups = n_pages // PAGES_PER_STEP
    rows = PAGES_PER_STEP * PAGE_SIZE // MB_BLOCK
    assert bd * groups <= bsz * ng * nt, "at most one page group per delta-rule grid step"
    kernel = functools.partial(_dn_core_kernel, tb=tb, chunk=DN_CHUNK, nh=nh)
    tok = lambda b, h, t: b * nt + t
    step = lambda b, h, t: jnp.minimum((b * ng + h) * nt + t, bd * groups - 1)

    def page_spec(i):
        def imap(b, h, t, pt):
            s = step(b, h, t)
            return (layer, pt[(s // groups) * n_pages + (s % groups) * PAGES_PER_STEP + i], 0, 0, 0)
        return pl.BlockSpec((None, None, PAGE_SIZE, heads, hd), imap)

    return pl.pallas_call(
        kernel,
        grid_spec=pltpu.PrefetchScalarGridSpec(
            num_scalar_prefetch=1,
            grid=(bsz, ng, nt),
            in_specs=[
                pl.BlockSpec((tb, nh * hd), lambda b, h, t, pt: (tok(b, h, t), h)),
                pl.BlockSpec((tb, nh * hd), lambda b, h, t, pt: (tok(b, h, t), ng + h)),
                pl.BlockSpec((tb, 2 * nh * hd), lambda b, h, t, pt: (tok(b, h, t), ng + h)),
                pl.BlockSpec((tb, 2 * nh * hd), lambda b, h, t, pt: (tok(b, h, t), 2 * ng + h)),
                pl.BlockSpec((tb, LANES), lambda b, h, t, pt: (tok(b, h, t), 0)),
                pl.BlockSpec((nh * SUBLANES, tb), lambda b, h, t, pt: (h, tok(b, h, t))),
                pl.BlockSpec((DN_CONV, nh * hd), lambda b, h, t, pt: (0, h)),
                pl.BlockSpec((DN_CONV, nh * hd), lambda b, h, t, pt: (0, ng + h)),
                pl.BlockSpec((DN_CONV, 2 * nh * hd), lambda b, h, t, pt: (0, ng + h)),
                pl.BlockSpec((1, hd), lambda b, h, t, pt: (0, 0)),
            ] + [page_spec(i) for i in range(PAGES_PER_STEP)],
            out_specs=[
                pl.BlockSpec((tb, 2 * nh * hd), lambda b, h, t, pt: (tok(b, h, t), h)),
                pl.BlockSpec((None, 2 * nh, hd, hd), lambda b, h, t, pt: (b, h, 0, 0)),
                pl.BlockSpec((None, rows, heads, hd),
                             lambda b, h, t, pt: (step(b, h, t) // groups, step(b, h, t) % groups, 0, 0)),
            ],
            scratch_shapes=[pltpu.VMEM((tb + SUBLANES, 4 * nh * hd), F32),
                            pltpu.VMEM((2 * nh, hd, hd), F32)],
        ),
        out_shape=[jax.ShapeDtypeStruct((bsz * seq, DN_VAL_DIM), BF16),
                   jax.ShapeDtypeStruct((bsz, DN_V_HEADS, hd, hd), F32),
                   jax.ShapeDtypeStruct((bd, groups * rows, heads, hd), F32)],
        compiler_params=_cparams("parallel", "parallel", "arbitrary"),
        name="dn_core",
    )(page_table.reshape(-1), proj, proj, proj, proj, bg, bgt, conv_w, conv_w, conv_w, norm_w,
      *([pool] * PAGES_PER_STEP))


def _dn_sample_kernel(p_ref, bg_ref, cst_ref, cw_ref, s_ref, nw_ref,
                      o_ref, s_out_ref, cst_out_ref, r_scr):
    hd = HEAD_DIM
    x = p_ref[:, 0:DN_CONV_DIM]
    buf = cst_ref[...]
    y = x * cw_ref[DN_CONV - 1:DN_CONV, :]
    for i in range(DN_CONV - 1):
        y = y + buf[i:i + 1, :] * cw_ref[i:i + 1, :]
    cst_out_ref[0:DN_CONV - 2, :] = buf[1:DN_CONV - 1, :]
    cst_out_ref[DN_CONV - 2:DN_CONV - 1, :] = x
    a = _silu(y)

    r_scr[...] = jnp.zeros(r_scr.shape, F32)
    qs, ks = [], []
    for h in range(DN_QK_HEADS):
        qh = a[:, h * hd:(h + 1) * hd]
        kh = a[:, DN_KEY_DIM + h * hd:DN_KEY_DIM + (h + 1) * hd]
        qh = qh * lax.rsqrt(jnp.sum(qh * qh, axis=-1, keepdims=True) + NORM_EPS) * (hd ** -0.5)
        kh = kh * lax.rsqrt(jnp.sum(kh * kh, axis=-1, keepdims=True) + NORM_EPS)
        qs.append(qh)
        ks.append(kh)
        r_scr[h:h + 1, :] = qh
        r_scr[DN_QK_HEADS + h:DN_QK_HEADS + h + 1, :] = kh
    rt = r_scr[...].T

    bg = bg_ref[...]
    nw = nw_ref[...]
    rep = DN_V_HEADS // DN_QK_HEADS
    for h in range(DN_V_HEADS):
        hq, x_in = h // rep, h % rep
        beta = bg[:, SUBLANES * hq + x_in:SUBLANES * hq + x_in + 1]
        g = bg[:, SUBLANES * hq + 2 + x_in:SUBLANES * hq + 3 + x_in]
        eg = jnp.exp(g)
        q_col = rt[:, hq:hq + 1]
        k_col = rt[:, DN_QK_HEADS + hq:DN_QK_HEADS + hq + 1]
        s_old = s_ref[h]
        k_s = jnp.sum(k_col * s_old, axis=0, keepdims=True)
        q_s = jnp.sum(q_col * s_old, axis=0, keepdims=True)
        v_h = a[:, 2 * DN_KEY_DIM + h * hd:2 * DN_KEY_DIM + (h + 1) * hd]
        u = beta * v_h - (beta * eg) * k_s
        qk = jnp.sum(qs[hq] * ks[hq], axis=-1, keepdims=True)
        o = eg * q_s + qk * u
        s_out_ref[h] = eg * s_old + k_col * u
        z_h = p_ref[:, DN_CONV_DIM + h * hd:DN_CONV_DIM + (h + 1) * hd]
        on = o * lax.rsqrt(jnp.mean(o * o, axis=-1, keepdims=True) + NORM_EPS) * nw * _silu(z_h)
        o_ref[:, h * hd:(h + 1) * hd] = on.astype(o_ref.dtype)


def _dn_sample(proj, bg, conv_state, conv_w, state, norm_w):
    bd = proj.shape[0]
    hd = HEAD_DIM
    o, s_new, c_new = pl.pallas_call(
        _dn_sample_kernel,
        grid=(bd,),
        in_specs=[
            pl.BlockSpec((None, 1, DN_MAIN_DIM), lambda b: (b, 0, 0)),
            pl.BlockSpec((None, 1, LANES), lambda b: (b, 0, 0)),
            pl.BlockSpec((None, DN_CONV - 1, DN_CONV_DIM), lambda b: (b, 0, 0)),
            pl.BlockSpec((DN_CONV, DN_CONV_DIM), lambda b: (0, 0)),
            pl.BlockSpec((None, DN_V_HEADS, hd, hd), lambda b: (b, 0, 0, 0)),
            pl.BlockSpec((1, hd), lambda b: (0, 0)),
        ],
        out_specs=[
            pl.BlockSpec((None, 1, DN_VAL_DIM), lambda b: (b, 0, 0)),
            pl.BlockSpec((None, DN_V_HEADS, hd, hd), lambda b: (b, 0, 0, 0)),
            pl.BlockSpec((None, DN_CONV - 1, DN_CONV_DIM), lambda b: (b, 0, 0)),
        ],
        out_shape=[jax.ShapeDtypeStruct((bd, 1, DN_VAL_DIM), BF16),
                   jax.ShapeDtypeStruct(state.shape, F32),
                   jax.ShapeDtypeStruct(conv_state.shape, F32)],
        scratch_shapes=[pltpu.VMEM((LANES, LANES), F32)],
        compiler_params=_cparams("parallel"),
        name="dn_sample",
    )(proj.reshape(bd, 1, DN_MAIN_DIM), bg.reshape(bd, 1, LANES), conv_state, conv_w, state, norm_w)
    return o.reshape(bd, DN_VAL_DIM), s_new, c_new


def _matmul_residual_kernel(a_ref, w_ref, y_ref, o_ref):
    o_ref[...] = y_ref[...] + _dot(a_ref[...], w_ref[...])


def _matmul_residual(a, w, y):
    m, k = a.shape
    n = w.shape[1]
    tm = min(512, m)
    return pl.pallas_call(
        _matmul_residual_kernel,
        grid=(m // tm,),
        in_specs=[pl.BlockSpec((tm, k), lambda i: (i, 0)),
                  pl.BlockSpec((k, n), lambda i: (0, 0)),
                  pl.BlockSpec((tm, n), lambda i: (i, 0))],
        out_specs=pl.BlockSpec((tm, n), lambda i: (i, 0)),
        out_shape=jax.ShapeDtypeStruct((m, n), F32),
        compiler_params=_cparams("parallel"),
        name="matmul_residual",
    )(a, w, y)


def _mb_qkv_kernel(x_ref, g_ref, w_ref, cos_ref, sin_ref, q_ref, k_ref, v_ref, *rest,
                   with_blocks, blocks_per_seq):
    if with_blocks:
        kb_ref, vt_ref, km_ref, h_scr = rest
    else:
        (h_scr,) = rest
    i, j = pl.program_id(0), pl.program_id(1)
    hd = HEAD_DIM

    @pl.when(j == 0)
    def _():
        h_scr[...] = _rms_rows(x_ref[...], g_ref[...]).astype(BF16)

    y = _dot(h_scr[...], w_ref[...])
    tm = y.shape[0]

    def rope(val):
        cos, sin = cos_ref[...], sin_ref[...]
        parts = []
        for h in range(MB_HEADS):
            yh = val[:, h * hd:(h + 1) * hd]
            parts.append(yh * cos + pltpu.roll(yh, hd // 2, axis=1) * sin)
        return jnp.concatenate(parts, axis=1)

    @pl.when(j == 0)
    def _():
        q_ref[...] = rope(y).astype(q_ref.dtype)

    @pl.when(j == 1)
    def _():
        k = rope(y)
        k_ref[...] = k
        if with_blocks:
            row = lax.broadcasted_iota(I32, (tm, LANES), 0)
            lane = lax.broadcasted_iota(I32, (tm, LANES), 1)
            first_blk = (i * (tm // MB_BLOCK)) % blocks_per_seq
            onehot = jnp.where(lane == first_blk + row // MB_BLOCK, 1.0, 0.0).astype(BF16)
            for h in range(MB_HEADS):
                kb_ref[:, 2 * h * hd:(2 * h + 1) * hd] = k[:, h * hd:(h + 1) * hd].astype(BF16)
                kb_ref[:, (2 * h + 1) * hd:(2 * h + 2) * hd] = onehot
            for r in range(tm // MB_BLOCK):
                blk = k[r * MB_BLOCK:(r + 1) * MB_BLOCK, :]
                km_ref[r] = jnp.sum(blk, axis=0, keepdims=True) * (1.0 / MB_BLOCK)

    @pl.when(j == 2)
    def _():
        v_ref[...] = y
        if with_blocks:
            ones = jnp.ones((MB_V_PAD, MB_BLOCK), BF16)
            rows = hd + MB_V_PAD
            for r in range(tm // MB_BLOCK):
                vt = y[r * MB_BLOCK:(r + 1) * MB_BLOCK, :].T.astype(BF16)
                for h in range(MB_HEADS):
                    vt_ref[r, h * rows:h * rows + hd, :] = vt[h * hd:(h + 1) * hd, :]
                    vt_ref[r, h * rows + hd:(h + 1) * rows, :] = ones


def _mb_qkv(x, g, w, cos, sin, *, with_blocks):
    m, d = x.shape
    tm = min(512, m)
    n_pos_tiles = cos.shape[0] // tm
    out_shape = [jax.ShapeDtypeStruct((m, d), BF16 if with_blocks else F32),
                 jax.ShapeDtypeStruct((m, d), F32), jax.ShapeDtypeStruct((m, d), F32)]
    row = pl.BlockSpec((tm, d), lambda i, j: (i, 0))
    out_specs = [row, row, row]
    if with_blocks:
        nb = tm // MB_BLOCK
        vt_rows = MB_HEADS * (HEAD_DIM + MB_V_PAD)
        out_shape += [jax.ShapeDtypeStruct((m, 2 * d), BF16),
                      jax.ShapeDtypeStruct((m // MB_BLOCK, vt_rows, MB_BLOCK), BF16),
                      jax.ShapeDtypeStruct((m // MB_BLOCK, 1, d), F32)]
        out_specs += [pl.BlockSpec((tm, 2 * d), lambda i, j: (i, 0)),
                      pl.BlockSpec((nb, vt_rows, MB_BLOCK), lambda i, j: (i, 0, 0)),
                      pl.BlockSpec((nb, 1, d), lambda i, j: (i, 0, 0))]
    return pl.pallas_call(
        functools.partial(_mb_qkv_kernel, with_blocks=with_blocks,
                          blocks_per_seq=cos.shape[0] // MB_BLOCK if with_blocks else 0),
        grid=(m // tm, 3),
        in_specs=[row,
                  pl.BlockSpec((1, d), lambda i, j: (0, 0)),
                  pl.BlockSpec((d, d), lambda i, j: (0, j)),
                  pl.BlockSpec((tm, HEAD_DIM), lambda i, j: (i % n_pos_tiles, 0)),
                  pl.BlockSpec((tm, HEAD_DIM), lambda i, j: (i % n_pos_tiles, 0))],
        out_specs=out_specs,
        out_shape=out_shape,
        scratch_shapes=[pltpu.VMEM((tm, d), BF16)],
        compiler_params=_cparams("parallel", "arbitrary"),
        name="mb_qkv",
    )(x, g, w, cos, sin)


def _mb_attn_kernel(q_ref, k_ref, vt_ref, km_ref, o_ref, *logit_bufs, heads):
    n = pl.program_id(2)
    bq, hd = MB_BLOCK, HEAD_DIM
    vrows = hd + MB_V_PAD
    c = (hd ** -0.5) * 1.4426950408889634
    nb = km_ref.shape[0]
    blk = lax.broadcasted_iota(I32, (nb, bq), 0)
    kpos = lax.broadcasted_iota(I32, (bq, bq), 0)
    qpos = lax.broadcasted_iota(I32, (bq, bq), 1)
    own_rows = pl.ds(pl.multiple_of(n * bq, bq), bq)

    qs = [q_ref[:, h * hd:(h + 1) * hd] for h in range(heads)]
    sts = [jnp.where(kpos <= qpos, _dot_nt(k_ref[own_rows, 2 * h * hd:(2 * h + 1) * hd], qs[h]), NEG_BIG)
           for h in range(heads)]
    scores = [_dot_nt(km_ref[:, h * hd:(h + 1) * hd].astype(BF16), qs[h]) for h in range(heads)]
    carry = []
    for h in range(heads):
        m = jnp.max(sts[h], axis=0, keepdims=True)
        p = jnp.exp2((sts[h] - m) * c)
        carry += [m, _dot(vt_ref[n, h * vrows:(h + 1) * vrows, :], p.astype(BF16))]

    qas = []
    for h in range(heads):
        s = jnp.where(blk < n, scores[h], -jnp.inf)
        bias = jnp.full((nb, bq), NEG_BIG, F32)
        for _ in range(MB_TOP_K):
            top = jnp.max(s, axis=0, keepdims=True)
            idx = jnp.min(jnp.where(s == top, blk, nb), axis=0, keepdims=True)
            hit = blk == idx
            bias = jnp.where(hit & (idx < n), 0.0, bias)
            s = jnp.where(hit, -jnp.inf, s)
        bias = jnp.concatenate([bias, jnp.full((LANES - nb, bq), NEG_BIG, F32)], axis=0)
        qas.append(jnp.concatenate([qs[h], bias.T.astype(BF16)], axis=1))

    sets_x = [logit_bufs[8 * h:8 * h + 4] for h in range(heads)]
    sets_y = [logit_bufs[8 * h + 4:8 * h + 8] for h in range(heads)]

    def fill(sets, j0):
        for h, bufs in enumerate(sets):
            for i, buf in enumerate(bufs):
                j = jnp.minimum(j0 + i, nb - 1)
                rows = pl.ds(pl.multiple_of(j * bq, bq), bq)
                buf[...] = _dot_nt(k_ref[rows, 2 * h * hd:(2 * h + 2) * hd], qas[h])

    def consume(sets, j0, carry):
        out = []
        for h, bufs in enumerate(sets):
            m, acc = carry[2 * h], carry[2 * h + 1]
            m_new = m
            for buf in bufs:
                m_new = jnp.maximum(m_new, jnp.max(buf[...], axis=0, keepdims=True))
            p = jnp.concatenate([jnp.exp2((buf[...] - m_new) * c).astype(BF16) for buf in bufs], axis=0)
            vt = jnp.concatenate([vt_ref[j0 + i, h * vrows:(h + 1) * vrows, :]
                                  for i in range(len(bufs))], axis=1)
            out += [m_new, acc * jnp.exp2((m - m_new) * c) + _dot(vt, p)]
        return tuple(out)

    def body(t, carry):
        j = 8 * t
        fill(sets_y, j + 4)
        carry = consume(sets_x, j, carry)
        fill(sets_x, j + 8)
        return consume(sets_y, j + 4, carry)

    half_trips = (n + 3) // 4
    fill(sets_x, 0)
    carry = lax.fori_loop(0, half_trips // 2, body, tuple(carry))
    carry = lax.cond(half_trips % 2 == 1,
                     lambda: consume(sets_x, 8 * (half_trips // 2), carry),
                     lambda: carry)
    for h in range(heads):
        acc = carry[2 * h + 1]
        o_ref[:, h * hd:(h + 1) * hd] = (acc[0:hd, :] / acc[hd:hd + 1, :]).T.astype(o_ref.dtype)


def _mb_attn(q, kb, vt, km, bsz, seq):
    hd, bq, heads = HEAD_DIM, MB_BLOCK, MB_HEADS_PER_STEP
    nb = seq // bq
    d = MB_HEADS * hd
    assert nb % 4 == 0
    return pl.pallas_call(
        functools.partial(_mb_attn_kernel, heads=heads),
        grid=(bsz, MB_HEADS // heads, nb),
        in_specs=[
            pl.BlockSpec((bq, heads * hd), lambda b, h, n: (b * nb + n, h)),
            pl.BlockSpec((seq, 2 * heads * hd), lambda b, h, n: (b, h)),
            pl.BlockSpec((nb, heads * (hd + MB_V_PAD), bq), lambda b, h, n: (b, h, 0)),
            pl.BlockSpec((nb, None, heads * hd), lambda b, h, n: (b, 0, h)),
        ],
        out_specs=pl.BlockSpec((bq, heads * hd), lambda b, h, n: (b * nb + n, h)),
        out_shape=jax.ShapeDtypeStruct((bsz * seq, d), BF16),
        scratch_shapes=[pltpu.VMEM((bq, bq), F32)] * (8 * heads),
        compiler_params=_cparams("parallel", "parallel", "arbitrary"),
        name="mb_attn",
    )(q, kb, vt, km)


def _mb_sample_select_kernel(q_ref, kn_ref, bs_ref, sel_ref, *, own):
    nb = bs_ref.shape[0]
    n_rows = nb + SUBLANES
    inv_blk = 1.0 / MB_BLOCK
    q = q_ref[...]
    sc = jnp.sum(bs_ref[...] * inv_blk * q[None], axis=-1, keepdims=True)
    own_sc = jnp.sum(kn_ref[...] * inv_blk * q, axis=-1, keepdims=True)
    tail_row = lax.broadcasted_iota(I32, (SUBLANES, MB_HEADS, 1), 0)
    tail = jnp.where(tail_row == 0, own_sc[None], -jnp.inf)
    blk = lax.broadcasted_iota(I32, (n_rows, MB_HEADS, 1), 0)
    s = jnp.where(blk < own, jnp.concatenate([sc, tail], axis=0), -jnp.inf)
    out_lane = lax.broadcasted_iota(I32, (MB_HEADS, LANES), 1)
    out = jnp.zeros((MB_HEADS, LANES), I32)
    for r in range(MB_TOP_K):
        m = jnp.max(s, axis=0, keepdims=True)
        idx = jnp.min(jnp.where(s == m, blk, n_rows), axis=0, keepdims=True)
        s = jnp.where(blk == idx, -jnp.inf, s)
        idx = idx[0]
        out = jnp.where(out_lane == r, idx, out)
        out = jnp.where(out_lane == MB_TOP_K + 1 + r, (idx < own).astype(I32), out)
    sel_ref[...] = out


def _mb_sample_select(q, k_new, blk_sums, own):
    bd = q.shape[0]
    nb = blk_sums.shape[1]
    return pl.pallas_call(
        functools.partial(_mb_sample_select_kernel, own=own),
        grid=(bd,),
        in_specs=[pl.BlockSpec((None, MB_HEADS, HEAD_DIM), lambda b: (b, 0, 0)),
                  pl.BlockSpec((None, MB_HEADS, HEAD_DIM), lambda b: (b, 0, 0)),
                  pl.BlockSpec((None, nb, MB_HEADS, HEAD_DIM), lambda b: (b, 0, 0, 0))],
        out_specs=pl.BlockSpec((None, MB_HEADS, LANES), lambda b: (b, 0, 0)),
        out_shape=jax.ShapeDtypeStruct((bd, MB_HEADS, LANES), I32),
        compiler_params=_cparams("parallel"),
        name="mb_sample_select",
    )(q, k_new, blk_sums)


def _mb_sample_attn_kernel(pt_ref, sel_ref, q_ref, kn_ref, vn_ref, pool_k, pool_v, o_ref,
                           kbuf, vbuf, sem, *, layer, n_pages):
    per_blk = MB_BLOCK // PAGE_SIZE
    n_pg = MB_TOP_K * per_blk
    b, h = pl.program_id(0), pl.program_id(1)
    n_heads = pl.num_programs(1)
    step = b * n_heads + h
    slot = step % 2
    scale = HEAD_DIM ** -0.5

    def copies(bb, hh, sl):
        base = (bb * MB_HEADS + hh) * SUBLANES
        out = []
        for r in range(MB_TOP_K):
            blk = jnp.minimum(sel_ref[base + r], n_pages // per_blk - 1)
            for half in range(per_blk):
                i = r * per_blk + half
                page = pt_ref[bb * n_pages + blk * per_blk + half]
                out.append(pltpu.make_async_copy(pool_k.at[layer, page, :, hh, :], kbuf.at[sl, i],
                                                 sem.at[sl, 0, i]))
                out.append(pltpu.make_async_copy(pool_v.at[layer, page, :, hh, :], vbuf.at[sl, i],
                                                 sem.at[sl, 1, i]))
        return out

    @pl.when(step == 0)
    def _():
        for cp in copies(b, h, slot):
            cp.start()

    @pl.when(step + 1 < pl.num_programs(0) * n_heads)
    def _():
        wrap = h + 1 == n_heads
        for cp in copies(jnp.where(wrap, b + 1, b), jnp.where(wrap, 0, h + 1), 1 - slot):
            cp.start()

    for cp in copies(b, h, slot):
        cp.wait()

    q = q_ref[...]
    q8 = jnp.broadcast_to(q, (SUBLANES, HEAD_DIM)).astype(BF16)
    keys = jnp.concatenate([kbuf[slot, i] for i in range(n_pg)], axis=0).astype(BF16)
    vals = jnp.concatenate([vbuf[slot, i] for i in range(n_pg)], axis=0).astype(BF16)
    lg = _dot_nt(q8, keys) * scale
    lane = lax.broadcasted_iota(I32, lg.shape, 1)
    base = (b * MB_HEADS + h) * SUBLANES
    for r in range(MB_TOP_K):
        penalty = jnp.where(sel_ref[base + MB_TOP_K + 1 + r] == 0, NEG_BIG, 0.0)
        in_blk = (lane >= r * MB_BLOCK) & (lane < (r + 1) * MB_BLOCK)
        lg = jnp.where(in_blk, lg + penalty, lg)
    qb = q.astype(BF16).astype(F32)
    lo = jnp.sum(qb * kn_ref[...].astype(BF16).astype(F32), axis=-1, keepdims=True) * scale
    m = jnp.maximum(jnp.max(lg, axis=-1, keepdims=True), lo)
    p = jnp.exp(lg - m)
    po = jnp.exp(lo - m)
    l = jnp.sum(p, axis=-1, keepdims=True) + po
    vn = vn_ref[...].astype(BF16).astype(F32)
    out = (_dot(p.astype(BF16), vals) + po.astype(BF16).astype(F32) * vn) / l
    o_ref[...] = out[0:1, :]


def _mb_sample_attn(pool_k, pool_v, page_table, sel, q, k_new, v_new, layer):
    bd, n_pages = page_table.shape
    hd = HEAD_DIM
    n_pg = MB_TOP_K * (MB_BLOCK // PAGE_SIZE)
    vec = pl.BlockSpec((None, None, 1, hd), lambda b, h, pt, sl: (b, h, 0, 0))
    hbm = pl.BlockSpec(memory_space=pl.ANY)
    q4 = q.reshape(bd, MB_HEADS, 1, hd)
    out = pl.pallas_call(
        functools.partial(_mb_sample_attn_kernel, layer=layer, n_pages=n_pages),
        grid_spec=pltpu.PrefetchScalarGridSpec(
            num_scalar_prefetch=2,
            grid=(bd, MB_HEADS),
            in_specs=[vec, vec, vec, hbm, hbm],
            out_specs=vec,
            scratch_shapes=[pltpu.VMEM((2, n_pg, PAGE_SIZE, hd), F32),
                            pltpu.VMEM((2, n_pg, PAGE_SIZE, hd), F32),
                            pltpu.SemaphoreType.DMA((2, 2, n_pg))],
        ),
        out_shape=jax.ShapeDtypeStruct((bd, MB_HEADS, 1, hd), F32),
        compiler_params=_cparams("arbitrary", "arbitrary"),
        name="mb_sample_attn",
    )(page_table.reshape(-1), sel[:, :, :SUBLANES].reshape(-1), q4, k_new.reshape(q4.shape), v_new.reshape(q4.shape),
      pool_k, pool_v)
    return out.reshape(bd, MB_HEADS * hd)


def _ffn_kernel(y_ref, g_ref, wg_ref, wv_ref, cwg_ref, cwv_ref, cbg_ref, cbv_ref, wo_ref,
                out_ref, bufg_ref, bufv_ref, h_scr, ug_scr, uv_scr, cg_scr, cv_scr, acc_scr,
                *, tiles_per_seq):
    i, f = pl.program_id(0), pl.program_id(1)
    tm = y_ref.shape[0]
    first = (i % tiles_per_seq) == 0

    @pl.when(f == 0)
    def _():
        h_scr[...] = _rms_rows(y_ref[...], g_ref[...]).astype(BF16)
        acc_scr[...] = jnp.zeros(acc_scr.shape, F32)

    @pl.when(first)
    def _():
        cg_scr[f] = jnp.zeros(cg_scr.shape[1:], F32)
        cv_scr[f] = jnp.zeros(cv_scr.shape[1:], F32)

    def branch(w_ref, cw_ref, cb_ref, u_scr, carry, buf_ref):
        u = _dot(h_scr[...], w_ref[...])
        u_scr[0:SUBLANES, :] = carry[f]
        u_scr[SUBLANES:SUBLANES + tm, :] = u
        r = cw_ref[FFN_CONV - 1:FFN_CONV, :] * u + cb_ref[...]
        for d in range(1, FFN_CONV):
            r = r + cw_ref[FFN_CONV - 1 - d:FFN_CONV - d, :] * u_scr[SUBLANES - d:SUBLANES - d + tm, :]
        last = u_scr[tm:tm + SUBLANES, :]
        carry[f] = last
        buf_ref[...] = last
        return r

    gate = branch(wg_ref, cwg_ref, cbg_ref, ug_scr, cg_scr, bufg_ref)
    val = branch(wv_ref, cwv_ref, cbv_ref, uv_scr, cv_scr, bufv_ref)
    act = (_silu(gate) * val).astype(BF16)
    acc_scr[...] += _dot(act, wo_ref[...])

    @pl.when(f == pl.num_programs(1) - 1)
    def _():
        out_ref[...] = y_ref[...] + acc_scr[...]


def _ffn_tile(dff):
    for cand in (1408, 1024, 512, 256, 128):
        if dff % cand == 0:
            return cand
    raise ValueError(f"unsupported d_ff {dff}")


def _ffn_prompt(y, g, w_in, conv_w, conv_b, w_out, bsz, seq):
    m, d = y.shape
    dff = w_out.shape[0]
    tm = min(512, seq)
    tf = _ffn_tile(dff)
    nf = dff // tf
    tiles_per_seq = seq // tm
    gate = lambda i, f: (0, f)
    val = lambda i, f: (0, f + nf)
    buf_spec = pl.BlockSpec((None, SUBLANES, tf), lambda i, f: (i, 0, f))
    out, bufg, bufv = pl.pallas_call(
        functools.partial(_ffn_kernel, tiles_per_seq=tiles_per_seq),
        grid=(m // tm, nf),
        in_specs=[pl.BlockSpec((tm, d), lambda i, f: (i, 0)),
                  pl.BlockSpec((1, d), lambda i, f: (0, 0)),
                  pl.BlockSpec((d, tf), gate), pl.BlockSpec((d, tf), val),
                  pl.BlockSpec((FFN_CONV, tf), gate), pl.BlockSpec((FFN_CONV, tf), val),
                  pl.BlockSpec((1, tf), gate), pl.BlockSpec((1, tf), val),
                  pl.BlockSpec((tf, d), lambda i, f: (f, 0))],
        out_specs=[pl.BlockSpec((tm, d), lambda i, f: (i, 0)), buf_spec, buf_spec],
        out_shape=[jax.ShapeDtypeStruct((m, d), F32),
                   jax.ShapeDtypeStruct((m // tm, SUBLANES, dff), F32),
                   jax.ShapeDtypeStruct((m // tm, SUBLANES, dff), F32)],
        scratch_shapes=[pltpu.VMEM((tm, d), BF16),
                        pltpu.VMEM((tm + SUBLANES, tf), F32), pltpu.VMEM((tm + SUBLANES, tf), F32),
                        pltpu.VMEM((nf, SUBLANES, tf), F32), pltpu.VMEM((nf, SUBLANES, tf), F32),
                        pltpu.VMEM((tm, d), F32)],
        compiler_params=_cparams("arbitrary", "arbitrary"),
        name="ffn_prompt",
    )(y, g, w_in, w_in, conv_w, conv_w, conv_b, conv_b, w_out)
    keep = FFN_CONV - 1
    last = slice(tiles_per_seq - 1, None, tiles_per_seq)
    new_buf = jnp.concatenate([bufg[last, SUBLANES - keep:], bufv[last, SUBLANES - keep:]], axis=-1)
    return out, new_buf


def _ffn_sample_kernel(y_ref, g_ref, wg_ref, wv_ref, cwg_ref, cwv_ref, cbg_ref, cbv_ref,
                       sg_ref, sv_ref, wo_ref, out_ref, ng_ref, nv_ref, h_scr, acc_scr):
    f = pl.program_id(0)

    @pl.when(f == 0)
    def _():
        h_scr[...] = _rms_rows(y_ref[...], g_ref[...]).astype(BF16)
        acc_scr[...] = jnp.zeros(acc_scr.shape, F32)

    def branch(w_ref, cw_ref, cb_ref, st_ref, new_ref):
        u = _dot(h_scr[...], w_ref[...])
        r = cw_ref[FFN_CONV - 1:FFN_CONV, :] * u + cb_ref[...]
        for i in range(FFN_CONV - 1):
            r = r + cw_ref[i:i + 1, :] * st_ref[i]
        for i in range(FFN_CONV - 2):
            new_ref[i] = st_ref[i + 1]
        new_ref[FFN_CONV - 2] = u
        return r

    gate = branch(wg_ref, cwg_ref, cbg_ref, sg_ref, ng_ref)
    val = branch(wv_ref, cwv_ref, cbv_ref, sv_ref, nv_ref)
    acc_scr[...] += _dot((_silu(gate) * val).astype(BF16), wo_ref[...])

    @pl.when(f == pl.num_programs(0) - 1)
    def _():
        out_ref[...] = y_ref[...] + acc_scr[...]


def _ffn_sample(y, g, w_in, conv_w, conv_b, w_out, state):
    bd, d = y.shape
    dff = w_out.shape[0]
    tf = _ffn_tile(dff)
    nf = dff // tf
    keep = FFN_CONV - 1
    st = jnp.transpose(state, (1, 0, 2))
    gate = lambda f: (0, f)
    val = lambda f: (0, f + nf)
    st_gate = pl.BlockSpec((keep, bd, tf), lambda f: (0, 0, f))
    st_val = pl.BlockSpec((keep, bd, tf), lambda f: (0, 0, f + nf))
    new_spec = pl.BlockSpec((keep, bd, tf), lambda f: (0, 0, f))
    out, ng, nv = pl.pallas_call(
        _ffn_sample_kernel,
        grid=(nf,),
        in_specs=[pl.BlockSpec((bd, d), lambda f: (0, 0)),
                  pl.BlockSpec((1, d), lambda f: (0, 0)),
                  pl.BlockSpec((d, tf), gate), pl.BlockSpec((d, tf), val),
                  pl.BlockSpec((FFN_CONV, tf), gate), pl.BlockSpec((FFN_CONV, tf), val),
                  pl.BlockSpec((1, tf), gate), pl.BlockSpec((1, tf), val),
                  st_gate, st_val,
                  pl.BlockSpec((tf, d), lambda f: (f, 0))],
        out_specs=[pl.BlockSpec((bd, d), lambda f: (0, 0)), new_spec, new_spec],
        out_shape=[jax.ShapeDtypeStruct((bd, d), F32),
                   jax.ShapeDtypeStruct((keep, bd, dff), F32),
                   jax.ShapeDtypeStruct((keep, bd, dff), F32)],
        scratch_shapes=[pltpu.VMEM((bd, d), BF16), pltpu.VMEM((bd, d), F32)],
        compiler_params=_cparams("arbitrary"),
        name="ffn_sample",
    )(y, g, w_in, w_in, conv_w, conv_w, conv_b, conv_b, st, st, w_out)
    new_state = jnp.transpose(jnp.concatenate([ng, nv], axis=-1), (1, 0, 2))
    return out, new_state


def _final_norm_kernel(x_ref, g_ref, o_ref):
    o_ref[...] = _rms_rows(x_ref[...], g_ref[...])


def _final_norm(x, g):
    m, d = x.shape
    tm = min(1024, m)
    return pl.pallas_call(
        _final_norm_kernel,
        grid=(m // tm,),
        in_specs=[pl.BlockSpec((tm, d), lambda i: (i, 0)), pl.BlockSpec((1, d), lambda i: (0, 0))],
        out_specs=pl.BlockSpec((tm, d), lambda i: (i, 0)),
        out_shape=jax.ShapeDtypeStruct((m, d), F32),
        compiler_params=_cparams("parallel"),
        name="final_norm",
    )(x, g)


def _rope_tables(pos):
    half = HEAD_DIM // 2
    inv = ROPE_THETA ** (-jnp.arange(half, dtype=F32) / half)
    ang = pos.astype(F32)[:, None] * inv[None, :]
    cos, sin = jnp.cos(ang), jnp.sin(ang)
    return jnp.concatenate([cos, cos], axis=-1), jnp.concatenate([-sin, sin], axis=-1)


def _dn_gate_layout(w_in, a_log, dt_bias):
    rep = DN_V_HEADS // DN_QK_HEADS
    n_gate = 2 * DN_V_HEADS
    src = np.full((LANES,), n_gate, np.int32)
    for hq in range(DN_QK_HEADS):
        for x in range(rep):
            src[SUBLANES * hq + x] = rep * hq + x
            src[SUBLANES * hq + rep + x] = DN_V_HEADS + rep * hq + x
    is_decay = (src >= DN_V_HEADS) & (src < n_gate)
    gates_t = w_in[:, DN_MAIN_DIM:].T
    w_bg = jnp.take(gates_t, src, axis=0, mode="fill", fill_value=0).T.astype(BF16)
    dec_src = np.where(is_decay, src - DN_V_HEADS, DN_V_HEADS)
    neg_a = jnp.take(-jnp.exp(a_log.astype(F32)), dec_src, mode="fill", fill_value=0)
    dt_b = jnp.take(dt_bias.astype(F32), dec_src, mode="fill", fill_value=0)
    is_beta = jnp.asarray((src < DN_V_HEADS).astype(np.float32))
    par = jnp.stack([neg_a, dt_b, is_beta])
    par = jnp.pad(par, ((0, SUBLANES - par.shape[0]), (0, 0)))
    return w_bg, par


def kernel(x_prompt, x_sample, cache_k, cache_v, state_dn, state_dn_conv, state_ffn_conv, page_table,
           norm_mix, norm_ffn, norm_out, dn_w_in, dn_conv_w, dn_a_log, dn_dt_bias, dn_norm, dn_w_out,
           mb_w_qkv, mb_w_o, ffn_w_in, ffn_conv_w, ffn_conv_b, ffn_w_out):
    bsz, seq, d = x_prompt.shape
    bd = x_sample.shape[0]
    depth = norm_mix.shape[0]
    n_pages = page_table.shape[1]
    past = n_pages * PAGE_SIZE
    own_blk = past // MB_BLOCK
    hd = HEAD_DIM

    yp = x_prompt.reshape(bsz * seq, d)
    ys = x_sample.reshape(bd, d)
    cos_p, sin_p = _rope_tables(jnp.arange(seq, dtype=I32))
    cos_s, sin_s = _rope_tables(jnp.full((bd,), past, dtype=I32))
    pool_k, pool_v = cache_k, cache_v

    kp_l, vp_l, ks_l, vs_l = [], [], [], []
    sp_l, ss_l, cp_l, cs_l = [], [], [], []
    fp_l, fs_l = [], []
    for layer in range(depth):
        j = layer // 2
        g_mix = norm_mix[layer].reshape(1, d)
        if layer % 2 == 0:
            w_main = dn_w_in[j][:, :DN_MAIN_DIM].astype(BF16)
            w_bg, par = _dn_gate_layout(dn_w_in[j], dn_a_log[j], dn_dt_bias[j])
            w_out = dn_w_out[j].astype(BF16)
            nw = dn_norm[j].reshape(1, hd)
            proj, bg, bgt = _dn_inproj(yp, g_mix, w_main, w_bg, par, transposed=True)
            o, s_new, blk_sums = _dn_core(proj, bg, bgt, dn_conv_w[j], nw, bsz, seq, pool_k, page_table, j)
            yp = _matmul_residual(o, w_out, yp)
            sp_l.append(s_new)
            cp_l.append(proj.reshape(bsz, seq, DN_MAIN_DIM)[:, seq - (DN_CONV - 1):, :DN_CONV_DIM])

            proj_s, bg_s = _dn_inproj(ys, g_mix, w_main, w_bg, par, transposed=False)
            o_s, s_s, c_s = _dn_sample(proj_s, bg_s, state_dn_conv[j], dn_conv_w[j], state_dn[j], nw)
            ys = _matmul_residual(o_s, w_out, ys)
            ss_l.append(s_s)
            cs_l.append(c_s)
        else:
            w_qkv = mb_w_qkv[j].astype(BF16)
            w_o = mb_w_o[j].astype(BF16)
            q, k, v, kb, vt, km = _mb_qkv(yp, g_mix, w_qkv, cos_p, sin_p, with_blocks=True)
            att = _mb_attn(q, kb, vt, km, bsz, seq)
            yp = _matmul_residual(att, w_o, yp)
            kp_l.append(k.reshape(bsz, seq, MB_HEADS, hd))
            vp_l.append(v.reshape(bsz, seq, MB_HEADS, hd))

            q_s, k_s, v_s = _mb_qkv(ys, g_mix, w_qkv, cos_s, sin_s, with_blocks=False)
            q3 = q_s.reshape(bd, MB_HEADS, hd)
            k3 = k_s.reshape(bd, MB_HEADS, hd)
            sel = _mb_sample_select(q3, k3, blk_sums, own_blk)
            att_s = _mb_sample_attn(pool_k, pool_v, page_table, sel, q3, k3, v_s.reshape(bd, MB_HEADS, hd), j)
            ys = _matmul_residual(att_s.astype(BF16), w_o, ys)
            ks_l.append(k_s.reshape(bd, 1, MB_HEADS, hd))
            vs_l.append(v_s.reshape(bd, 1, MB_HEADS, hd))

        g_ffn = norm_ffn[layer].reshape(1, d)
        w_fi = ffn_w_in[layer].astype(BF16)
        w_fo = ffn_w_out[layer].astype(BF16)
        cb = ffn_conv_b[layer].reshape(1, -1)
        yp, fbp = _ffn_prompt(yp, g_ffn, w_fi, ffn_conv_w[layer], cb, w_fo, bsz, seq)
        ys, fbs = _ffn_sample(ys, g_ffn, w_fi, ffn_conv_w[layer], cb, w_fo, state_ffn_conv[layer])
        fp_l.append(fbp)
        fs_l.append(fbs)

    g_out = norm_out.reshape(1, d)
    y_prompt = _final_norm(yp, g_out).reshape(bsz, seq, d)
    y_sample = _final_norm(ys, g_out).reshape(bd, 1, d)
    return (y_prompt, y_sample, jnp.stack(kp_l), jnp.stack(vp_l), jnp.stack(ks_l), jnp.stack(vs_l),
            jnp.stack(sp_l), jnp.stack(ss_l), jnp.stack(cp_l), jnp.stack(cs_l), jnp.stack(fp_l), jnp.stack(fs_l))
```

```python
import functools

import jax
import numpy as np
import jax.numpy as jnp
from jax import lax
from jax.experimental import pallas as pl
from jax.experimental.pallas import tpu as pltpu

F32 = jnp.float32
BF16 = jnp.bfloat16
I32 = jnp.int32

NORM_EPS = 1e-6
ROPE_THETA = 10000.0
HEAD_DIM = 128
DN_QK_HEADS = 8
DN_V_HEADS = 16
DN_CONV = 4
DN_KEY_DIM = DN_QK_HEADS * HEAD_DIM
DN_VAL_DIM = DN_V_HEADS * HEAD_DIM
DN_CONV_DIM = 2 * DN_KEY_DIM + DN_VAL_DIM
DN_MAIN_DIM = DN_CONV_DIM + DN_VAL_DIM
DN_CHUNK = 64
DN_TBLOCK = 128
DN_HEADS_PER_STEP = 8
MB_HEADS = 8
MB_BLOCK = 256
MB_TOP_K = 3
PAGES_PER_STEP = 16
MB_SAMPLE_SLOTS = 3
MB_V_PAD = 16
MB_HEADS_PER_STEP = 2
PAGE_SIZE = 128
FFN_CONV = 3
SUBLANES = 8
LANES = 128
NEG_BIG = -1e30
VMEM_LIMIT = 56 * 1024 * 1024


def _cparams(*sem):
    return pltpu.CompilerParams(dimension_semantics=sem, vmem_limit_bytes=VMEM_LIMIT)


def _sigmoid(x):
    return 1.0 / (1.0 + jnp.exp(-x))


def _silu(x):
    return x * _sigmoid(x)


def _softplus(x):
    return jnp.maximum(x, 0.0) + jnp.log1p(jnp.exp(-jnp.abs(x)))


def _rms_rows(x, g):
    ms = jnp.mean(x * x, axis=-1, keepdims=True)
    return x * lax.rsqrt(ms + NORM_EPS) * g


def _dot(a, b):
    return jnp.dot(a, b, preferred_element_type=F32)


def _dot_nt(a, b):
    return lax.dot_general(a, b, (((1,), (1,)), ((), ())), preferred_element_type=F32)


def _dot_tn(a, b):
    return lax.dot_general(a, b, (((0,), (0,)), ((), ())), preferred_element_type=F32)


def _split2(a):
    hi = a.astype(BF16)
    lo = (a - hi.astype(F32)).astype(BF16)
    return hi, lo


def _mm2_pair(a, bh, left):
    ah, al = a
    b_l, b_r = bh * left, bh * (1 - left)
    return _dot(jnp.concatenate([ah, al], axis=1), jnp.concatenate([b_l, b_r, b_l, b_r], axis=0))


def _dot_exact01(m01, x):
    hi = x.astype(BF16)
    r1 = x - hi.astype(F32)
    mid = r1.astype(BF16)
    lo = (r1 - mid.astype(F32)).astype(BF16)
    return _dot(m01, hi) + (_dot(m01, mid) + _dot(m01, lo))


def _dot_exact01_r(x, m01):
    hi = x.astype(BF16)
    r1 = x - hi.astype(F32)
    mid = r1.astype(BF16)
    lo = (r1 - mid.astype(F32)).astype(BF16)
    return _dot(hi, m01) + (_dot(mid, m01) + _dot(lo, m01))


def _dn_inproj_kernel(x_ref, g_ref, w_ref, wbg_ref, par_ref, o_ref, bg_ref, *rest, transposed):
    if transposed:
        bgt_ref, h_scr = rest
    else:
        (h_scr,) = rest

    @pl.when(pl.program_id(1) == 0)
    def _():
        h = _rms_rows(x_ref[...], g_ref[...]).astype(BF16)
        h_scr[...] = h
        raw = _dot(h, wbg_ref[...])
        neg_a, dt_b, is_beta = par_ref[0:1, :], par_ref[1:2, :], par_ref[2:3, :]
        act = jnp.where(is_beta > 0.5, _sigmoid(raw), neg_a * _softplus(raw + dt_b))
        bg_ref[...] = act
        if transposed:
            bgt_ref[...] = act.T

    o_ref[...] = _dot(h_scr[...], w_ref[...])


def _dn_inproj(x, g, w_main, w_bg, par, *, transposed):
    m, d = x.shape
    n = w_main.shape[1]
    tm = min(1024, m)
    tn = 2048
    out_shape = [jax.ShapeDtypeStruct((m, n), F32), jax.ShapeDtypeStruct((m, LANES), F32)]
    out_specs = [pl.BlockSpec((tm, tn), lambda i, j: (i, j)),
                 pl.BlockSpec((tm, LANES), lambda i, j: (i, 0))]
    if transposed:
        out_shape.append(jax.ShapeDtypeStruct((LANES, m), F32))
        out_specs.append(pl.BlockSpec((LANES, tm), lambda i, j: (0, i)))
    return pl.pallas_call(
        functools.partial(_dn_inproj_kernel, transposed=transposed),
        grid=(m // tm, n // tn),
        in_specs=[pl.BlockSpec((tm, d), lambda i, j: (i, 0)),
                  pl.BlockSpec((1, d), lambda i, j: (0, 0)),
                  pl.BlockSpec((d, tn), lambda i, j: (0, j)),
                  pl.BlockSpec((d, LANES), lambda i, j: (0, 0)),
                  pl.BlockSpec((SUBLANES, LANES), lambda i, j: (0, 0))],
        out_specs=out_specs,
        out_shape=out_shape,
        scratch_shapes=[pltpu.VMEM((tm, d), BF16)],
        compiler_params=_cparams("parallel", "arbitrary"),
        name="dn_inproj",
    )(x, g, w_main, w_bg, par)


def _page_sum_kernel(pt_ref, *refs):
    pages, o_ref = refs[:PAGES_PER_STEP], refs[PAGES_PER_STEP]
    per_blk = MB_BLOCK // PAGE_SIZE
    for r in range(PAGES_PER_STEP // per_blk):
        s = jnp.sum(pages[per_blk * r][...], axis=0)
        for i in range(1, per_blk):
            s = s + jnp.sum(pages[per_blk * r + i][...], axis=0)
        o_ref[r] = s


def _unit_lower_inverse_pairs(lows, c):
    row = lax.broadcasted_iota(I32, (c, 2 * c), 0)
    lane = lax.broadcasted_iota(I32, (c, 2 * c), 1)
    col = lane & (c - 1)
    left = jnp.where(lane < c, 1.0, 0.0).astype(BF16)

    def same_block(shift):
        return (row >> shift) == (col >> shift)

    low_parts = [_split2(low) for low in lows]
    base = same_block(3)
    negs = [jnp.where(base, -low, 0.0) for low in lows]
    eye = jnp.where(row == col, 1.0, 0.0)
    invs = [eye + neg for neg in negs]
    powers = [_split2(neg) for neg in negs]
    for _ in range(2):
        powers = [_split2(_mm2_pair(p, p[0], left)) for p in powers]
        invs = [inv + _mm2_pair(_split2(inv), p[0], left) for inv, p in zip(invs, powers)]
    shift = 3
    while (1 << shift) < c:
        mask = same_block(shift + 1) & jnp.logical_not(same_block(shift))
        inv_parts = [_split2(inv) for inv in invs]
        tmps = [jnp.where(mask, _mm2_pair(lp, ip[0], left), 0.0).astype(BF16)
                for lp, ip in zip(low_parts, inv_parts)]
        invs = [inv - _mm2_pair(ip, t, left) for inv, ip, t in zip(invs, inv_parts, tmps)]
        shift += 1
    return invs


def _dn_core_kernel(pt_ref, q_ref, k_ref, v_ref, z_ref, bg_ref, bgt_ref, cwq_ref, cwk_ref, cwv_ref, nw_ref,
                    *rest, tb, chunk, nh):
    pages = rest[:PAGES_PER_STEP]
    o_ref, s_out_ref, psum_ref, xe_scr, s_scr = rest[PAGES_PER_STEP:]
    _page_sum_kernel(pt_ref, *pages, psum_ref)
    hg = pl.program_id(1)
    t = pl.program_id(2)
    hd = HEAD_DIM
    n_chunks = tb // chunk
    k_off, v_off = nh * hd, 2 * nh * hd

    @pl.when(t == 0)
    def _():
        s_scr[...] = jnp.zeros(s_scr.shape, F32)
        xe_scr[0:SUBLANES, :] = jnp.zeros((SUBLANES, xe_scr.shape[1]), F32)

    xe_scr[SUBLANES:SUBLANES + tb, 0:k_off] = q_ref[...]
    xe_scr[SUBLANES:SUBLANES + tb, k_off:v_off] = k_ref[...]
    xe_scr[SUBLANES:SUBLANES + tb, v_off:] = v_ref[...]

    def conv(lo, hi, cw_ref, w_lo):
        w = cw_ref[:, w_lo:w_lo + hi - lo]
        acc = w[DN_CONV - 1:DN_CONV, :] * xe_scr[SUBLANES:SUBLANES + tb, lo:hi]
        for d in range(1, DN_CONV):
            acc = acc + w[DN_CONV - 1 - d:DN_CONV - d, :] * xe_scr[SUBLANES - d:SUBLANES - d + tb, lo:hi]
        return _silu(acc)

    ri = lax.broadcasted_iota(I32, (tb, tb), 0)
    ci = lax.broadcasted_iota(I32, (tb, tb), 1)
    cshift = chunk.bit_length() - 1
    same_chunk = (ri >> cshift) == (ci >> cshift)
    upper = jnp.where(same_chunk & (ri <= ci), 1.0, 0.0).astype(BF16)
    lower = jnp.where(same_chunk & (ri >= ci), 1.0, 0.0).astype(BF16)

    rc = lax.broadcasted_iota(I32, (chunk, 2 * chunk), 0)
    lane = lax.broadcasted_iota(I32, (chunk, 2 * chunk), 1)
    cc = lane & (chunk - 1)
    is_left = lane < chunk
    nw = nw_ref[...]

    items = []
    for g in range(nh):
        qc = conv(g * hd, (g + 1) * hd, cwq_ref, g * hd)
        kc = conv(k_off + g * hd, k_off + (g + 1) * hd, cwk_ref, g * hd)
        vc = conv(v_off + 2 * g * hd, v_off + 2 * (g + 1) * hd, cwv_ref, 2 * g * hd)
        qn = qc * lax.rsqrt(jnp.sum(qc * qc, axis=-1, keepdims=True) + NORM_EPS) * (hd ** -0.5)
        kn = kc * lax.rsqrt(jnp.sum(kc * kc, axis=-1, keepdims=True) + NORM_EPS)
        hq = hg * nh + g
        bgr = pltpu.roll(bg_ref[...], (LANES - SUBLANES * hq) % LANES, axis=1)
        bgt = bgt_ref[g * SUBLANES:(g + 1) * SUBLANES, :]
        gc_rows = _dot_exact01_r(bgt, upper)
        gc_cols = _dot_exact01(lower, bgr)
        for c in range(n_chunks):
            r0, r1 = c * chunk, (c + 1) * chunk
            q_c, k_c = qn[r0:r1], kn[r0:r1]
            qb, kb = q_c.astype(BF16), k_c.astype(BF16)
            prods = _dot_nt(jnp.concatenate([qb, kb], axis=0), jnp.concatenate([kb, kb], axis=0))
            qk2, kk2 = prods[:chunk], prods[chunk:]
            g_cols = [gc_cols[r0:r1, 2 + x:3 + x] for x in range(2)]
            b_cols = [bgr[r0:r1, x:x + 1] for x in range(2)]
            g_row2 = jnp.concatenate([gc_rows[2 + x:3 + x, r0:r1] for x in range(2)], axis=1)
            g_col2 = jnp.where(is_left, g_cols[0], g_cols[1])
            b_col2 = jnp.where(is_left, b_cols[0], b_cols[1])
            decay2 = jnp.exp(jnp.where(rc >= cc, g_col2 - g_row2, -jnp.inf))
            heads = []
            for x in range(2):
                eg = jnp.exp(g_cols[x])
                g_last = g_cols[x][chunk - 1:chunk, :]
                v_c = vc[r0:r1, x * hd:(x + 1) * hd]
                heads.append(dict(
                    rhs=jnp.concatenate([b_cols[x] * v_c, (b_cols[x] * eg) * k_c], axis=1).astype(BF16),
                    qg=(q_c * eg).astype(BF16),
                    kd=(k_c * jnp.exp(g_last - g_cols[x])).astype(BF16),
                    gl=jnp.exp(g_last)))
            items.append(dict(g=g, c=c, low=jnp.where(rc > cc, b_col2 * decay2 * kk2, 0.0),
                              qkd=(qk2 * decay2).astype(BF16), heads=heads))
    xe_scr[0:SUBLANES, :] = xe_scr[tb:tb + SUBLANES, :]

    invs = _unit_lower_inverse_pairs([it["low"] for it in items], chunk)
    for inv, it in zip(invs, items):
        ra, rb = it["heads"][0]["rhs"], it["heads"][1]["rhs"]
        zero = jnp.zeros_like(ra)
        rhs = jnp.concatenate([jnp.concatenate([ra, zero], axis=1),
                               jnp.concatenate([zero, rb], axis=1)], axis=0)
        it["sol"] = _dot(inv.astype(BF16), rhs)

    states = [s_scr[h] for h in range(2 * nh)]
    for c in range(n_chunks):
        r0, r1 = c * chunk, (c + 1) * chunk
        row = [it for it in items if it["c"] == c]
        sbs = [states[2 * it["g"] + x].astype(BF16) for it in row for x in range(2)]
        ws_qs = [_dot(jnp.concatenate([it["sol"][:, (2 * x + 1) * hd:(2 * x + 2) * hd].astype(BF16),
                                       it["heads"][x]["qg"]], axis=0), sbs[2 * i + x])
                 for i, it in enumerate(row) for x in range(2)]
        ubs = [(it["sol"][:, 2 * x * hd:(2 * x + 1) * hd] - ws_qs[2 * i + x][:chunk]).astype(BF16)
               for i, it in enumerate(row) for x in range(2)]
        intras = []
        for i, it in enumerate(row):
            zero = jnp.zeros_like(ubs[0])
            u_diag = jnp.concatenate([jnp.concatenate([ubs[2 * i], zero], axis=1),
                                      jnp.concatenate([zero, ubs[2 * i + 1]], axis=1)], axis=0)
            intras.append(_dot(it["qkd"], u_diag))
        for i, it in enumerate(row):
            for x in range(2):
                h = 2 * it["g"] + x
                head = it["heads"][x]
                states[h] = head["gl"] * states[h] + _dot_tn(head["kd"], ubs[2 * i + x])
                o = ws_qs[2 * i + x][chunk:] + intras[i][:, x * hd:(x + 1) * hd]
                z_c = z_ref[r0:r1, h * hd:(h + 1) * hd]
                on = o * lax.rsqrt(jnp.mean(o * o, axis=-1, keepdims=True) + NORM_EPS) * nw * _silu(z_c)
                o_ref[r0:r1, h * hd:(h + 1) * hd] = on.astype(o_ref.dtype)
    for h in range(2 * nh):
        s_scr[h] = states[h]

    @pl.when(t == pl.num_programs(2) - 1)
    def _():
        s_out_ref[...] = s_scr[...]


def _dn_core(proj, bg, bgt, conv_w, norm_w, bsz, seq, pool, page_table, layer):
    tb, hd, nh = DN_TBLOCK, HEAD_DIM, DN_HEADS_PER_STEP
    nt = seq // tb
    ng = DN_QK_HEADS // nh
    bd, n_pages = page_table.shape
    heads = pool.shape[-2]
    groups = n_pages // PAGES_PER_STEP
    rows = PAGES_PER_STEP * PAGE_SIZE // MB_BLOCK
    assert bd * groups <= bsz * ng * nt, "at most one page group per delta-rule grid step"
    kernel = functools.partial(_dn_core_kernel, tb=tb, chunk=DN_CHUNK, nh=nh)
    tok = lambda b, h, t: b * nt + t
    step = lambda b, h, t: jnp.minimum((b * ng + h) * nt + t, bd * groups - 1)

    def page_spec(i):
        def imap(b, h, t, pt):
            s = step(b, h, t)
            return (layer, pt[(s // groups) * n_pages + (s % groups) * PAGES_PER_STEP + i], 0, 0, 0)
        return pl.BlockSpec((None, None, PAGE_SIZE, heads, hd), imap)

    return pl.pallas_call(
        kernel,
        grid_spec=pltpu.PrefetchScalarGridSpec(
            num_scalar_prefetch=1,
            grid=(bsz, ng, nt),
            in_specs=[
                pl.BlockSpec((tb, nh * hd), lambda b, h, t, pt: (tok(b, h, t), h)),
                pl.BlockSpec((tb, nh * hd), lambda b, h, t, pt: (tok(b, h, t), ng + h)),
                pl.BlockSpec((tb, 2 * nh * hd), lambda b, h, t, pt: (tok(b, h, t), ng + h)),
                pl.BlockSpec((tb, 2 * nh * hd), lambda b, h, t, pt: (tok(b, h, t), 2 * ng + h)),
                pl.BlockSpec((tb, LANES), lambda b, h, t, pt: (tok(b, h, t), 0)),
                pl.BlockSpec((nh * SUBLANES, tb), lambda b, h, t, pt: (h, tok(b, h, t))),
                pl.BlockSpec((DN_CONV, nh * hd), lambda b, h, t, pt: (0, h)),
                pl.BlockSpec((DN_CONV, nh * hd), lambda b, h, t, pt: (0, ng + h)),
                pl.BlockSpec((DN_CONV, 2 * nh * hd), lambda b, h, t, pt: (0, ng + h)),
                pl.BlockSpec((1, hd), lambda b, h, t, pt: (0, 0)),
            ] + [page_spec(i) for i in range(PAGES_PER_STEP)],
            out_specs=[
                pl.BlockSpec((tb, 2 * nh * hd), lambda b, h, t, pt: (tok(b, h, t), h)),
                pl.BlockSpec((None, 2 * nh, hd, hd), lambda b, h, t, pt: (b, h, 0, 0)),
                pl.BlockSpec((None, rows, heads, hd),
                             lambda b, h, t, pt: (step(b, h, t) // groups, step(b, h, t) % groups, 0, 0)),
            ],
            scratch_shapes=[pltpu.VMEM((tb + SUBLANES, 4 * nh * hd), F32),
                            pltpu.VMEM((2 * nh, hd, hd), F32)],
        ),
        out_shape=[jax.ShapeDtypeStruct((bsz * seq, DN_VAL_DIM), BF16),
                   jax.ShapeDtypeStruct((bsz, DN_V_HEADS, hd, hd), F32),
                   jax.ShapeDtypeStruct((bd, groups * rows, heads, hd), F32)],
        compiler_params=_cparams("parallel", "parallel", "arbitrary"),
        name="dn_core",
    )(page_table.reshape(-1), proj, proj, proj, proj, bg, bgt, conv_w, conv_w, conv_w, norm_w,
      *([pool] * PAGES_PER_STEP))


def _dn_sample_kernel(p_ref, bg_ref, cst_ref, cw_ref, s_ref, nw_ref,
                      o_ref, s_out_ref, cst_out_ref, r_scr):
    hd = HEAD_DIM
    x = p_ref[:, 0:DN_CONV_DIM]
    buf = cst_ref[...]
    y = x * cw_ref[DN_CONV - 1:DN_CONV, :]
    for i in range(DN_CONV - 1):
        y = y + buf[i:i + 1, :] * cw_ref[i:i + 1, :]
    cst_out_ref[0:DN_CONV - 2, :] = buf[1:DN_CONV - 1, :]
    cst_out_ref[DN_CONV - 2:DN_CONV - 1, :] = x
    a = _silu(y)

    r_scr[...] = jnp.zeros(r_scr.shape, F32)
    qs, ks = [], []
    for h in range(DN_QK_HEADS):
        qh = a[:, h * hd:(h + 1) * hd]
        kh = a[:, DN_KEY_DIM + h * hd:DN_KEY_DIM + (h + 1) * hd]
        qh = qh * lax.rsqrt(jnp.sum(qh * qh, axis=-1, keepdims=True) + NORM_EPS) * (hd ** -0.5)
        kh = kh * lax.rsqrt(jnp.sum(kh * kh, axis=-1, keepdims=True) + NORM_EPS)
        qs.append(qh)
        ks.append(kh)
        r_scr[h:h + 1, :] = qh
        r_scr[DN_QK_HEADS + h:DN_QK_HEADS + h + 1, :] = kh
    rt = r_scr[...].T

    bg = bg_ref[...]
    nw = nw_ref[...]
    rep = DN_V_HEADS // DN_QK_HEADS
    for h in range(DN_V_HEADS):
        hq, x_in = h // rep, h % rep
        beta = bg[:, SUBLANES * hq + x_in:SUBLANES * hq + x_in + 1]
        g = bg[:, SUBLANES * hq + 2 + x_in:SUBLANES * hq + 3 + x_in]
        eg = jnp.exp(g)
        q_col = rt[:, hq:hq + 1]
        k_col = rt[:, DN_QK_HEADS + hq:DN_QK_HEADS + hq + 1]
        s_old = s_ref[h]
        k_s = jnp.sum(k_col * s_old, axis=0, keepdims=True)
        q_s = jnp.sum(q_col * s_old, axis=0, keepdims=True)
        v_h = a[:, 2 * DN_KEY_DIM + h * hd:2 * DN_KEY_DIM + (h + 1) * hd]
        u = beta * v_h - (beta * eg) * k_s
        qk = jnp.sum(qs[hq] * ks[hq], axis=-1, keepdims=True)
        o = eg * q_s + qk * u
        s_out_ref[h] = eg * s_old + k_col * u
        z_h = p_ref[:, DN_CONV_DIM + h * hd:DN_CONV_DIM + (h + 1) * hd]
        on = o * lax.rsqrt(jnp.mean(o * o, axis=-1, keepdims=True) + NORM_EPS) * nw * _silu(z_h)
        o_ref[:, h * hd:(h + 1) * hd] = on.astype(o_ref.dtype)


def _dn_sample(proj, bg, conv_state, conv_w, state, norm_w):
    bd = proj.shape[0]
    hd = HEAD_DIM
    o, s_new, c_new = pl.pallas_call(
        _dn_sample_kernel,
        grid=(bd,),
        in_specs=[
            pl.BlockSpec((None, 1, DN_MAIN_DIM), lambda b: (b, 0, 0)),
            pl.BlockSpec((None, 1, LANES), lambda b: (b, 0, 0)),
            pl.BlockSpec((None, DN_CONV - 1, DN_CONV_DIM), lambda b: (b, 0, 0)),
            pl.BlockSpec((DN_CONV, DN_CONV_DIM), lambda b: (0, 0)),
            pl.BlockSpec((None, DN_V_HEADS, hd, hd), lambda b: (b, 0, 0, 0)),
            pl.BlockSpec((1, hd), lambda b: (0, 0)),
        ],
        out_specs=[
            pl.BlockSpec((None, 1, DN_VAL_DIM), lambda b: (b, 0, 0)),
            pl.BlockSpec((None, DN_V_HEADS, hd, hd), lambda b: (b, 0, 0, 0)),
            pl.BlockSpec((None, DN_CONV - 1, DN_CONV_DIM), lambda b: (b, 0, 0)),
        ],
        out_shape=[jax.ShapeDtypeStruct((bd, 1, DN_VAL_DIM), BF16),
                   jax.ShapeDtypeStruct(state.shape, F32),
                   jax.ShapeDtypeStruct(conv_state.shape, F32)],
        scratch_shapes=[pltpu.VMEM((LANES, LANES), F32)],
        compiler_params=_cparams("parallel"),
        name="dn_sample",
    )(proj.reshape(bd, 1, DN_MAIN_DIM), bg.reshape(bd, 1, LANES), conv_state, conv_w, state, norm_w)
    return o.reshape(bd, DN_VAL_DIM), s_new, c_new


def _matmul_residual_kernel(a_ref, w_ref, y_ref, o_ref):
    o_ref[...] = y_ref[...] + _dot(a_ref[...], w_ref[...])


def _matmul_residual(a, w, y):
    m, k = a.shape
    n = w.shape[1]
    tm = min(512, m)
    return pl.pallas_call(
        _matmul_residual_kernel,
        grid=(m // tm,),
        in_specs=[pl.BlockSpec((tm, k), lambda i: (i, 0)),
                  pl.BlockSpec((k, n), lambda i: (0, 0)),
                  pl.BlockSpec((tm, n), lambda i: (i, 0))],
        out_specs=pl.BlockSpec((tm, n), lambda i: (i, 0)),
        out_shape=jax.ShapeDtypeStruct((m, n), F32),
        compiler_params=_cparams("parallel"),
        name="matmul_residual",
    )(a, w, y)


def _mb_qkv_kernel(x_ref, g_ref, w_ref, cos_ref, sin_ref, q_ref, k_ref, v_ref, *rest,
                   with_blocks, blocks_per_seq):
    if with_blocks:
        kb_ref, vt_ref, km_ref, h_scr = rest
    else:
        (h_scr,) = rest
    i, j = pl.program_id(0), pl.program_id(1)
    hd = HEAD_DIM

    @pl.when(j == 0)
    def _():
        h_scr[...] = _rms_rows(x_ref[...], g_ref[...]).astype(BF16)

    y = _dot(h_scr[...], w_ref[...])
    tm = y.shape[0]

    def rope(val):
        cos, sin = cos_ref[...], sin_ref[...]
        parts = []
        for h in range(MB_HEADS):
            yh = val[:, h * hd:(h + 1) * hd]
            parts.append(yh * cos + pltpu.roll(yh, hd // 2, axis=1) * sin)
        return jnp.concatenate(parts, axis=1)

    @pl.when(j == 0)
    def _():
        q_ref[...] = rope(y).astype(q_ref.dtype)

    @pl.when(j == 1)
    def _():
        k = rope(y)
        k_ref[...] = k
        if with_blocks:
            row = lax.broadcasted_iota(I32, (tm, LANES), 0)
            lane = lax.broadcasted_iota(I32, (tm, LANES), 1)
            first_blk = (i * (tm // MB_BLOCK)) % blocks_per_seq
            onehot = jnp.where(lane == first_blk + row // MB_BLOCK, 1.0, 0.0).astype(BF16)
            for h in range(MB_HEADS):
                kb_ref[:, 2 * h * hd:(2 * h + 1) * hd] = k[:, h * hd:(h + 1) * hd].astype(BF16)
                kb_ref[:, (2 * h + 1) * hd:(2 * h + 2) * hd] = onehot
            for r in range(tm // MB_BLOCK):
                blk = k[r * MB_BLOCK:(r + 1) * MB_BLOCK, :]
                km_ref[r] = jnp.sum(blk, axis=0, keepdims=True) * (1.0 / MB_BLOCK)

    @pl.when(j == 2)
    def _():
        v_ref[...] = y
        if with_blocks:
            ones = jnp.ones((MB_V_PAD, MB_BLOCK), BF16)
            rows = hd + MB_V_PAD
            for r in range(tm // MB_BLOCK):
                vt = y[r * MB_BLOCK:(r + 1) * MB_BLOCK, :].T.astype(BF16)
                for h in range(MB_HEADS):
                    vt_ref[r, h * rows:h * rows + hd, :] = vt[h * hd:(h + 1) * hd, :]
                    vt_ref[r, h * rows + hd:(h + 1) * rows, :] = ones


def _mb_qkv(x, g, w, cos, sin, *, with_blocks):
    m, d = x.shape
    tm = min(512, m)
    n_pos_tiles = cos.shape[0] // tm
    out_shape = [jax.ShapeDtypeStruct((m, d), BF16 if with_blocks else F32),
                 jax.ShapeDtypeStruct((m, d), F32), jax.ShapeDtypeStruct((m, d), F32)]
    row = pl.BlockSpec((tm, d), lambda i, j: (i, 0))
    out_specs = [row, row, row]
    if with_blocks:
        nb = tm // MB_BLOCK
        vt_rows = MB_HEADS * (HEAD_DIM + MB_V_PAD)
        out_shape += [jax.ShapeDtypeStruct((m, 2 * d), BF16),
                      jax.ShapeDtypeStruct((m // MB_BLOCK, vt_rows, MB_BLOCK), BF16),
                      jax.ShapeDtypeStruct((m // MB_BLOCK, 1, d), F32)]
        out_specs += [pl.BlockSpec((tm, 2 * d), lambda i, j: (i, 0)),
                      pl.BlockSpec((nb, vt_rows, MB_BLOCK), lambda i, j: (i, 0, 0)),
                      pl.BlockSpec((nb, 1, d), lambda i, j: (i, 0, 0))]
    return pl.pallas_call(
        functools.partial(_mb_qkv_kernel, with_blocks=with_blocks,
                          blocks_per_seq=cos.shape[0] // MB_BLOCK if with_blocks else 0),
        grid=(m // tm, 3),
        in_specs=[row,
                  pl.BlockSpec((1, d), lambda i, j: (0, 0)),
                  pl.BlockSpec((d, d), lambda i, j: (0, j)),
                  pl.BlockSpec((tm, HEAD_DIM), lambda i, j: (i % n_pos_tiles, 0)),
                  pl.BlockSpec((tm, HEAD_DIM), lambda i, j: (i % n_pos_tiles, 0))],
        out_specs=out_specs,
        out_shape=out_shape,
        scratch_shapes=[pltpu.VMEM((tm, d), BF16)],
        compiler_params=_cparams("parallel", "arbitrary"),
        name="mb_qkv",
    )(x, g, w, cos, sin)


def _mb_attn_kernel(q_ref, k_ref, vt_ref, km_ref, o_ref, *logit_bufs, heads):
    n = pl.program_id(2)
    bq, hd = MB_BLOCK, HEAD_DIM
    vrows = hd + MB_V_PAD
    c = (hd ** -0.5) * 1.4426950408889634
    nb = km_ref.shape[0]
    blk = lax.broadcasted_iota(I32, (nb, bq), 0)
    kpos = lax.broadcasted_iota(I32, (bq, bq), 0)
    qpos = lax.broadcasted_iota(I32, (bq, bq), 1)
    own_rows = pl.ds(pl.multiple_of(n * bq, bq), bq)

    qs = [q_ref[:, h * hd:(h + 1) * hd] for h in range(heads)]
    sts = [jnp.where(kpos <= qpos, _dot_nt(k_ref[own_rows, 2 * h * hd:(2 * h + 1) * hd], qs[h]), NEG_BIG)
           for h in range(heads)]
    scores = [_dot_nt(km_ref[:, h * hd:(h + 1) * hd].astype(BF16), qs[h]) for h in range(heads)]
    carry = []
    for h in range(heads):
        m = jnp.max(sts[h], axis=0, keepdims=True)
        p = jnp.exp2((sts[h] - m) * c)
        carry += [m, _dot(vt_ref[n, h * vrows:(h + 1) * vrows, :], p.astype(BF16))]

    qas = []
    for h in range(heads):
        s = jnp.where(blk < n, scores[h], -jnp.inf)
        bias = jnp.full((nb, bq), NEG_BIG, F32)
        for _ in range(MB_TOP_K):
            top = jnp.max(s, axis=0, keepdims=True)
            idx = jnp.min(jnp.where(s == top, blk, nb), axis=0, keepdims=True)
            hit = blk == idx
            bias = jnp.where(hit & (idx < n), 0.0, bias)
            s = jnp.where(hit, -jnp.inf, s)
        bias = jnp.concatenate([bias, jnp.full((LANES - nb, bq), NEG_BIG, F32)], axis=0)
        qas.append(jnp.concatenate([qs[h], bias.T.astype(BF16)], axis=1))

    sets_x = [logit_bufs[8 * h:8 * h + 4] for h in range(heads)]
    sets_y = [logit_bufs[8 * h + 4:8 * h + 8] for h in range(heads)]

    def fill(sets, j0):
        for h, bufs in enumerate(sets):
            for i, buf in enumerate(bufs):
                j = jnp.minimum(j0 + i, nb - 1)
                rows = pl.ds(pl.multiple_of(j * bq, bq), bq)
                buf[...] = _dot_nt(k_ref[rows, 2 * h * hd:(2 * h + 2) * hd], qas[h])

    def consume(sets, j0, carry):
        out = []
        for h, bufs in enumerate(sets):
            m, acc = carry[2 * h], carry[2 * h + 1]
            m_new = m
            for buf in bufs:
                m_new = jnp.maximum(m_new, jnp.max(buf[...], axis=0, keepdims=True))
            p = jnp.concatenate([jnp.exp2((buf[...] - m_new) * c).astype(BF16) for buf in bufs], axis=0)
            vt = jnp.concatenate([vt_ref[j0 + i, h * vrows:(h + 1) * vrows, :]
                                  for i in range(len(bufs))], axis=1)
            out += [m_new, acc * jnp.exp2((m - m_new) * c) + _dot(vt, p)]
        return tuple(out)

    def body(t, carry):
        j = 8 * t
        fill(sets_y, j + 4)
        carry = consume(sets_x, j, carry)
        fill(sets_x, j + 8)
        return consume(sets_y, j + 4, carry)

    half_trips = (n + 3) // 4
    fill(sets_x, 0)
    carry = lax.fori_loop(0, half_trips // 2, body, tuple(carry))
    carry = lax.cond(half_trips % 2 == 1,
                     lambda: consume(sets_x, 8 * (half_trips // 2), carry),
                     lambda: carry)
    for h in range(heads):
        acc = carry[2 * h + 1]
        o_ref[:, h * hd:(h + 1) * hd] = (acc[0:hd, :] / acc[hd:hd + 1, :]).T.astype(o_ref.dtype)


def _mb_attn(q, kb, vt, km, bsz, seq):
    hd, bq, heads = HEAD_DIM, MB_BLOCK, MB_HEADS_PER_STEP
    nb = seq // bq
    d = MB_HEADS * hd
    assert nb % 4 == 0
    return pl.pallas_call(
        functools.partial(_mb_attn_kernel, heads=heads),
        grid=(bsz, MB_HEADS // heads, nb),
        in_specs=[
            pl.BlockSpec((bq, heads * hd), lambda b, h, n: (b * nb + n, h)),
            pl.BlockSpec((seq, 2 * heads * hd), lambda b, h, n: (b, h)),
            pl.BlockSpec((nb, heads * (hd + MB_V_PAD), bq), lambda b, h, n: (b, h, 0)),
            pl.BlockSpec((nb, None, heads * hd), lambda b, h, n: (b, 0, h)),
        ],
        out_specs=pl.BlockSpec((bq, heads * hd), lambda b, h, n: (b * nb + n, h)),
        out_shape=jax.ShapeDtypeStruct((bsz * seq, d), BF16),
        scratch_shapes=[pltpu.VMEM((bq, bq), F32)] * (8 * heads),
        compiler_params=_cparams("parallel", "parallel", "arbitrary"),
        name="mb_attn",
    )(q, kb, vt, km)


def _mb_sample_select_kernel(q_ref, kn_ref, bs_ref, sel_ref, *, own):
    nb = bs_ref.shape[0]
    n_rows = nb + SUBLANES
    inv_blk = 1.0 / MB_BLOCK
    q = q_ref[...]
    sc = jnp.sum(bs_ref[...] * inv_blk * q[None], axis=-1, keepdims=True)
    own_sc = jnp.sum(kn_ref[...] * inv_blk * q, axis=-1, keepdims=True)
    tail_row = lax.broadcasted_iota(I32, (SUBLANES, MB_HEADS, 1), 0)
    tail = jnp.where(tail_row == 0, own_sc[None], -jnp.inf)
    blk = lax.broadcasted_iota(I32, (n_rows, MB_HEADS, 1), 0)
    s = jnp.where(blk < own, jnp.concatenate([sc, tail], axis=0), -jnp.inf)
    out_lane = lax.broadcasted_iota(I32, (MB_HEADS, LANES), 1)
    out = jnp.zeros((MB_HEADS, LANES), I32)
    for r in range(MB_TOP_K):
        m = jnp.max(s, axis=0, keepdims=True)
        idx = jnp.min(jnp.where(s == m, blk, n_rows), axis=0, keepdims=True)
        s = jnp.where(blk == idx, -jnp.inf, s)
        idx = idx[0]
        out = jnp.where(out_lane == r, idx, out)
        out = jnp.where(out_lane == MB_TOP_K + 1 + r, (idx < own).astype(I32), out)
    sel_ref[...] = out


def _mb_sample_select(q, k_new, blk_sums, own):
    bd = q.shape[0]
    nb = blk_sums.shape[1]
    return pl.pallas_call(
        functools.partial(_mb_sample_select_kernel, own=own),
        grid=(bd,),
        in_specs=[pl.BlockSpec((None, MB_HEADS, HEAD_DIM), lambda b: (b, 0, 0)),
                  pl.BlockSpec((None, MB_HEADS, HEAD_DIM), lambda b: (b, 0, 0)),
                  pl.BlockSpec((None, nb, MB_HEADS, HEAD_DIM), lambda b: (b, 0, 0, 0))],
        out_specs=pl.BlockSpec((None, MB_HEADS, LANES), lambda b: (b, 0, 0)),
        out_shape=jax.ShapeDtypeStruct((bd, MB_HEADS, LANES), I32),
        compiler_params=_cparams("parallel"),
        name="mb_sample_select",
    )(q, k_new, blk_sums)


def _mb_sample_attn_kernel(pt_ref, sel_ref, q_ref, kn_ref, vn_ref, pool_k, pool_v, o_ref,
                           kbuf, vbuf, sem, *, layer, n_pages):
    per_blk = MB_BLOCK // PAGE_SIZE
    n_pg = MB_TOP_K * per_blk
    b, h = pl.program_id(0), pl.program_id(1)
    n_heads = pl.num_programs(1)
    step = b * n_heads + h
    n_steps = pl.num_programs(0) * n_heads
    slot = step % MB_SAMPLE_SLOTS
    scale = HEAD_DIM ** -0.5

    def copies(bb, hh, sl):
        base = (bb * MB_HEADS + hh) * SUBLANES
        out = []
        for r in range(MB_TOP_K):
            blk = jnp.minimum(sel_ref[base + r], n_pages // per_blk - 1)
            for half in range(per_blk):
                i = r * per_blk + half
                page = pt_ref[bb * n_pages + blk * per_blk + half]
                out.append(pltpu.make_async_copy(pool_k.at[layer, page, :, hh, :], kbuf.at[sl, i],
                                                 sem.at[sl, 0, i]))
                out.append(pltpu.make_async_copy(pool_v.at[layer, page, :, hh, :], vbuf.at[sl, i],
                                                 sem.at[sl, 1, i]))
        return out

    ahead = MB_SAMPLE_SLOTS - 1

    def start_for(s):
        for cp in copies(s // n_heads, s % n_heads, s % MB_SAMPLE_SLOTS):
            cp.start()

    @pl.when(step == 0)
    def _():
        for s in range(ahead):
            pl.when(s < n_steps)(functools.partial(start_for, s))

    @pl.when(step + ahead < n_steps)
    def _():
        start_for(step + ahead)

    for cp in copies(b, h, slot):
        cp.wait()

    q = q_ref[...]
    q8 = jnp.broadcast_to(q, (SUBLANES, HEAD_DIM)).astype(BF16)
    keys = jnp.concatenate([kbuf[slot, i] for i in range(n_pg)], axis=0).astype(BF16)
    vals = jnp.concatenate([vbuf[slot, i] for i in range(n_pg)], axis=0).astype(BF16)
    lg = _dot_nt(q8, keys) * scale
    lane = lax.broadcasted_iota(I32, lg.shape, 1)
    base = (b * MB_HEADS + h) * SUBLANES
    for r in range(MB_TOP_K):
        penalty = jnp.where(sel_ref[base + MB_TOP_K + 1 + r] == 0, NEG_BIG, 0.0)
        in_blk = (lane >= r * MB_BLOCK) & (lane < (r + 1) * MB_BLOCK)
        lg = jnp.where(in_blk, lg + penalty, lg)
    qb = q.astype(BF16).astype(F32)
    lo = jnp.sum(qb * kn_ref[...].astype(BF16).astype(F32), axis=-1, keepdims=True) * scale
    m = jnp.maximum(jnp.max(lg, axis=-1, keepdims=True), lo)
    p = jnp.exp(lg - m)
    po = jnp.exp(lo - m)
    l = jnp.sum(p, axis=-1, keepdims=True) + po
    vn = vn_ref[...].astype(BF16).astype(F32)
    out = (_dot(p.astype(BF16), vals) + po.astype(BF16).astype(F32) * vn) / l
    o_ref[...] = out[0:1, :]


def _mb_sample_attn(pool_k, pool_v, page_table, sel, q, k_new, v_new, layer):
    bd, n_pages = page_table.shape
    hd = HEAD_DIM
    n_pg = MB_TOP_K * (MB_BLOCK // PAGE_SIZE)
    vec = pl.BlockSpec((None, None, 1, hd), lambda b, h, pt, sl: (b, h, 0, 0))
    hbm = pl.BlockSpec(memory_space=pl.ANY)
    q4 = q.reshape(bd, MB_HEADS, 1, hd)
    out = pl.pallas_call(
        functools.partial(_mb_sample_attn_kernel, layer=layer, n_pages=n_pages),
        grid_spec=pltpu.PrefetchScalarGridSpec(
            num_scalar_prefetch=2,
            grid=(bd, MB_HEADS),
            in_specs=[vec, vec, vec, hbm, hbm],
            out_specs=vec,
            scratch_shapes=[pltpu.VMEM((MB_SAMPLE_SLOTS, n_pg, PAGE_SIZE, hd), F32),
                            pltpu.VMEM((MB_SAMPLE_SLOTS, n_pg, PAGE_SIZE, hd), F32),
                            pltpu.SemaphoreType.DMA((MB_SAMPLE_SLOTS, 2, n_pg))],
        ),
        out_shape=jax.ShapeDtypeStruct((bd, MB_HEADS, 1, hd), F32),
        compiler_params=_cparams("arbitrary", "arbitrary"),
        name="mb_sample_attn",
    )(page_table.reshape(-1), sel[:, :, :SUBLANES].reshape(-1), q4, k_new.reshape(q4.shape), v_new.reshape(q4.shape),
      pool_k, pool_v)
    return out.reshape(bd, MB_HEADS * hd)


def _ffn_kernel(y_ref, g_ref, wg_ref, wv_ref, cwg_ref, cwv_ref, cbg_ref, cbv_ref, wo_ref,
                out_ref, bufg_ref, bufv_ref, h_scr, ug_scr, uv_scr, cg_scr, cv_scr, acc_scr,
                *, tiles_per_seq):
    i, f = pl.program_id(0), pl.program_id(1)
    tm = y_ref.shape[0]
    first = (i % tiles_per_seq) == 0

    @pl.when(f == 0)
    def _():
        h_scr[...] = _rms_rows(y_ref[...], g_ref[...]).astype(BF16)
        acc_scr[...] = jnp.zeros(acc_scr.shape, F32)

    @pl.when(first)
    def _():
        cg_scr[f] = jnp.zeros(cg_scr.shape[1:], F32)
        cv_scr[f] = jnp.zeros(cv_scr.shape[1:], F32)

    def branch(w_ref, cw_ref, cb_ref, u_scr, carry, buf_ref):
        u = _dot(h_scr[...], w_ref[...])
        u_scr[0:SUBLANES, :] = carry[f]
        u_scr[SUBLANES:SUBLANES + tm, :] = u
        r = cw_ref[FFN_CONV - 1:FFN_CONV, :] * u + cb_ref[...]
        for d in range(1, FFN_CONV):
            r = r + cw_ref[FFN_CONV - 1 - d:FFN_CONV - d, :] * u_scr[SUBLANES - d:SUBLANES - d + tm, :]
        last = u_scr[tm:tm + SUBLANES, :]
        carry[f] = last
        buf_ref[...] = last
        return r

    gate = branch(wg_ref, cwg_ref, cbg_ref, ug_scr, cg_scr, bufg_ref)
    val = branch(wv_ref, cwv_ref, cbv_ref, uv_scr, cv_scr, bufv_ref)
    act = (_silu(gate) * val).astype(BF16)
    acc_scr[...] += _dot(act, wo_ref[...])

    @pl.when(f == pl.num_programs(1) - 1)
    def _():
        out_ref[...] = y_ref[...] + acc_scr[...]


def _ffn_tile(dff):
    for cand in (1408, 1024, 512, 256, 128):
        if dff % cand == 0:
            return cand
    raise ValueError(f"unsupported d_ff {dff}")


def _ffn_prompt(y, g, w_in, conv_w, conv_b, w_out, bsz, seq):
    m, d = y.shape
    dff = w_out.shape[0]
    tm = min(512, seq)
    tf = _ffn_tile(dff)
    nf = dff // tf
    tiles_per_seq = seq // tm
    gate = lambda i, f: (0, f)
    val = lambda i, f: (0, f + nf)
    buf_spec = pl.BlockSpec((None, SUBLANES, tf), lambda i, f: (i, 0, f))
    out, bufg, bufv = pl.pallas_call(
        functools.partial(_ffn_kernel, tiles_per_seq=tiles_per_seq),
        grid=(m // tm, nf),
        in_specs=[pl.BlockSpec((tm, d), lambda i, f: (i, 0)),
                  pl.BlockSpec((1, d), lambda i, f: (0, 0)),
                  pl.BlockSpec((d, tf), gate), pl.BlockSpec((d, tf), val),
                  pl.BlockSpec((FFN_CONV, tf), gate), pl.BlockSpec((FFN_CONV, tf), val),
                  pl.BlockSpec((1, tf), gate), pl.BlockSpec((1, tf), val),
                  pl.BlockSpec((tf, d), lambda i, f: (f, 0))],
        out_specs=[pl.BlockSpec((tm, d), lambda i, f: (i, 0)), buf_spec, buf_spec],
        out_shape=[jax.ShapeDtypeStruct((m, d), F32),
                   jax.ShapeDtypeStruct((m // tm, SUBLANES, dff), F32),
                   jax.ShapeDtypeStruct((m // tm, SUBLANES, dff), F32)],
        scratch_shapes=[pltpu.VMEM((tm, d), BF16),
                        pltpu.VMEM((tm + SUBLANES, tf), F32), pltpu.VMEM((tm + SUBLANES, tf), F32),
                        pltpu.VMEM((nf, SUBLANES, tf), F32), pltpu.VMEM((nf, SUBLANES, tf), F32),
                        pltpu.VMEM((tm, d), F32)],
        compiler_params=_cparams("arbitrary", "arbitrary"),
        name="ffn_prompt",
    )(y, g, w_in, w_in, conv_w, conv_w, conv_b, conv_b, w_out)
    keep = FFN_CONV - 1
    last = slice(tiles_per_seq - 1, None, tiles_per_seq)
    new_buf = jnp.concatenate([bufg[last, SUBLANES - keep:], bufv[last, SUBLANES - keep:]], axis=-1)
    return out, new_buf


def _ffn_sample_kernel(y_ref, g_ref, wg_ref, wv_ref, cwg_ref, cwv_ref, cbg_ref, cbv_ref,
                       sg_ref, sv_ref, wo_ref, out_ref, ng_ref, nv_ref, h_scr, acc_scr):
    f = pl.program_id(0)

    @pl.when(f == 0)
    def _():
        h_scr[...] = _rms_rows(y_ref[...], g_ref[...]).astype(BF16)
        acc_scr[...] = jnp.zeros(acc_scr.shape, F32)

    def branch(w_ref, cw_ref, cb_ref, st_ref, new_ref):
        u = _dot(h_scr[...], w_ref[...])
        r = cw_ref[FFN_CONV - 1:FFN_CONV, :] * u + cb_ref[...]
        for i in range(FFN_CONV - 1):
            r = r + cw_ref[i:i + 1, :] * st_ref[i]
        for i in range(FFN_CONV - 2):
            new_ref[i] = st_ref[i + 1]
        new_ref[FFN_CONV - 2] = u
        return r

    gate = branch(wg_ref, cwg_ref, cbg_ref, sg_ref, ng_ref)
    val = branch(wv_ref, cwv_ref, cbv_ref, sv_ref, nv_ref)
    acc_scr[...] += _dot((_silu(gate) * val).astype(BF16), wo_ref[...])

    @pl.when(f == pl.num_programs(0) - 1)
    def _():
        out_ref[...] = y_ref[...] + acc_scr[...]


def _ffn_sample(y, g, w_in, conv_w, conv_b, w_out, state):
    bd, d = y.shape
    dff = w_out.shape[0]
    tf = _ffn_tile(dff)
    nf = dff // tf
    keep = FFN_CONV - 1
    st = jnp.transpose(state, (1, 0, 2))
    gate = lambda f: (0, f)
    val = lambda f: (0, f + nf)
    st_gate = pl.BlockSpec((keep, bd, tf), lambda f: (0, 0, f))
    st_val = pl.BlockSpec((keep, bd, tf), lambda f: (0, 0, f + nf))
    new_spec = pl.BlockSpec((keep, bd, tf), lambda f: (0, 0, f))
    out, ng, nv = pl.pallas_call(
        _ffn_sample_kernel,
        grid=(nf,),
        in_specs=[pl.BlockSpec((bd, d), lambda f: (0, 0)),
                  pl.BlockSpec((1, d), lambda f: (0, 0)),
                  pl.BlockSpec((d, tf), gate), pl.BlockSpec((d, tf), val),
                  pl.BlockSpec((FFN_CONV, tf), gate), pl.BlockSpec((FFN_CONV, tf), val),
                  pl.BlockSpec((1, tf), gate), pl.BlockSpec((1, tf), val),
                  st_gate, st_val,
                  pl.BlockSpec((tf, d), lambda f: (f, 0))],
        out_specs=[pl.BlockSpec((bd, d), lambda f: (0, 0)), new_spec, new_spec],
        out_shape=[jax.ShapeDtypeStruct((bd, d), F32),
                   jax.ShapeDtypeStruct((keep, bd, dff), F32),
                   jax.ShapeDtypeStruct((keep, bd, dff), F32)],
        scratch_shapes=[pltpu.VMEM((bd, d), BF16), pltpu.VMEM((bd, d), F32)],
        compiler_params=_cparams("arbitrary"),
        name="ffn_sample",
    )(y, g, w_in, w_in, conv_w, conv_w, conv_b, conv_b, st, st, w_out)
    new_state = jnp.transpose(jnp.concatenate([ng, nv], axis=-1), (1, 0, 2))
    return out, new_state


def _final_norm_kernel(x_ref, g_ref, o_ref):
    o_ref[...] = _rms_rows(x_ref[...], g_ref[...])


def _final_norm(x, g):
    m, d = x.shape
    tm = min(1024, m)
    return pl.pallas_call(
        _final_norm_kernel,
        grid=(m // tm,),
        in_specs=[pl.BlockSpec((tm, d), lambda i: (i, 0)), pl.BlockSpec((1, d), lambda i: (0, 0))],
        out_specs=pl.BlockSpec((tm, d), lambda i: (i, 0)),
        out_shape=jax.ShapeDtypeStruct((m, d), F32),
        compiler_params=_cparams("parallel"),
        name="final_norm",
    )(x, g)


def _rope_tables(pos):
    half = HEAD_DIM // 2
    inv = ROPE_THETA ** (-jnp.arange(half, dtype=F32) / half)
    ang = pos.astype(F32)[:, None] * inv[None, :]
    cos, sin = jnp.cos(ang), jnp.sin(ang)
    return jnp.concatenate([cos, cos], axis=-1), jnp.concatenate([-sin, sin], axis=-1)


def _dn_gate_layout(w_in, a_log, dt_bias):
    rep = DN_V_HEADS // DN_QK_HEADS
    n_gate = 2 * DN_V_HEADS
    src = np.full((LANES,), n_gate, np.int32)
    for hq in range(DN_QK_HEADS):
        for x in range(rep):
            src[SUBLANES * hq + x] = rep * hq + x
            src[SUBLANES * hq + rep + x] = DN_V_HEADS + rep * hq + x
    is_decay = (src >= DN_V_HEADS) & (src < n_gate)
    gates_t = w_in[:, DN_MAIN_DIM:].T
    w_bg = jnp.take(gates_t, src, axis=0, mode="fill", fill_value=0).T.astype(BF16)
    dec_src = np.where(is_decay, src - DN_V_HEADS, DN_V_HEADS)
    neg_a = jnp.take(-jnp.exp(a_log.astype(F32)), dec_src, mode="fill", fill_value=0)
    dt_b = jnp.take(dt_bias.astype(F32), dec_src, mode="fill", fill_value=0)
    is_beta = jnp.asarray((src < DN_V_HEADS).astype(np.float32))
    par = jnp.stack([neg_a, dt_b, is_beta])
    par = jnp.pad(par, ((0, SUBLANES - par.shape[0]), (0, 0)))
    return w_bg, par


def kernel(x_prompt, x_sample, cache_k, cache_v, state_dn, state_dn_conv, state_ffn_conv, page_table,
           norm_mix, norm_ffn, norm_out, dn_w_in, dn_conv_w, dn_a_log, dn_dt_bias, dn_norm, dn_w_out,
           mb_w_qkv, mb_w_o, ffn_w_in, ffn_conv_w, ffn_conv_b, ffn_w_out):
    bsz, seq, d = x_prompt.shape
    bd = x_sample.shape[0]
    depth = norm_mix.shape[0]
    n_pages = page_table.shape[1]
    past = n_pages * PAGE_SIZE
    own_blk = past // MB_BLOCK
    hd = HEAD_DIM

    yp = x_prompt.reshape(bsz * seq, d)
    ys = x_sample.reshape(bd, d)
    cos_p, sin_p = _rope_tables(jnp.arange(seq, dtype=I32))
    cos_s, sin_s = _rope_tables(jnp.full((bd,), past, dtype=I32))
    pool_k, pool_v = cache_k, cache_v

    kp_l, vp_l, ks_l, vs_l = [], [], [], []
    sp_l, ss_l, cp_l, cs_l = [], [], [], []
    fp_l, fs_l = [], []
    for layer in range(depth):
        j = layer // 2
        g_mix = norm_mix[layer].reshape(1, d)
        if layer % 2 == 0:
            w_main = dn_w_in[j][:, :DN_MAIN_DIM].astype(BF16)
            w_bg, par = _dn_gate_layout(dn_w_in[j], dn_a_log[j], dn_dt_bias[j])
            w_out = dn_w_out[j].astype(BF16)
            nw = dn_norm[j].reshape(1, hd)
            proj, bg, bgt = _dn_inproj(yp, g_mix, w_main, w_bg, par, transposed=True)
            o, s_new, blk_sums = _dn_core(proj, bg, bgt, dn_conv_w[j], nw, bsz, seq, pool_k, page_table, j)
            yp = _matmul_residual(o, w_out, yp)
            sp_l.append(s_new)
            cp_l.append(proj.reshape(bsz, seq, DN_MAIN_DIM)[:, seq - (DN_CONV - 1):, :DN_CONV_DIM])

            proj_s, bg_s = _dn_inproj(ys, g_mix, w_main, w_bg, par, transposed=False)
            o_s, s_s, c_s = _dn_sample(proj_s, bg_s, state_dn_conv[j], dn_conv_w[j], state_dn[j], nw)
            ys = _matmul_residual(o_s, w_out, ys)
            ss_l.append(s_s)
            cs_l.append(c_s)
        else:
            w_qkv = mb_w_qkv[j].astype(BF16)
            w_o = mb_w_o[j].astype(BF16)
            q, k, v, kb, vt, km = _mb_qkv(yp, g_mix, w_qkv, cos_p, sin_p, with_blocks=True)
            att = _mb_attn(q, kb, vt, km, bsz, seq)
            yp = _matmul_residual(att, w_o, yp)
            kp_l.append(k.reshape(bsz, seq, MB_HEADS, hd))
            vp_l.append(v.reshape(bsz, seq, MB_HEADS, hd))

            q_s, k_s, v_s = _mb_qkv(ys, g_mix, w_qkv, cos_s, sin_s, with_blocks=False)
            q3 = q_s.reshape(bd, MB_HEADS, hd)
            k3 = k_s.reshape(bd, MB_HEADS, hd)
            sel = _mb_sample_select(q3, k3, blk_sums, own_blk)
            att_s = _mb_sample_attn(pool_k, pool_v, page_table, sel, q3, k3, v_s.reshape(bd, MB_HEADS, hd), j)
            ys = _matmul_residual(att_s.astype(BF16), w_o, ys)
            ks_l.append(k_s.reshape(bd, 1, MB_HEADS, hd))
            vs_l.append(v_s.reshape(bd, 1, MB_HEADS, hd))

        g_ffn = norm_ffn[layer].reshape(1, d)
        w_fi = ffn_w_in[layer].astype(BF16)
        w_fo = ffn_w_out[layer].astype(BF16)
        cb = ffn_conv_b[layer].reshape(1, -1)
        yp, fbp = _ffn_prompt(yp, g_ffn, w_fi, ffn_conv_w[layer], cb, w_fo, bsz, seq)
        ys, fbs = _ffn_sample(ys, g_ffn, w_fi, ffn_conv_w[layer], cb, w_fo, state_ffn_conv[layer])
        fp_l.append(fbp)
        fs_l.append(fbs)

    g_out = norm_out.reshape(1, d)
    y_prompt = _final_norm(yp, g_out).reshape(bsz, seq, d)
    y_sample = _final_norm(ys, g_out).reshape(bd, 1, d)
    return (y_prompt, y_sample, jnp.stack(kp_l), jnp.stack(vp_l), jnp.stack(ks_l), jnp.stack(vs_l),
            jnp.stack(sp_l), jnp.stack(ss_l), jnp.stack(cp_l), jnp.stack(cs_l), jnp.stack(fp_l), jnp.stack(fs_l))
```

```python
import functools

import jax
import numpy as np
import jax.numpy as jnp
from jax import lax
from jax.experimental import pallas as pl
from jax.experimental.pallas import tpu as pltpu

F32 = jnp.float32
BF16 = jnp.bfloat16
I32 = jnp.int32

NORM_EPS = 1e-6
ROPE_THETA = 10000.0
HEAD_DIM = 128
DN_QK_HEADS = 8
DN_V_HEADS = 16
DN_CONV = 4
DN_KEY_DIM = DN_QK_HEADS * HEAD_DIM
DN_VAL_DIM = DN_V_HEADS * HEAD_DIM
DN_CONV_DIM = 2 * DN_KEY_DIM + DN_VAL_DIM
DN_MAIN_DIM = DN_CONV_DIM + DN_VAL_DIM
DN_CHUNK = 64
DN_TBLOCK = 128
DN_HEADS_PER_STEP = 8
MB_HEADS = 8
MB_BLOCK = 256
MB_TOP_K = 3
PAGES_PER_STEP = 16
MB_SAMPLE_SLOTS = 4
MB_V_PAD = 16
MB_HEADS_PER_STEP = 2
PAGE_SIZE = 128
FFN_CONV = 3
SUBLANES = 8
LANES = 128
NEG_BIG = -1e30
VMEM_LIMIT = 56 * 1024 * 1024


def _cparams(*sem):
    return pltpu.CompilerParams(dimension_semantics=sem, vmem_limit_bytes=VMEM_LIMIT)


def _sigmoid(x):
    return 1.0 / (1.0 + jnp.exp(-x))


def _silu(x):
    return x * _sigmoid(x)


def _softplus(x):
    return jnp.maximum(x, 0.0) + jnp.log1p(jnp.exp(-jnp.abs(x)))


def _rms_rows(x, g):
    ms = jnp.mean(x * x, axis=-1, keepdims=True)
    return x * lax.rsqrt(ms + NORM_EPS) * g


def _dot(a, b):
    return jnp.dot(a, b, preferred_element_type=F32)


def _dot_nt(a, b):
    return lax.dot_general(a, b, (((1,), (1,)), ((), ())), preferred_element_type=F32)


def _dot_tn(a, b):
    return lax.dot_general(a, b, (((0,), (0,)), ((), ())), preferred_element_type=F32)


def _split2(a):
    hi = a.astype(BF16)
    lo = (a - hi.astype(F32)).astype(BF16)
    return hi, lo


def _mm2_pair(a, bh, left):
    ah, al = a
    b_l, b_r = bh * left, bh * (1 - left)
    return _dot(jnp.concatenate([ah, al], axis=1), jnp.concatenate([b_l, b_r, b_l, b_r], axis=0))


def _dot_exact01(m01, x):
    hi = x.astype(BF16)
    r1 = x - hi.astype(F32)
    mid = r1.astype(BF16)
    lo = (r1 - mid.astype(F32)).astype(BF16)
    return _dot(m01, hi) + (_dot(m01, mid) + _dot(m01, lo))


def _dot_exact01_r(x, m01):
    hi = x.astype(BF16)
    r1 = x - hi.astype(F32)
    mid = r1.astype(BF16)
    lo = (r1 - mid.astype(F32)).astype(BF16)
    return _dot(hi, m01) + (_dot(mid, m01) + _dot(lo, m01))


def _dn_inproj_kernel(x_ref, g_ref, w_ref, wbg_ref, par_ref, o_ref, bg_ref, *rest, transposed):
    if transposed:
        bgt_ref, h_scr = rest
    else:
        (h_scr,) = rest

    @pl.when(pl.program_id(1) == 0)
    def _():
        h = _rms_rows(x_ref[...], g_ref[...]).astype(BF16)
        h_scr[...] = h
        raw = _dot(h, wbg_ref[...])
        neg_a, dt_b, is_beta = par_ref[0:1, :], par_ref[1:2, :], par_ref[2:3, :]
        act = jnp.where(is_beta > 0.5, _sigmoid(raw), neg_a * _softplus(raw + dt_b))
        bg_ref[...] = act
        if transposed:
            bgt_ref[...] = act.T

    o_ref[...] = _dot(h_scr[...], w_ref[...])


def _dn_inproj(x, g, w_main, w_bg, par, *, transposed):
    m, d = x.shape
    n = w_main.shape[1]
    tm = min(1024, m)
    tn = 2048
    out_shape = [jax.ShapeDtypeStruct((m, n), F32), jax.ShapeDtypeStruct((m, LANES), F32)]
    out_specs = [pl.BlockSpec((tm, tn), lambda i, j: (i, j)),
                 pl.BlockSpec((tm, LANES), lambda i, j: (i, 0))]
    if transposed:
        out_shape.append(jax.ShapeDtypeStruct((LANES, m), F32))
        out_specs.append(pl.BlockSpec((LANES, tm), lambda i, j: (0, i)))
    return pl.pallas_call(
        functools.partial(_dn_inproj_kernel, transposed=transposed),
        grid=(m // tm, n // tn),
        in_specs=[pl.BlockSpec((tm, d), lambda i, j: (i, 0)),
                  pl.BlockSpec((1, d), lambda i, j: (0, 0)),
                  pl.BlockSpec((d, tn), lambda i, j: (0, j)),
                  pl.BlockSpec((d, LANES), lambda i, j: (0, 0)),
                  pl.BlockSpec((SUBLANES, LANES), lambda i, j: (0, 0))],
        out_specs=out_specs,
        out_shape=out_shape,
        scratch_shapes=[pltpu.VMEM((tm, d), BF16)],
        compiler_params=_cparams("parallel", "arbitrary"),
        name="dn_inproj",
    )(x, g, w_main, w_bg, par)


def _page_sum_kernel(pt_ref, *refs):
    pages, o_ref = refs[:PAGES_PER_STEP], refs[PAGES_PER_STEP]
    per_blk = MB_BLOCK // PAGE_SIZE
    for r in range(PAGES_PER_STEP // per_blk):
        s = jnp.sum(pages[per_blk * r][...], axis=0)
        for i in range(1, per_blk):
            s = s + jnp.sum(pages[per_blk * r + i][...], axis=0)
        o_ref[r] = s


def _unit_lower_inverse_pairs(lows, c):
    row = lax.broadcasted_iota(I32, (c, 2 * c), 0)
    lane = lax.broadcasted_iota(I32, (c, 2 * c), 1)
    col = lane & (c - 1)
    left = jnp.where(lane < c, 1.0, 0.0).astype(BF16)

    def same_block(shift):
        return (row >> shift) == (col >> shift)

    low_parts = [_split2(low) for low in lows]
    base = same_block(3)
    negs = [jnp.where(base, -low, 0.0) for low in lows]
    eye = jnp.where(row == col, 1.0, 0.0)
    invs = [eye + neg for neg in negs]
    powers = [_split2(neg) for neg in negs]
    for _ in range(2):
        powers = [_split2(_mm2_pair(p, p[0], left)) for p in powers]
        invs = [inv + _mm2_pair(_split2(inv), p[0], left) for inv, p in zip(invs, powers)]
    shift = 3
    while (1 << shift) < c:
        mask = same_block(shift + 1) & jnp.logical_not(same_block(shift))
        inv_parts = [_split2(inv) for inv in invs]
        tmps = [jnp.where(mask, _mm2_pair(lp, ip[0], left), 0.0).astype(BF16)
                for lp, ip in zip(low_parts, inv_parts)]
        invs = [inv - _mm2_pair(ip, t, left) for inv, ip, t in zip(invs, inv_parts, tmps)]
        shift += 1
    return invs


def _dn_core_kernel(pt_ref, q_ref, k_ref, v_ref, z_ref, bg_ref, bgt_ref, cwq_ref, cwk_ref, cwv_ref, nw_ref,
                    *rest, tb, chunk, nh):
    pages = rest[:PAGES_PER_STEP]
    o_ref, s_out_ref, psum_ref, xe_scr, s_scr = rest[PAGES_PER_STEP:]
    _page_sum_kernel(pt_ref, *pages, psum_ref)
    hg = pl.program_id(1)
    t = pl.program_id(2)
    hd = HEAD_DIM
    n_chunks = tb // chunk
    k_off, v_off = nh * hd, 2 * nh * hd

    @pl.when(t == 0)
    def _():
        s_scr[...] = jnp.zeros(s_scr.shape, F32)
        xe_scr[0:SUBLANES, :] = jnp.zeros((SUBLANES, xe_scr.shape[1]), F32)

    xe_scr[SUBLANES:SUBLANES + tb, 0:k_off] = q_ref[...]
    xe_scr[SUBLANES:SUBLANES + tb, k_off:v_off] = k_ref[...]
    xe_scr[SUBLANES:SUBLANES + tb, v_off:] = v_ref[...]

    def conv(lo, hi, cw_ref, w_lo):
        w = cw_ref[:, w_lo:w_lo + hi - lo]
        acc = w[DN_CONV - 1:DN_CONV, :] * xe_scr[SUBLANES:SUBLANES + tb, lo:hi]
        for d in range(1, DN_CONV):
            acc = acc + w[DN_CONV - 1 - d:DN_CONV - d, :] * xe_scr[SUBLANES - d:SUBLANES - d + tb, lo:hi]
        return _silu(acc)

    ri = lax.broadcasted_iota(I32, (tb, tb), 0)
    ci = lax.broadcasted_iota(I32, (tb, tb), 1)
    cshift = chunk.bit_length() - 1
    same_chunk = (ri >> cshift) == (ci >> cshift)
    upper = jnp.where(same_chunk & (ri <= ci), 1.0, 0.0).astype(BF16)
    lower = jnp.where(same_chunk & (ri >= ci), 1.0, 0.0).astype(BF16)

    rc = lax.broadcasted_iota(I32, (chunk, 2 * chunk), 0)
    lane = lax.broadcasted_iota(I32, (chunk, 2 * chunk), 1)
    cc = lane & (chunk - 1)
    is_left = lane < chunk
    nw = nw_ref[...]

    items = []
    for g in range(nh):
        qc = conv(g * hd, (g + 1) * hd, cwq_ref, g * hd)
        kc = conv(k_off + g * hd, k_off + (g + 1) * hd, cwk_ref, g * hd)
        vc = conv(v_off + 2 * g * hd, v_off + 2 * (g + 1) * hd, cwv_ref, 2 * g * hd)
        qn = qc * lax.rsqrt(jnp.sum(qc * qc, axis=-1, keepdims=True) + NORM_EPS) * (hd ** -0.5)
        kn = kc * lax.rsqrt(jnp.sum(kc * kc, axis=-1, keepdims=True) + NORM_EPS)
        hq = hg * nh + g
        bgr = pltpu.roll(bg_ref[...], (LANES - SUBLANES * hq) % LANES, axis=1)
        bgt = bgt_ref[g * SUBLANES:(g + 1) * SUBLANES, :]
        gc_rows = _dot_exact01_r(bgt, upper)
        gc_cols = _dot_exact01(lower, bgr)
        for c in range(n_chunks):
            r0, r1 = c * chunk, (c + 1) * chunk
            q_c, k_c = qn[r0:r1], kn[r0:r1]
            qb, kb = q_c.astype(BF16), k_c.astype(BF16)
            prods = _dot_nt(jnp.concatenate([qb, kb], axis=0), jnp.concatenate([kb, kb], axis=0))
            qk2, kk2 = prods[:chunk], prods[chunk:]
            g_cols = [gc_cols[r0:r1, 2 + x:3 + x] for x in range(2)]
            b_cols = [bgr[r0:r1, x:x + 1] for x in range(2)]
            g_row2 = jnp.concatenate([gc_rows[2 + x:3 + x, r0:r1] for x in range(2)], axis=1)
            g_col2 = jnp.where(is_left, g_cols[0], g_cols[1])
            b_col2 = jnp.where(is_left, b_cols[0], b_cols[1])
            decay2 = jnp.exp(jnp.where(rc >= cc, g_col2 - g_row2, -jnp.inf))
            heads = []
            for x in range(2):
                eg = jnp.exp(g_cols[x])
                g_last = g_cols[x][chunk - 1:chunk, :]
                v_c = vc[r0:r1, x * hd:(x + 1) * hd]
                heads.append(dict(
                    rhs=jnp.concatenate([b_cols[x] * v_c, (b_cols[x] * eg) * k_c], axis=1).astype(BF16),
                    qg=(q_c * eg).astype(BF16),
                    kd=(k_c * jnp.exp(g_last - g_cols[x])).astype(BF16),
                    gl=jnp.exp(g_last)))
            items.append(dict(g=g, c=c, low=jnp.where(rc > cc, b_col2 * decay2 * kk2, 0.0),
                              qkd=(qk2 * decay2).astype(BF16), heads=heads))
    xe_scr[0:SUBLANES, :] = xe_scr[tb:tb + SUBLANES, :]

    invs = _unit_lower_inverse_pairs([it["low"] for it in items], chunk)
    for inv, it in zip(invs, items):
        ra, rb = it["heads"][0]["rhs"], it["heads"][1]["rhs"]
        zero = jnp.zeros_like(ra)
        rhs = jnp.concatenate([jnp.concatenate([ra, zero], axis=1),
                               jnp.concatenate([zero, rb], axis=1)], axis=0)
        it["sol"] = _dot(inv.astype(BF16), rhs)

    states = [s_scr[h] for h in range(2 * nh)]
    for c in range(n_chunks):
        r0, r1 = c * chunk, (c + 1) * chunk
        row = [it for it in items if it["c"] == c]
        sbs = [states[2 * it["g"] + x].astype(BF16) for it in row for x in range(2)]
        ws_qs = [_dot(jnp.concatenate([it["sol"][:, (2 * x + 1) * hd:(2 * x + 2) * hd].astype(BF16),
                                       it["heads"][x]["qg"]], axis=0), sbs[2 * i + x])
                 for i, it in enumerate(row) for x in range(2)]
        ubs = [(it["sol"][:, 2 * x * hd:(2 * x + 1) * hd] - ws_qs[2 * i + x][:chunk]).astype(BF16)
               for i, it in enumerate(row) for x in range(2)]
        intras = []
        for i, it in enumerate(row):
            zero = jnp.zeros_like(ubs[0])
            u_diag = jnp.concatenate([jnp.concatenate([ubs[2 * i], zero], axis=1),
                                      jnp.concatenate([zero, ubs[2 * i + 1]], axis=1)], axis=0)
            intras.append(_dot(it["qkd"], u_diag))
        for i, it in enumerate(row):
            for x in range(2):
                h = 2 * it["g"] + x
                head = it["heads"][x]
                states[h] = head["gl"] * states[h] + _dot_tn(head["kd"], ubs[2 * i + x])
                o = ws_qs[2 * i + x][chunk:] + intras[i][:, x * hd:(x + 1) * hd]
                z_c = z_ref[r0:r1, h * hd:(h + 1) * hd]
                on = o * lax.rsqrt(jnp.mean(o * o, axis=-1, keepdims=True) + NORM_EPS) * nw * _silu(z_c)
                o_ref[r0:r1, h * hd:(h + 1) * hd] = on.astype(o_ref.dtype)
    for h in range(2 * nh):
        s_scr[h] = states[h]

    @pl.when(t == pl.num_programs(2) - 1)
    def _():
        s_out_ref[...] = s_scr[...]


def _dn_core(proj, bg, bgt, conv_w, norm_w, bsz, seq, pool, page_table, layer):
    tb, hd, nh = DN_TBLOCK, HEAD_DIM, DN_HEADS_PER_STEP
    nt = seq // tb
    ng = DN_QK_HEADS // nh
    bd, n_pages = page_table.shape
    heads = pool.shape[-2]
    groups = n_pages // PAGES_PER_STEP
    rows = PAGES_PER_STEP * PAGE_SIZE // MB_BLOCK
    assert bd * groups <= bsz * ng * nt, "at most one page group per delta-rule grid step"
    kernel = functools.partial(_dn_core_kernel, tb=tb, chunk=DN_CHUNK, nh=nh)
    tok = lambda b, h, t: b * nt + t
    step = lambda b, h, t: jnp.minimum((b * ng + h) * nt + t, bd * groups - 1)

    def page_spec(i):
        def imap(b, h, t, pt):
            s = step(b, h, t)
            return (layer, pt[(s // groups) * n_pages + (s % groups) * PAGES_PER_STEP + i], 0, 0, 0)
        return pl.BlockSpec((None, None, PAGE_SIZE, heads, hd), imap)

    return pl.pallas_call(
        kernel,
        grid_spec=pltpu.PrefetchScalarGridSpec(
            num_scalar_prefetch=1,
            grid=(bsz, ng, nt),
            in_specs=[
                pl.BlockSpec((tb, nh * hd), lambda b, h, t, pt: (tok(b, h, t), h)),
                pl.BlockSpec((tb, nh * hd), lambda b, h, t, pt: (tok(b, h, t), ng + h)),
                pl.BlockSpec((tb, 2 * nh * hd), lambda b, h, t, pt: (tok(b, h, t), ng + h)),
                pl.BlockSpec((tb, 2 * nh * hd), lambda b, h, t, pt: (tok(b, h, t), 2 * ng + h)),
                pl.BlockSpec((tb, LANES), lambda b, h, t, pt: (tok(b, h, t), 0)),
                pl.BlockSpec((nh * SUBLANES, tb), lambda b, h, t, pt: (h, tok(b, h, t))),
                pl.BlockSpec((DN_CONV, nh * hd), lambda b, h, t, pt: (0, h)),
                pl.BlockSpec((DN_CONV, nh * hd), lambda b, h, t, pt: (0, ng + h)),
                pl.BlockSpec((DN_CONV, 2 * nh * hd), lambda b, h, t, pt: (0, ng + h)),
                pl.BlockSpec((1, hd), lambda b, h, t, pt: (0, 0)),
            ] + [page_spec(i) for i in range(PAGES_PER_STEP)],
            out_specs=[
                pl.BlockSpec((tb, 2 * nh * hd), lambda b, h, t, pt: (tok(b, h, t), h)),
                pl.BlockSpec((None, 2 * nh, hd, hd), lambda b, h, t, pt: (b, h, 0, 0)),
                pl.BlockSpec((None, rows, heads, hd),
                             lambda b, h, t, pt: (step(b, h, t) // groups, step(b, h, t) % groups, 0, 0)),
            ],
            scratch_shapes=[pltpu.VMEM((tb + SUBLANES, 4 * nh * hd), F32),
                            pltpu.VMEM((2 * nh, hd, hd), F32)],
        ),
        out_shape=[jax.ShapeDtypeStruct((bsz * seq, DN_VAL_DIM), BF16),
                   jax.ShapeDtypeStruct((bsz, DN_V_HEADS, hd, hd), F32),
                   jax.ShapeDtypeStruct((bd, groups * rows, heads, hd), F32)],
        compiler_params=_cparams("parallel", "parallel", "arbitrary"),
        name="dn_core",
    )(page_table.reshape(-1), proj, proj, proj, proj, bg, bgt, conv_w, conv_w, conv_w, norm_w,
      *([pool] * PAGES_PER_STEP))


def _dn_sample_kernel(p_ref, bg_ref, cst_ref, cw_ref, s_ref, nw_ref,
                      o_ref, s_out_ref, cst_out_ref, r_scr):
    hd = HEAD_DIM
    x = p_ref[:, 0:DN_CONV_DIM]
    buf = cst_ref[...]
    y = x * cw_ref[DN_CONV - 1:DN_CONV, :]
    for i in range(DN_CONV - 1):
        y = y + buf[i:i + 1, :] * cw_ref[i:i + 1, :]
    cst_out_ref[0:DN_CONV - 2, :] = buf[1:DN_CONV - 1, :]
    cst_out_ref[DN_CONV - 2:DN_CONV - 1, :] = x
    a = _silu(y)

    r_scr[...] = jnp.zeros(r_scr.shape, F32)
    qs, ks = [], []
    for h in range(DN_QK_HEADS):
        qh = a[:, h * hd:(h + 1) * hd]
        kh = a[:, DN_KEY_DIM + h * hd:DN_KEY_DIM + (h + 1) * hd]
        qh = qh * lax.rsqrt(jnp.sum(qh * qh, axis=-1, keepdims=True) + NORM_EPS) * (hd ** -0.5)
        kh = kh * lax.rsqrt(jnp.sum(kh * kh, axis=-1, keepdims=True) + NORM_EPS)
        qs.append(qh)
        ks.append(kh)
        r_scr[h:h + 1, :] = qh
        r_scr[DN_QK_HEADS + h:DN_QK_HEADS + h + 1, :] = kh
    rt = r_scr[...].T

    bg = bg_ref[...]
    nw = nw_ref[...]
    rep = DN_V_HEADS // DN_QK_HEADS
    for h in range(DN_V_HEADS):
        hq, x_in = h // rep, h % rep
        beta = bg[:, SUBLANES * hq + x_in:SUBLANES * hq + x_in + 1]
        g = bg[:, SUBLANES * hq + 2 + x_in:SUBLANES * hq + 3 + x_in]
        eg = jnp.exp(g)
        q_col = rt[:, hq:hq + 1]
        k_col = rt[:, DN_QK_HEADS + hq:DN_QK_HEADS + hq + 1]
        s_old = s_ref[h]
        k_s = jnp.sum(k_col * s_old, axis=0, keepdims=True)
        q_s = jnp.sum(q_col * s_old, axis=0, keepdims=True)
        v_h = a[:, 2 * DN_KEY_DIM + h * hd:2 * DN_KEY_DIM + (h + 1) * hd]
        u = beta * v_h - (beta * eg) * k_s
        qk = jnp.sum(qs[hq] * ks[hq], axis=-1, keepdims=True)
        o = eg * q_s + qk * u
        s_out_ref[h] = eg * s_old + k_col * u
        z_h = p_ref[:, DN_CONV_DIM + h * hd:DN_CONV_DIM + (h + 1) * hd]
        on = o * lax.rsqrt(jnp.mean(o * o, axis=-1, keepdims=True) + NORM_EPS) * nw * _silu(z_h)
        o_ref[:, h * hd:(h + 1) * hd] = on.astype(o_ref.dtype)


def _dn_sample(proj, bg, conv_state, conv_w, state, norm_w):
    bd = proj.shape[0]
    hd = HEAD_DIM
    o, s_new, c_new = pl.pallas_call(
        _dn_sample_kernel,
        grid=(bd,),
        in_specs=[
            pl.BlockSpec((None, 1, DN_MAIN_DIM), lambda b: (b, 0, 0)),
            pl.BlockSpec((None, 1, LANES), lambda b: (b, 0, 0)),
            pl.BlockSpec((None, DN_CONV - 1, DN_CONV_DIM), lambda b: (b, 0, 0)),
            pl.BlockSpec((DN_CONV, DN_CONV_DIM), lambda b: (0, 0)),
            pl.BlockSpec((None, DN_V_HEADS, hd, hd), lambda b: (b, 0, 0, 0)),
            pl.BlockSpec((1, hd), lambda b: (0, 0)),
        ],
        out_specs=[
            pl.BlockSpec((None, 1, DN_VAL_DIM), lambda b: (b, 0, 0)),
            pl.BlockSpec((None, DN_V_HEADS, hd, hd), lambda b: (b, 0, 0, 0)),
            pl.BlockSpec((None, DN_CONV - 1, DN_CONV_DIM), lambda b: (b, 0, 0)),
        ],
        out_shape=[jax.ShapeDtypeStruct((bd, 1, DN_VAL_DIM), BF16),
                   jax.ShapeDtypeStruct(state.shape, F32),
                   jax.ShapeDtypeStruct(conv_state.shape, F32)],
        scratch_shapes=[pltpu.VMEM((LANES, LANES), F32)],
        compiler_params=_cparams("parallel"),
        name="dn_sample",
    )(proj.reshape(bd, 1, DN_MAIN_DIM), bg.reshape(bd, 1, LANES), conv_state, conv_w, state, norm_w)
    return o.reshape(bd, DN_VAL_DIM), s_new, c_new


def _matmul_residual_kernel(a_ref, w_ref, y_ref, o_ref):
    o_ref[...] = y_ref[...] + _dot(a_ref[...], w_ref[...])


def _matmul_residual(a, w, y):
    m, k = a.shape
    n = w.shape[1]
    tm = min(512, m)
    return pl.pallas_call(
        _matmul_residual_kernel,
        grid=(m // tm,),
        in_specs=[pl.BlockSpec((tm, k), lambda i: (i, 0)),
                  pl.BlockSpec((k, n), lambda i: (0, 0)),
                  pl.BlockSpec((tm, n), lambda i: (i, 0))],
        out_specs=pl.BlockSpec((tm, n), lambda i: (i, 0)),
        out_shape=jax.ShapeDtypeStruct((m, n), F32),
        compiler_params=_cparams("parallel"),
        name="matmul_residual",
    )(a, w, y)


def _mb_qkv_kernel(x_ref, g_ref, w_ref, cos_ref, sin_ref, q_ref, k_ref, v_ref, *rest,
                   with_blocks, blocks_per_seq):
    if with_blocks:
        kb_ref, vt_ref, km_ref, h_scr = rest
    else:
        (h_scr,) = rest
    i, j = pl.program_id(0), pl.program_id(1)
    hd = HEAD_DIM

    @pl.when(j == 0)
    def _():
        h_scr[...] = _rms_rows(x_ref[...], g_ref[...]).astype(BF16)

    y = _dot(h_scr[...], w_ref[...])
    tm = y.shape[0]

    def rope(val):
        cos, sin = cos_ref[...], sin_ref[...]
        parts = []
        for h in range(MB_HEADS):
            yh = val[:, h * hd:(h + 1) * hd]
            parts.append(yh * cos + pltpu.roll(yh, hd // 2, axis=1) * sin)
        return jnp.concatenate(parts, axis=1)

    @pl.when(j == 0)
    def _():
        q_ref[...] = rope(y).astype(q_ref.dtype)

    @pl.when(j == 1)
    def _():
        k = rope(y)
        k_ref[...] = k
        if with_blocks:
            row = lax.broadcasted_iota(I32, (tm, LANES), 0)
            lane = lax.broadcasted_iota(I32, (tm, LANES), 1)
            first_blk = (i * (tm // MB_BLOCK)) % blocks_per_seq
            onehot = jnp.where(lane == first_blk + row // MB_BLOCK, 1.0, 0.0).astype(BF16)
            for h in range(MB_HEADS):
                kb_ref[:, 2 * h * hd:(2 * h + 1) * hd] = k[:, h * hd:(h + 1) * hd].astype(BF16)
                kb_ref[:, (2 * h + 1) * hd:(2 * h + 2) * hd] = onehot
            for r in range(tm // MB_BLOCK):
                blk = k[r * MB_BLOCK:(r + 1) * MB_BLOCK, :]
                km_ref[r] = jnp.sum(blk, axis=0, keepdims=True) * (1.0 / MB_BLOCK)

    @pl.when(j == 2)
    def _():
        v_ref[...] = y
        if with_blocks:
            ones = jnp.ones((MB_V_PAD, MB_BLOCK), BF16)
            rows = hd + MB_V_PAD
            for r in range(tm // MB_BLOCK):
                vt = y[r * MB_BLOCK:(r + 1) * MB_BLOCK, :].T.astype(BF16)
                for h in range(MB_HEADS):
                    vt_ref[r, h * rows:h * rows + hd, :] = vt[h * hd:(h + 1) * hd, :]
                    vt_ref[r, h * rows + hd:(h + 1) * rows, :] = ones


def _mb_qkv(x, g, w, cos, sin, *, with_blocks):
    m, d = x.shape
    tm = min(512, m)
    n_pos_tiles = cos.shape[0] // tm
    out_shape = [jax.ShapeDtypeStruct((m, d), BF16 if with_blocks else F32),
                 jax.ShapeDtypeStruct((m, d), F32), jax.ShapeDtypeStruct((m, d), F32)]
    row = pl.BlockSpec((tm, d), lambda i, j: (i, 0))
    out_specs = [row, row, row]
    if with_blocks:
        nb = tm // MB_BLOCK
        vt_rows = MB_HEADS * (HEAD_DIM + MB_V_PAD)
        out_shape += [jax.ShapeDtypeStruct((m, 2 * d), BF16),
                      jax.ShapeDtypeStruct((m // MB_BLOCK, vt_rows, MB_BLOCK), BF16),
                      jax.ShapeDtypeStruct((m // MB_BLOCK, 1, d), F32)]
        out_specs += [pl.BlockSpec((tm, 2 * d), lambda i, j: (i, 0)),
                      pl.BlockSpec((nb, vt_rows, MB_BLOCK), lambda i, j: (i, 0, 0)),
                      pl.BlockSpec((nb, 1, d), lambda i, j: (i, 0, 0))]
    return pl.pallas_call(
        functools.partial(_mb_qkv_kernel, with_blocks=with_blocks,
                          blocks_per_seq=cos.shape[0] // MB_BLOCK if with_blocks else 0),
        grid=(m // tm, 3),
        in_specs=[row,
                  pl.BlockSpec((1, d), lambda i, j: (0, 0)),
                  pl.BlockSpec((d, d), lambda i, j: (0, j)),
                  pl.BlockSpec((tm, HEAD_DIM), lambda i, j: (i % n_pos_tiles, 0)),
                  pl.BlockSpec((tm, HEAD_DIM), lambda i, j: (i % n_pos_tiles, 0))],
        out_specs=out_specs,
        out_shape=out_shape,
        scratch_shapes=[pltpu.VMEM((tm, d), BF16)],
        compiler_params=_cparams("parallel", "arbitrary"),
        name="mb_qkv",
    )(x, g, w, cos, sin)


def _mb_attn_kernel(q_ref, k_ref, vt_ref, km_ref, o_ref, *logit_bufs, heads):
    n = pl.program_id(2)
    bq, hd = MB_BLOCK, HEAD_DIM
    vrows = hd + MB_V_PAD
    c = (hd ** -0.5) * 1.4426950408889634
    nb = km_ref.shape[0]
    blk = lax.broadcasted_iota(I32, (nb, bq), 0)
    kpos = lax.broadcasted_iota(I32, (bq, bq), 0)
    qpos = lax.broadcasted_iota(I32, (bq, bq), 1)
    own_rows = pl.ds(pl.multiple_of(n * bq, bq), bq)

    qs = [q_ref[:, h * hd:(h + 1) * hd] for h in range(heads)]
    sts = [jnp.where(kpos <= qpos, _dot_nt(k_ref[own_rows, 2 * h * hd:(2 * h + 1) * hd], qs[h]), NEG_BIG)
           for h in range(heads)]
    scores = [_dot_nt(km_ref[:, h * hd:(h + 1) * hd].astype(BF16), qs[h]) for h in range(heads)]
    carry = []
    for h in range(heads):
        m = jnp.max(sts[h], axis=0, keepdims=True)
        p = jnp.exp2((sts[h] - m) * c)
        carry += [m, _dot(vt_ref[n, h * vrows:(h + 1) * vrows, :], p.astype(BF16))]

    qas = []
    for h in range(heads):
        s = jnp.where(blk < n, scores[h], -jnp.inf)
        bias = jnp.full((nb, bq), NEG_BIG, F32)
        for _ in range(MB_TOP_K):
            top = jnp.max(s, axis=0, keepdims=True)
            idx = jnp.min(jnp.where(s == top, blk, nb), axis=0, keepdims=True)
            hit = blk == idx
            bias = jnp.where(hit & (idx < n), 0.0, bias)
            s = jnp.where(hit, -jnp.inf, s)
        bias = jnp.concatenate([bias, jnp.full((LANES - nb, bq), NEG_BIG, F32)], axis=0)
        qas.append(jnp.concatenate([qs[h], bias.T.astype(BF16)], axis=1))

    sets_x = [logit_bufs[8 * h:8 * h + 4] for h in range(heads)]
    sets_y = [logit_bufs[8 * h + 4:8 * h + 8] for h in range(heads)]

    def fill(sets, j0):
        for h, bufs in enumerate(sets):
            for i, buf in enumerate(bufs):
                j = jnp.minimum(j0 + i, nb - 1)
                rows = pl.ds(pl.multiple_of(j * bq, bq), bq)
                buf[...] = _dot_nt(k_ref[rows, 2 * h * hd:(2 * h + 2) * hd], qas[h])

    def consume(sets, j0, carry):
        out = []
        for h, bufs in enumerate(sets):
            m, acc = carry[2 * h], carry[2 * h + 1]
            m_new = m
            for buf in bufs:
                m_new = jnp.maximum(m_new, jnp.max(buf[...], axis=0, keepdims=True))
            p = jnp.concatenate([jnp.exp2((buf[...] - m_new) * c).astype(BF16) for buf in bufs], axis=0)
            vt = jnp.concatenate([vt_ref[j0 + i, h * vrows:(h + 1) * vrows, :]
                                  for i in range(len(bufs))], axis=1)
            out += [m_new, acc * jnp.exp2((m - m_new) * c) + _dot(vt, p)]
        return tuple(out)

    def body(t, carry):
        j = 8 * t
        fill(sets_y, j + 4)
        carry = consume(sets_x, j, carry)
        fill(sets_x, j + 8)
        return consume(sets_y, j + 4, carry)

    half_trips = (n + 3) // 4
    fill(sets_x, 0)
    carry = lax.fori_loop(0, half_trips // 2, body, tuple(carry))
    carry = lax.cond(half_trips % 2 == 1,
                     lambda: consume(sets_x, 8 * (half_trips // 2), carry),
                     lambda: carry)
    for h in range(heads):
        acc = carry[2 * h + 1]
        o_ref[:, h * hd:(h + 1) * hd] = (acc[0:hd, :] / acc[hd:hd + 1, :]).T.astype(o_ref.dtype)


def _mb_attn(q, kb, vt, km, bsz, seq):
    hd, bq, heads = HEAD_DIM, MB_BLOCK, MB_HEADS_PER_STEP
    nb = seq // bq
    d = MB_HEADS * hd
    assert nb % 4 == 0
    return pl.pallas_call(
        functools.partial(_mb_attn_kernel, heads=heads),
        grid=(bsz, MB_HEADS // heads, nb),
        in_specs=[
            pl.BlockSpec((bq, heads * hd), lambda b, h, n: (b * nb + n, h)),
            pl.BlockSpec((seq, 2 * heads * hd), lambda b, h, n: (b, h)),
            pl.BlockSpec((nb, heads * (hd + MB_V_PAD), bq), lambda b, h, n: (b, h, 0)),
            pl.BlockSpec((nb, None, heads * hd), lambda b, h, n: (b, 0, h)),
        ],
        out_specs=pl.BlockSpec((bq, heads * hd), lambda b, h, n: (b * nb + n, h)),
        out_shape=jax.ShapeDtypeStruct((bsz * seq, d), BF16),
        scratch_shapes=[pltpu.VMEM((bq, bq), F32)] * (8 * heads),
        compiler_params=_cparams("parallel", "parallel", "arbitrary"),
        name="mb_attn",
    )(q, kb, vt, km)


def _mb_sample_select_kernel(q_ref, kn_ref, bs_ref, sel_ref, *, own):
    nb = bs_ref.shape[0]
    n_rows = nb + SUBLANES
    inv_blk = 1.0 / MB_BLOCK
    q = q_ref[...]
    sc = jnp.sum(bs_ref[...] * inv_blk * q[None], axis=-1, keepdims=True)
    own_sc = jnp.sum(kn_ref[...] * inv_blk * q, axis=-1, keepdims=True)
    tail_row = lax.broadcasted_iota(I32, (SUBLANES, MB_HEADS, 1), 0)
    tail = jnp.where(tail_row == 0, own_sc[None], -jnp.inf)
    blk = lax.broadcasted_iota(I32, (n_rows, MB_HEADS, 1), 0)
    s = jnp.where(blk < own, jnp.concatenate([sc, tail], axis=0), -jnp.inf)
    out_lane = lax.broadcasted_iota(I32, (MB_HEADS, LANES), 1)
    out = jnp.zeros((MB_HEADS, LANES), I32)
    for r in range(MB_TOP_K):
        m = jnp.max(s, axis=0, keepdims=True)
        idx = jnp.min(jnp.where(s == m, blk, n_rows), axis=0, keepdims=True)
        s = jnp.where(blk == idx, -jnp.inf, s)
        idx = idx[0]
        out = jnp.where(out_lane == r, idx, out)
        out = jnp.where(out_lane == MB_TOP_K + 1 + r, (idx < own).astype(I32), out)
    sel_ref[...] = out


def _mb_sample_select(q, k_new, blk_sums, own):
    bd = q.shape[0]
    nb = blk_sums.shape[1]
    return pl.pallas_call(
        functools.partial(_mb_sample_select_kernel, own=own),
        grid=(bd,),
        in_specs=[pl.BlockSpec((None, MB_HEADS, HEAD_DIM), lambda b: (b, 0, 0)),
                  pl.BlockSpec((None, MB_HEADS, HEAD_DIM), lambda b: (b, 0, 0)),
                  pl.BlockSpec((None, nb, MB_HEADS, HEAD_DIM), lambda b: (b, 0, 0, 0))],
        out_specs=pl.BlockSpec((None, MB_HEADS, LANES), lambda b: (b, 0, 0)),
        out_shape=jax.ShapeDtypeStruct((bd, MB_HEADS, LANES), I32),
        compiler_params=_cparams("parallel"),
        name="mb_sample_select",
    )(q, k_new, blk_sums)


def _mb_sample_attn_kernel(pt_ref, sel_ref, q_ref, kn_ref, vn_ref, pool_k, pool_v, o_ref,
                           kbuf, vbuf, sem, *, layer, n_pages):
    per_blk = MB_BLOCK // PAGE_SIZE
    n_pg = MB_TOP_K * per_blk
    b, h = pl.program_id(0), pl.program_id(1)
    n_heads = pl.num_programs(1)
    step = b * n_heads + h
    n_steps = pl.num_programs(0) * n_heads
    slot = step % MB_SAMPLE_SLOTS
    scale = HEAD_DIM ** -0.5

    def copies(bb, hh, sl):
        base = (bb * MB_HEADS + hh) * SUBLANES
        out = []
        for r in range(MB_TOP_K):
            blk = jnp.minimum(sel_ref[base + r], n_pages // per_blk - 1)
            for half in range(per_blk):
                i = r * per_blk + half
                page = pt_ref[bb * n_pages + blk * per_blk + half]
                out.append(pltpu.make_async_copy(pool_k.at[layer, page, :, hh, :], kbuf.at[sl, i],
                                                 sem.at[sl, 0, i]))
                out.append(pltpu.make_async_copy(pool_v.at[layer, page, :, hh, :], vbuf.at[sl, i],
                                                 sem.at[sl, 1, i]))
        return out

    ahead = MB_SAMPLE_SLOTS - 1

    def start_for(s):
        for cp in copies(s // n_heads, s % n_heads, s % MB_SAMPLE_SLOTS):
            cp.start()

    @pl.when(step == 0)
    def _():
        for s in range(ahead):
            pl.when(s < n_steps)(functools.partial(start_for, s))

    @pl.when(step + ahead < n_steps)
    def _():
        start_for(step + ahead)

    for cp in copies(b, h, slot):
        cp.wait()

    q = q_ref[...]
    q8 = jnp.broadcast_to(q, (SUBLANES, HEAD_DIM)).astype(BF16)
    keys = jnp.concatenate([kbuf[slot, i] for i in range(n_pg)], axis=0).astype(BF16)
    vals = jnp.concatenate([vbuf[slot, i] for i in range(n_pg)], axis=0).astype(BF16)
    lg = _dot_nt(q8, keys) * scale
    lane = lax.broadcasted_iota(I32, lg.shape, 1)
    base = (b * MB_HEADS + h) * SUBLANES
    for r in range(MB_TOP_K):
        penalty = jnp.where(sel_ref[base + MB_TOP_K + 1 + r] == 0, NEG_BIG, 0.0)
        in_blk = (lane >= r * MB_BLOCK) & (lane < (r + 1) * MB_BLOCK)
        lg = jnp.where(in_blk, lg + penalty, lg)
    qb = q.astype(BF16).astype(F32)
    lo = jnp.sum(qb * kn_ref[...].astype(BF16).astype(F32), axis=-1, keepdims=True) * scale
    m = jnp.maximum(jnp.max(lg, axis=-1, keepdims=True), lo)
    p = jnp.exp(lg - m)
    po = jnp.exp(lo - m)
    l = jnp.sum(p, axis=-1, keepdims=True) + po
    vn = vn_ref[...].astype(BF16).astype(F32)
    out = (_dot(p.astype(BF16), vals) + po.astype(BF16).astype(F32) * vn) / l
    o_ref[...] = out[0:1, :]


def _mb_sample_attn(pool_k, pool_v, page_table, sel, q, k_new, v_new, layer):
    bd, n_pages = page_table.shape
    hd = HEAD_DIM
    n_pg = MB_TOP_K * (MB_BLOCK // PAGE_SIZE)
    vec = pl.BlockSpec((None, None, 1, hd), lambda b, h, pt, sl: (b, h, 0, 0))
    hbm = pl.BlockSpec(memory_space=pl.ANY)
    q4 = q.reshape(bd, MB_HEADS, 1, hd)
    out = pl.pallas_call(
        functools.partial(_mb_sample_attn_kernel, layer=layer, n_pages=n_pages),
        grid_spec=pltpu.PrefetchScalarGridSpec(
            num_scalar_prefetch=2,
            grid=(bd, MB_HEADS),
            in_specs=[vec, vec, vec, hbm, hbm],
            out_specs=vec,
            scratch_shapes=[pltpu.VMEM((MB_SAMPLE_SLOTS, n_pg, PAGE_SIZE, hd), F32),
                            pltpu.VMEM((MB_SAMPLE_SLOTS, n_pg, PAGE_SIZE, hd), F32),
                            pltpu.SemaphoreType.DMA((MB_SAMPLE_SLOTS, 2, n_pg))],
        ),
        out_shape=jax.ShapeDtypeStruct((bd, MB_HEADS, 1, hd), F32),
        compiler_params=_cparams("arbitrary", "arbitrary"),
        name="mb_sample_attn",
    )(page_table.reshape(-1), sel[:, :, :SUBLANES].reshape(-1), q4, k_new.reshape(q4.shape), v_new.reshape(q4.shape),
      pool_k, pool_v)
    return out.reshape(bd, MB_HEADS * hd)


def _ffn_kernel(y_ref, g_ref, wg_ref, wv_ref, cwg_ref, cwv_ref, cbg_ref, cbv_ref, wo_ref,
                out_ref, bufg_ref, bufv_ref, h_scr, ug_scr, uv_scr, cg_scr, cv_scr, acc_scr,
                *, tiles_per_seq):
    i, f = pl.program_id(0), pl.program_id(1)
    tm = y_ref.shape[0]
    first = (i % tiles_per_seq) == 0

    @pl.when(f == 0)
    def _():
        h_scr[...] = _rms_rows(y_ref[...], g_ref[...]).astype(BF16)
        acc_scr[...] = jnp.zeros(acc_scr.shape, F32)

    @pl.when(first)
    def _():
        cg_scr[f] = jnp.zeros(cg_scr.shape[1:], F32)
        cv_scr[f] = jnp.zeros(cv_scr.shape[1:], F32)

    def branch(w_ref, cw_ref, cb_ref, u_scr, carry, buf_ref):
        u = _dot(h_scr[...], w_ref[...])
        u_scr[0:SUBLANES, :] = carry[f]
        u_scr[SUBLANES:SUBLANES + tm, :] = u
        r = cw_ref[FFN_CONV - 1:FFN_CONV, :] * u + cb_ref[...]
        for d in range(1, FFN_CONV):
            r = r + cw_ref[FFN_CONV - 1 - d:FFN_CONV - d, :] * u_scr[SUBLANES - d:SUBLANES - d + tm, :]
        last = u_scr[tm:tm + SUBLANES, :]
        carry[f] = last
        buf_ref[...] = last
        return r

    gate = branch(wg_ref, cwg_ref, cbg_ref, ug_scr, cg_scr, bufg_ref)
    val = branch(wv_ref, cwv_ref, cbv_ref, uv_scr, cv_scr, bufv_ref)
    act = (_silu(gate) * val).astype(BF16)
    acc_scr[...] += _dot(act, wo_ref[...])

    @pl.when(f == pl.num_programs(1) - 1)
    def _():
        out_ref[...] = y_ref[...] + acc_scr[...]


def _ffn_tile(dff):
    for cand in (1408, 1024, 512, 256, 128):
        if dff % cand == 0:
            return cand
    raise ValueError(f"unsupported d_ff {dff}")


def _ffn_prompt(y, g, w_in, conv_w, conv_b, w_out, bsz, seq):
    m, d = y.shape
    dff = w_out.shape[0]
    tm = min(512, seq)
    tf = _ffn_tile(dff)
    nf = dff // tf
    tiles_per_seq = seq // tm
    gate = lambda i, f: (0, f)
    val = lambda i, f: (0, f + nf)
    buf_spec = pl.BlockSpec((None, SUBLANES, tf), lambda i, f: (i, 0, f))
    out, bufg, bufv = pl.pallas_call(
        functools.partial(_ffn_kernel, tiles_per_seq=tiles_per_seq),
        grid=(m // tm, nf),
        in_specs=[pl.BlockSpec((tm, d), lambda i, f: (i, 0)),
                  pl.BlockSpec((1, d), lambda i, f: (0, 0)),
                  pl.BlockSpec((d, tf), gate), pl.BlockSpec((d, tf), val),
                  pl.BlockSpec((FFN_CONV, tf), gate), pl.BlockSpec((FFN_CONV, tf), val),
                  pl.BlockSpec((1, tf), gate), pl.BlockSpec((1, tf), val),
                  pl.BlockSpec((tf, d), lambda i, f: (f, 0))],
        out_specs=[pl.BlockSpec((tm, d), lambda i, f: (i, 0)), buf_spec, buf_spec],
        out_shape=[jax.ShapeDtypeStruct((m, d), F32),
                   jax.ShapeDtypeStruct((m // tm, SUBLANES, dff), F32),
                   jax.ShapeDtypeStruct((m // tm, SUBLANES, dff), F32)],
        scratch_shapes=[pltpu.VMEM((tm, d), BF16),
                        pltpu.VMEM((tm + SUBLANES, tf), F32), pltpu.VMEM((tm + SUBLANES, tf), F32),
                        pltpu.VMEM((nf, SUBLANES, tf), F32), pltpu.VMEM((nf, SUBLANES, tf), F32),
                        pltpu.VMEM((tm, d), F32)],
        compiler_params=_cparams("arbitrary", "arbitrary"),
        name="ffn_prompt",
    )(y, g, w_in, w_in, conv_w, conv_w, conv_b, conv_b, w_out)
    keep = FFN_CONV - 1
    last = slice(tiles_per_seq - 1, None, tiles_per_seq)
    new_buf = jnp.concatenate([bufg[last, SUBLANES - keep:], bufv[last, SUBLANES - keep:]], axis=-1)
    return out, new_buf


def _ffn_sample_kernel(y_ref, g_ref, wg_ref, wv_ref, cwg_ref, cwv_ref, cbg_ref, cbv_ref,
                       sg_ref, sv_ref, wo_ref, out_ref, ng_ref, nv_ref, h_scr, acc_scr):
    f = pl.program_id(0)

    @pl.when(f == 0)
    def _():
        h_scr[...] = _rms_rows(y_ref[...], g_ref[...]).astype(BF16)
        acc_scr[...] = jnp.zeros(acc_scr.shape, F32)

    def branch(w_ref, cw_ref, cb_ref, st_ref, new_ref):
        u = _dot(h_scr[...], w_ref[...])
        r = cw_ref[FFN_CONV - 1:FFN_CONV, :] * u + cb_ref[...]
        for i in range(FFN_CONV - 1):
            r = r + cw_ref[i:i + 1, :] * st_ref[i]
        for i in range(FFN_CONV - 2):
            new_ref[i] = st_ref[i + 1]
        new_ref[FFN_CONV - 2] = u
        return r

    gate = branch(wg_ref, cwg_ref, cbg_ref, sg_ref, ng_ref)
    val = branch(wv_ref, cwv_ref, cbv_ref, sv_ref, nv_ref)
    acc_scr[...] += _dot((_silu(gate) * val).astype(BF16), wo_ref[...])

    @pl.when(f == pl.num_programs(0) - 1)
    def _():
        out_ref[...] = y_ref[...] + acc_scr[...]


def _ffn_sample(y, g, w_in, conv_w, conv_b, w_out, state):
    bd, d = y.shape
    dff = w_out.shape[0]
    tf = _ffn_tile(dff)
    nf = dff // tf
    keep = FFN_CONV - 1
    st = jnp.transpose(state, (1, 0, 2))
    gate = lambda f: (0, f)
    val = lambda f: (0, f + nf)
    st_gate = pl.BlockSpec((keep, bd, tf), lambda f: (0, 0, f))
    st_val = pl.BlockSpec((keep, bd, tf), lambda f: (0, 0, f + nf))
    new_spec = pl.BlockSpec((keep, bd, tf), lambda f: (0, 0, f))
    out, ng, nv = pl.pallas_call(
        _ffn_sample_kernel,
        grid=(nf,),
        in_specs=[pl.BlockSpec((bd, d), lambda f: (0, 0)),
                  pl.BlockSpec((1, d), lambda f: (0, 0)),
                  pl.BlockSpec((d, tf), gate), pl.BlockSpec((d, tf), val),
                  pl.BlockSpec((FFN_CONV, tf), gate), pl.BlockSpec((FFN_CONV, tf), val),
                  pl.BlockSpec((1, tf), gate), pl.BlockSpec((1, tf), val),
                  st_gate, st_val,
                  pl.BlockSpec((tf, d), lambda f: (f, 0))],
        out_specs=[pl.BlockSpec((bd, d), lambda f: (0, 0)), new_spec, new_spec],
        out_shape=[jax.ShapeDtypeStruct((bd, d), F32),
                   jax.ShapeDtypeStruct((keep, bd, dff), F32),
                   jax.ShapeDtypeStruct((keep, bd, dff), F32)],
        scratch_shapes=[pltpu.VMEM((bd, d), BF16), pltpu.VMEM((bd, d), F32)],
        compiler_params=_cparams("arbitrary"),
        name="ffn_sample",
    )(y, g, w_in, w_in, conv_w, conv_w, conv_b, conv_b, st, st, w_out)
    new_state = jnp.transpose(jnp.concatenate([ng, nv], axis=-1), (1, 0, 2))
    return out, new_state


def _final_norm_kernel(x_ref, g_ref, o_ref):
    o_ref[...] = _rms_rows(x_ref[...], g_ref[...])


def _final_norm(x, g):
    m, d = x.shape
    tm = min(1024, m)
    return pl.pallas_call(
        _final_norm_kernel,
        grid=(m // tm,),
        in_specs=[pl.BlockSpec((tm, d), lambda i: (i, 0)), pl.BlockSpec((1, d), lambda i: (0, 0))],
        out_specs=pl.BlockSpec((tm, d), lambda i: (i, 0)),
        out_shape=jax.ShapeDtypeStruct((m, d), F32),
        compiler_params=_cparams("parallel"),
        name="final_norm",
    )(x, g)


def _rope_tables(pos):
    half = HEAD_DIM // 2
    inv = ROPE_THETA ** (-jnp.arange(half, dtype=F32) / half)
    ang = pos.astype(F32)[:, None] * inv[None, :]
    cos, sin = jnp.cos(ang), jnp.sin(ang)
    return jnp.concatenate([cos, cos], axis=-1), jnp.concatenate([-sin, sin], axis=-1)


def _dn_gate_layout(w_in, a_log, dt_bias):
    rep = DN_V_HEADS // DN_QK_HEADS
    n_gate = 2 * DN_V_HEADS
    src = np.full((LANES,), n_gate, np.int32)
    for hq in range(DN_QK_HEADS):
        for x in range(rep):
            src[SUBLANES * hq + x] = rep * hq + x
            src[SUBLANES * hq + rep + x] = DN_V_HEADS + rep * hq + x
    is_decay = (src >= DN_V_HEADS) & (src < n_gate)
    gates_t = w_in[:, DN_MAIN_DIM:].T
    w_bg = jnp.take(gates_t, src, axis=0, mode="fill", fill_value=0).T.astype(BF16)
    dec_src = np.where(is_decay, src - DN_V_HEADS, DN_V_HEADS)
    neg_a = jnp.take(-jnp.exp(a_log.astype(F32)), dec_src, mode="fill", fill_value=0)
    dt_b = jnp.take(dt_bias.astype(F32), dec_src, mode="fill", fill_value=0)
    is_beta = jnp.asarray((src < DN_V_HEADS).astype(np.float32))
    par = jnp.stack([neg_a, dt_b, is_beta])
    par = jnp.pad(par, ((0, SUBLANES - par.shape[0]), (0, 0)))
    return w_bg, par


def kernel(x_prompt, x_sample, cache_k, cache_v, state_dn, state_dn_conv, state_ffn_conv, page_table,
           norm_mix, norm_ffn, norm_out, dn_w_in, dn_conv_w, dn_a_log, dn_dt_bias, dn_norm, dn_w_out,
           mb_w_qkv, mb_w_o, ffn_w_in, ffn_conv_w, ffn_conv_b, ffn_w_out):
    bsz, seq, d = x_prompt.shape
    bd = x_sample.shape[0]
    depth = norm_mix.shape[0]
    n_pages = page_table.shape[1]
    past = n_pages * PAGE_SIZE
    own_blk = past // MB_BLOCK
    hd = HEAD_DIM

    yp = x_prompt.reshape(bsz * seq, d)
    ys = x_sample.reshape(bd, d)
    cos_p, sin_p = _rope_tables(jnp.arange(seq, dtype=I32))
    cos_s, sin_s = _rope_tables(jnp.full((bd,), past, dtype=I32))
    pool_k, pool_v = cache_k, cache_v

    kp_l, vp_l, ks_l, vs_l = [], [], [], []
    sp_l, ss_l, cp_l, cs_l = [], [], [], []
    fp_l, fs_l = [], []
    for layer in range(depth):
        j = layer // 2
        g_mix = norm_mix[layer].reshape(1, d)
        if layer % 2 == 0:
            w_main = dn_w_in[j][:, :DN_MAIN_DIM].astype(BF16)
            w_bg, par = _dn_gate_layout(dn_w_in[j], dn_a_log[j], dn_dt_bias[j])
            w_out = dn_w_out[j].astype(BF16)
            nw = dn_norm[j].reshape(1, hd)
            proj, bg, bgt = _dn_inproj(yp, g_mix, w_main, w_bg, par, transposed=True)
            o, s_new, blk_sums = _dn_core(proj, bg, bgt, dn_conv_w[j], nw, bsz, seq, pool_k, page_table, j)
            yp = _matmul_residual(o, w_out, yp)
            sp_l.append(s_new)
            cp_l.append(proj.reshape(bsz, seq, DN_MAIN_DIM)[:, seq - (DN_CONV - 1):, :DN_CONV_DIM])

            proj_s, bg_s = _dn_inproj(ys, g_mix, w_main, w_bg, par, transposed=False)
            o_s, s_s, c_s = _dn_sample(proj_s, bg_s, state_dn_conv[j], dn_conv_w[j], state_dn[j], nw)
            ys = _matmul_residual(o_s, w_out, ys)
            ss_l.append(s_s)
            cs_l.append(c_s)
        else:
            w_qkv = mb_w_qkv[j].astype(BF16)
            w_o = mb_w_o[j].astype(BF16)
            q, k, v, kb, vt, km = _mb_qkv(yp, g_mix, w_qkv, cos_p, sin_p, with_blocks=True)
            att = _mb_attn(q, kb, vt, km, bsz, seq)
            yp = _matmul_residual(att, w_o, yp)
            kp_l.append(k.reshape(bsz, seq, MB_HEADS, hd))
            vp_l.append(v.reshape(bsz, seq, MB_HEADS, hd))

            q_s, k_s, v_s = _mb_qkv(ys, g_mix, w_qkv, cos_s, sin_s, with_blocks=False)
            q3 = q_s.reshape(bd, MB_HEADS, hd)
            k3 = k_s.reshape(bd, MB_HEADS, hd)
            sel = _mb_sample_select(q3, k3, blk_sums, own_blk)
            att_s = _mb_sample_attn(pool_k, pool_v, page_table, sel, q3, k3, v_s.reshape(bd, MB_HEADS, hd), j)
            ys = _matmul_residual(att_s.astype(BF16), w_o, ys)
            ks_l.append(k_s.reshape(bd, 1, MB_HEADS, hd))
            vs_l.append(v_s.reshape(bd, 1, MB_HEADS, hd))

        g_ffn = norm_ffn[layer].reshape(1, d)
        w_fi = ffn_w_in[layer].astype(BF16)
        w_fo = ffn_w_out[layer].astype(BF16)
        cb = ffn_conv_b[layer].reshape(1, -1)
        yp, fbp = _ffn_prompt(yp, g_ffn, w_fi, ffn_conv_w[layer], cb, w_fo, bsz, seq)
        ys, fbs = _ffn_sample(ys, g_ffn, w_fi, ffn_conv_w[layer], cb, w_fo, state_ffn_conv[layer])
        fp_l.append(fbp)
        fs_l.append(fbs)

    g_out = norm_out.reshape(1, d)
    y_prompt = _final_norm(yp, g_out).reshape(bsz, seq, d)
    y_sample = _final_norm(ys, g_out).reshape(bd, 1, d)
    return (y_prompt, y_sample, jnp.stack(kp_l), jnp.stack(vp_l), jnp.stack(ks_l), jnp.stack(vs_l),
            jnp.stack(sp_l), jnp.stack(ss_l), jnp.stack(cp_l), jnp.stack(cs_l), jnp.stack(fp_l), jnp.stack(fs_l))
```
